```python
import jax
import jax.numpy as jnp
from jax import lax
import numpy as np

D_MODEL = 1024
BATCH = 8
SEQ = 4096
DEPTH = 1
DEC_BATCH = 128
DEC_SEQ = 1
PAST_LEN = 8192
PAGE_SIZE = 128

HEAD_DIM = 64
RET_WIDTH = D_MODEL // 2
RET_HEADS = RET_WIDTH // HEAD_DIM
NSA_WIDTH = D_MODEL - RET_WIDTH
NSA_HEADS = NSA_WIDTH // HEAD_DIM
NSA_KV_HEADS = 2
NSA_GROUP = NSA_HEADS // NSA_KV_HEADS
KV_WIDTH = NSA_KV_HEADS * HEAD_DIM
MIX_WIDTH = RET_WIDTH + NSA_WIDTH
N_BRANCH = 3
IN_SPLITS = (RET_WIDTH, RET_WIDTH, RET_WIDTH, RET_WIDTH, NSA_WIDTH,
             KV_WIDTH, KV_WIDTH, KV_WIDTH, KV_WIDTH, KV_WIDTH, KV_WIDTH,
             NSA_HEADS * N_BRANCH)
IN_WIDTH = 4 * RET_WIDTH + NSA_WIDTH + 6 * KV_WIDTH + NSA_HEADS * N_BRANCH
D_FF = 4 * D_MODEL
CMP_BLOCK = 64
CMP_HIDDEN = HEAD_DIM
N_SELECT = 16
WINDOW = 512
Q_BLOCK = 128
RET_CHUNK = 128
ROPE_BASE = 10000.0
EPS = 1e-6
NEG = -1e30
SCALE = HEAD_DIM ** -0.5

kernel_name = 'hymba_retnet_nsa_step'


def rmsnorm(x, w):
    xf = x.astype(jnp.float32)
    y = xf * lax.rsqrt(jnp.mean(xf * xf, axis=-1, keepdims=True) + EPS)
    return (y * w).astype(x.dtype)


def rope(x, pos):
    half = HEAD_DIM // 2
    inv = ROPE_BASE ** (-jnp.arange(half, dtype=jnp.float32) / half)
    ang = pos.astype(jnp.float32)[:, None] * inv
    cos, sin = jnp.cos(ang)[:, None, :], jnp.sin(ang)[:, None, :]
    x1, x2 = x[..., :half].astype(jnp.float32), x[..., half:].astype(jnp.float32)
    return jnp.concatenate([x1 * cos - x2 * sin, x1 * sin + x2 * cos], -1).astype(x.dtype)


def mix_inputs(x, pos, ln_w, w_in):
    B, T, _ = x.shape
    h = rmsnorm(x, ln_w)
    parts = jnp.split(h @ w_in, np.cumsum(IN_SPLITS)[:-1].tolist(), axis=-1)
    rq, rk, rv, rg, nq, ck, cv, sk, sv, wk, wv, gt = parts
    rq = rope(rq.reshape(B, T, RET_HEADS, HEAD_DIM), pos)
    rk = rope(rk.reshape(B, T, RET_HEADS, HEAD_DIM), pos) * SCALE
    rv = rv.reshape(B, T, RET_HEADS, HEAD_DIM)
    nq = nq.reshape(B, T, NSA_KV_HEADS, NSA_GROUP, HEAD_DIM)
    kv = [t.reshape(B, T, NSA_KV_HEADS, HEAD_DIM) for t in (ck, cv, sk, sv, wk, wv)]
    gates = jax.nn.sigmoid(gt.astype(jnp.float32)).reshape(
        B, T, NSA_KV_HEADS, NSA_GROUP, N_BRANCH).astype(x.dtype)
    return (rq, rk, rv, rg, nq, gates, kv[0], kv[1], kv[2], kv[3], kv[4], kv[5])


def retention_chunkwise(q, k, v, s0, chunk):
    B, T, H, d = q.shape
    n = T // chunk
    lg = jnp.log1p(-jnp.exp2(-5.0 - jnp.arange(H, dtype=jnp.float32)))
    i = jnp.arange(chunk, dtype=jnp.float32)
    diff = i[:, None] - i[None, :]
    inner = jnp.where(diff >= 0, jnp.exp(lg[:, None, None] * jnp.maximum(diff, 0.0)), 0.0)
    q_dec = jnp.exp(lg[None, :] * (i[:, None] + 1.0))[None, :, :, None]
    k_dec = jnp.exp(lg[None, :] * (chunk - 1.0 - i[:, None]))[None, :, :, None]
    c_dec = jnp.exp(lg * chunk)[None, :, None, None]

    def blocks(t):
        return t.astype(jnp.float32).reshape(B, n, chunk, H, d).swapaxes(0, 1)

    def step(s, xs):
        qc, kc, vc = xs
        a = jnp.einsum('bihd,bjhd->bhij', qc, kc) * inner
        o = jnp.einsum('bhij,bjhd->bihd', a, vc) + jnp.einsum('bihd,bhde->bihe', qc, s) * q_dec
        s = s * c_dec + jnp.einsum('bjhd,bjhe->bhde', kc * k_dec, vc)
        return s, o

    s, o = lax.scan(step, s0.astype(jnp.float32), (blocks(q), blocks(k), blocks(v)))
    return o.swapaxes(0, 1).reshape(B, T, H, d).astype(q.dtype), s.astype(s0.dtype)


def retention_out(o, g, gn_w):
    B, T, H, d = o.shape
    of = o.astype(jnp.float32)
    mu = jnp.mean(of, -1, keepdims=True)
    var = jnp.mean(jnp.square(of - mu), -1, keepdims=True)
    y = ((of - mu) * lax.rsqrt(var + EPS)).reshape(B, T, H * d) * gn_w
    return (jax.nn.silu(g.astype(jnp.float32)) * y).astype(o.dtype)


def compress(rows, pos_emb, w1, w2):
    B, T, G, d = rows.shape
    nb = T // CMP_BLOCK
    blk = rows[:, :nb * CMP_BLOCK].reshape(B, nb, CMP_BLOCK, G, d) + pos_emb[:, None, :]
    h = jax.nn.gelu(jnp.einsum('bnlgd,ldf->bngf', blk, w1))
    return jnp.einsum('bngf,fd->bngd', h, w2)


def cmp_attend(q, kc, vc, pos_q):
    nb = kc.shape[1]
    s = jnp.einsum('bqgrd,bngd->bqgrn', q, kc).astype(jnp.float32) * SCALE
    valid = ((jnp.arange(nb) + 1) * CMP_BLOCK - 1)[None, :] <= pos_q[:, None]
    vm = valid[None, :, None, None, :]
    p = jnp.where(vm, jax.nn.softmax(jnp.where(vm, s, NEG), axis=-1), 0.0)
    o = jnp.einsum('bqgrn,bngd->bqgrd', p.astype(vc.dtype), vc)
    return o, p


def select_blocks(p, pos_q):
    nb = p.shape[-1]
    imp = jnp.sum(p, axis=3)
    cur = (pos_q // CMP_BLOCK).astype(jnp.int32)
    cand = jnp.arange(nb)[None, :] < cur[:, None]
    imp = jnp.where(cand[None, :, None, :], imp, -1.0)
    val, idx = lax.top_k(imp, min(N_SELECT - 1, nb))
    cur_b = jnp.broadcast_to(cur[None, :, None, None], idx.shape[:-1] + (1,))
    idx = jnp.concatenate([cur_b, idx.astype(jnp.int32)], -1)
    ok = jnp.concatenate([jnp.ones(cur_b.shape, dtype=bool), val >= 0.0], -1)
    return idx, ok


def sel_attend(q, kg, vg, pos_q, tok, ok):
    s = jnp.einsum('...qgrd,...qgskd->...qgrsk', q, kg).astype(jnp.float32) * SCALE
    mask = jnp.expand_dims(ok[..., None] & (tok <= pos_q[:, None, None, None]), -3)
    s = jnp.where(mask, s, NEG)
    p = jax.nn.softmax(s.reshape(s.shape[:-2] + (-1,)), axis=-1).reshape(s.shape)
    return jnp.einsum('...qgrsk,...qgskd->...qgrd', p.astype(vg.dtype), vg)


def sel_prompt(q, k_rows, v_rows, idx, ok):
    B, T, G, R, d = q.shape
    nq = T // Q_BLOCK
    S = idx.shape[-1]
    g_ix = jnp.arange(G)[None, :, None, None]
    offs = jnp.arange(CMP_BLOCK, dtype=jnp.int32)

    def one(args):
        i, qi, ii, oi = args
        b = i // nq
        pos_q = (i % nq) * Q_BLOCK + jnp.arange(Q_BLOCK, dtype=jnp.int32)
        tok = ii[..., None] * CMP_BLOCK + offs
        kb = k_rows[b][tok, g_ix]
        vb = v_rows[b][tok, g_ix]
        return sel_attend(qi, kb, vb, pos_q, tok, oi)

    out = lax.map(one, (jnp.arange(B * nq), q.reshape(B * nq, Q_BLOCK, G, R, d),
                        idx.reshape(B * nq, Q_BLOCK, G, S), ok.reshape(B * nq, Q_BLOCK, G, S)))
    return out.reshape(B, T, G, R, d)


def sel_sample(q, pool_k, pool_v, page_table, new_k, new_v, idx, ok, pos_q):
    DB, Qn, G, _ = new_k.shape
    past_len = page_table.shape[1] * PAGE_SIZE
    tok = idx[..., None] * CMP_BLOCK + jnp.arange(CMP_BLOCK, dtype=jnp.int32)
    b_ix = jnp.arange(DB)[:, None, None, None, None]
    g_ix = jnp.arange(G)[None, None, :, None, None]
    tp = jnp.minimum(tok, past_len - 1)
    phys = page_table[b_ix, tp // PAGE_SIZE]
    tn = jnp.clip(tok - past_len, 0, Qn - 1)
    is_past = (tok < past_len)[..., None]
    kg = jnp.where(is_past, pool_k[phys, tp % PAGE_SIZE, g_ix], new_k[b_ix, tn, g_ix])
    vg = jnp.where(is_past, pool_v[phys, tp % PAGE_SIZE, g_ix], new_v[b_ix, tn, g_ix])
    return sel_attend(q, kg, vg, pos_q, tok, ok)


def window_attend(q, k, v, pos_q, pos_k):
    s = jnp.einsum('...qgrd,...kgd->...qgrk', q, k).astype(jnp.float32) * SCALE
    dpos = pos_q[..., :, None] - pos_k[..., None, :]
    mask = (dpos >= 0) & (dpos < WINDOW) & (pos_k[..., None, :] >= 0)
    s = jnp.where(mask[..., :, None, None, :], s, NEG)
    p = jax.nn.softmax(s, axis=-1)
    return jnp.einsum('...qgrk,...kgd->...qgrd', p.astype(v.dtype), v)


def window_prompt(q, k, v):
    B, T, G, R, d = q.shape
    nq = T // Q_BLOCK
    nw = WINDOW // Q_BLOCK
    kl = (nw + 1) * Q_BLOCK
    bi = jnp.arange(nq)[:, None] + jnp.arange(nw + 1)[None, :]
    pos_q = jnp.arange(T, dtype=jnp.int32).reshape(nq, Q_BLOCK)
    pos_k = (jnp.arange(nq, dtype=jnp.int32)[:, None] - nw) * Q_BLOCK + jnp.arange(kl, dtype=jnp.int32)[None, :]

    def band(r):
        rp = jnp.pad(r, ((nw * Q_BLOCK, 0), (0, 0), (0, 0))).reshape(nq + nw, Q_BLOCK, G, d)
        return rp[bi].reshape(nq, kl, G, d)

    def one(args):
        qs, ks, vs = args
        return window_attend(qs.reshape(nq, Q_BLOCK, G, R, d), band(ks), band(vs), pos_q, pos_k)

    return lax.map(one, (q, k, v)).reshape(B, T, G, R, d)


def window_sample(q, buf_k, buf_v, new_k, new_v, pos_q, past_len):
    wb = buf_k.shape[1]
    qn = new_k.shape[1]
    k = jnp.concatenate([buf_k, new_k], 1)
    v = jnp.concatenate([buf_v, new_v], 1)
    pos_k = jnp.concatenate([past_len - wb + jnp.arange(wb, dtype=jnp.int32),
                             past_len + jnp.arange(qn, dtype=jnp.int32)])
    return window_attend(q, k, v, pos_q, pos_k), k[:, -wb:], v[:, -wb:]


def gather_pages(pool, page_table):
    DB, NP = page_table.shape
    return pool[page_table].reshape(DB, NP * PAGE_SIZE, NSA_KV_HEADS, HEAD_DIM)


def nsa_combine(gates, o_cmp, o_sel, o_win):
    B, T = o_cmp.shape[:2]
    o = gates[..., 0:1] * o_cmp + gates[..., 1:2] * o_sel + gates[..., 2:3] * o_win
    return o.reshape(B, T, NSA_WIDTH)


def finish(x, ret_y, nsa_y, w_o, ln_post_mix, ln_pre_ffn, w_up, w_down, ln_post_ffn):
    mix = jnp.concatenate([ret_y, nsa_y], -1) @ w_o
    h = x + rmsnorm(mix, ln_post_mix)
    f = jnp.square(jax.nn.relu(rmsnorm(h, ln_pre_ffn) @ w_up)) @ w_down
    return h + rmsnorm(f, ln_post_ffn)


def setup_inputs(seed: int = 0) -> dict:
    key = jax.random.key(seed)
    ks = jax.random.split(key, 32)
    f32 = jnp.float32
    n_pages = PAST_LEN // PAGE_SIZE
    n_phys = (DEC_BATCH * n_pages * 5) // 4
    win_len = min(WINDOW, PAST_LEN)

    def nrm(k, shape, s):
        return jax.random.normal(k, shape, f32) * s

    def gain(k, n):
        return 1.0 + 0.05 * jax.random.normal(k, (DEPTH, n), f32)

    page_table = jax.random.permutation(ks[0], n_phys)[:DEC_BATCH * n_pages].reshape(
        DEC_BATCH, n_pages).astype(jnp.int32)
    pool = (DEPTH, n_phys, PAGE_SIZE, NSA_KV_HEADS, HEAD_DIM)
    wbuf = (DEPTH, DEC_BATCH, win_len, NSA_KV_HEADS, HEAD_DIM)
    return {
        'x_prompt': nrm(ks[1], (BATCH, SEQ, D_MODEL), 1.0),
        'x_sample': nrm(ks[2], (DEC_BATCH, DEC_SEQ, D_MODEL), 1.0),
        'state_ret': nrm(ks[3], (DEPTH, DEC_BATCH, RET_HEADS, HEAD_DIM, HEAD_DIM), 1.0),
        'cache_cmp_k': nrm(ks[4], pool, 1.0),
        'cache_cmp_v': nrm(ks[5], pool, 1.0),
        'cache_sel_k': nrm(ks[6], pool, 1.0),
        'cache_sel_v': nrm(ks[7], pool, 1.0),
        'cache_win_k': nrm(ks[8], wbuf, 1.0),
        'cache_win_v': nrm(ks[9], wbuf, 1.0),
        'page_table': page_table,
        'ln_pre_mix': gain(ks[10], D_MODEL),
        'w_in': nrm(ks[11], (DEPTH, D_MODEL, IN_WIDTH), D_MODEL ** -0.5),
        'ret_gn_w': gain(ks[12], RET_WIDTH),
        'cmp_pos_k': nrm(ks[13], (DEPTH, CMP_BLOCK, HEAD_DIM), 0.1),
        'cmp_w1_k': nrm(ks[14], (DEPTH, CMP_BLOCK, HEAD_DIM, CMP_HIDDEN), (CMP_BLOCK * HEAD_DIM) ** -0.5),
        'cmp_w2_k': nrm(ks[15], (DEPTH, CMP_HIDDEN, HEAD_DIM), CMP_HIDDEN ** -0.5),
        'cmp_pos_v': nrm(ks[16], (DEPTH, CMP_BLOCK, HEAD_DIM), 0.1),
        'cmp_w1_v': nrm(ks[17], (DEPTH, CMP_BLOCK, HEAD_DIM, CMP_HIDDEN), (CMP_BLOCK * HEAD_DIM) ** -0.5),
        'cmp_w2_v': nrm(ks[18], (DEPTH, CMP_HIDDEN, HEAD_DIM), CMP_HIDDEN ** -0.5),
        'w_o': nrm(ks[19], (DEPTH, MIX_WIDTH, D_MODEL), MIX_WIDTH ** -0.5),
        'ln_post_mix': gain(ks[20], D_MODEL),
        'ln_pre_ffn': gain(ks[21], D_MODEL),
        'w_up': nrm(ks[22], (DEPTH, D_MODEL, D_FF), D_MODEL ** -0.5),
        'w_down': nrm(ks[23], (DEPTH, D_FF, D_MODEL), D_FF ** -0.5),
        'ln_post_ffn': gain(ks[24], D_MODEL),
    }


def reference(x_prompt, x_sample, state_ret, cache_cmp_k, cache_cmp_v, cache_sel_k, cache_sel_v,
              cache_win_k, cache_win_v, page_table, ln_pre_mix, w_in, ret_gn_w,
              cmp_pos_k, cmp_w1_k, cmp_w2_k, cmp_pos_v, cmp_w1_v, cmp_w2_v,
              w_o, ln_post_mix, ln_pre_ffn, w_up, w_down, ln_post_ffn):
    B, T, _ = x_prompt.shape
    DB, Tn, _ = x_sample.shape
    past_len = page_table.shape[1] * PAGE_SIZE
    pos_p = jnp.arange(T, dtype=jnp.int32)
    pos_s = past_len + jnp.arange(Tn, dtype=jnp.int32)
    win_p = min(WINDOW, T)
    xp, xs = x_prompt, x_sample
    rs_p, ck_p, cv_p, sk_p, sv_p, wk_p, wv_p = [], [], [], [], [], [], []
    rs_s, ck_s, cv_s, sk_s, sv_s, wk_s, wv_s = [], [], [], [], [], [], []
    for l in range(DEPTH):
        tail = (w_o[l], ln_post_mix[l], ln_pre_ffn[l], w_up[l], w_down[l], ln_post_ffn[l])
        rq, rk, rv, rg, nq, gates, ck, cv, sk, sv, wk, wv = mix_inputs(xp, pos_p, ln_pre_mix[l], w_in[l])
        s0 = jnp.zeros((B, RET_HEADS, HEAD_DIM, HEAD_DIM), xp.dtype)
        ro, rs = retention_chunkwise(rq, rk, rv, s0, min(RET_CHUNK, T))
        kc = compress(ck, cmp_pos_k[l], cmp_w1_k[l], cmp_w2_k[l])
        vc = compress(cv, cmp_pos_v[l], cmp_w1_v[l], cmp_w2_v[l])
        o_cmp, p = cmp_attend(nq, kc, vc, pos_p)
        idx, ok = select_blocks(p, pos_p)
        o_sel = sel_prompt(nq, sk, sv, idx, ok)
        o_win = window_prompt(nq, wk, wv)
        xp = finish(xp, retention_out(ro, rg, ret_gn_w[l]), nsa_combine(gates, o_cmp, o_sel, o_win), *tail)
        rs_p.append(rs); ck_p.append(ck); cv_p.append(cv); sk_p.append(sk); sv_p.append(sv)
        wk_p.append(wk[:, T - win_p:]); wv_p.append(wv[:, T - win_p:])
        rq, rk, rv, rg, nq, gates, ck, cv, sk, sv, wk, wv = mix_inputs(xs, pos_s, ln_pre_mix[l], w_in[l])
        ro, rs = retention_chunkwise(rq, rk, rv, state_ret[l], Tn)
        kc = compress(jnp.concatenate([gather_pages(cache_cmp_k[l], page_table), ck], 1),
                      cmp_pos_k[l], cmp_w1_k[l], cmp_w2_k[l])
        vc = compress(jnp.concatenate([gather_pages(cache_cmp_v[l], page_table), cv], 1),
                      cmp_pos_v[l], cmp_w1_v[l], cmp_w2_v[l])
        o_cmp, p = cmp_attend(nq, kc, vc, pos_s)
        idx, ok = select_blocks(p, pos_s)
        o_sel = sel_sample(nq, cache_sel_k[l], cache_sel_v[l], page_table, sk, sv, idx, ok, pos_s)
        o_win, nwk, nwv = window_sample(nq, cache_win_k[l], cache_win_v[l], wk, wv, pos_s, past_len)
        xs = finish(xs, retention_out(ro, rg, ret_gn_w[l]), nsa_combine(gates, o_cmp, o_sel, o_win), *tail)
        rs_s.append(rs); ck_s.append(ck); cv_s.append(cv); sk_s.append(sk); sv_s.append(sv)
        wk_s.append(nwk); wv_s.append(nwv)
    return (xp, xs,
            jnp.stack(rs_p), jnp.stack(ck_p), jnp.stack(cv_p), jnp.stack(sk_p), jnp.stack(sv_p),
            jnp.stack(wk_p), jnp.stack(wv_p),
            jnp.stack(rs_s), jnp.stack(ck_s), jnp.stack(cv_s), jnp.stack(sk_s), jnp.stack(sv_s),
            jnp.stack(wk_s), jnp.stack(wv_s))
```

```python
import functools

import numpy as np
import jax
import jax.numpy as jnp
from jax import lax
from jax.experimental import pallas as pl
from jax.experimental.pallas import tpu as pltpu

F32 = jnp.float32
BF16 = jnp.bfloat16
I32 = jnp.int32

D_MODEL = 1024
HEAD_DIM = 64
RET_WIDTH = 512
RET_HEADS = 8
NSA_WIDTH = 512
NSA_HEADS = 8
NSA_KV_HEADS = 2
NSA_GROUP = 4
KV_WIDTH = 128
N_BRANCH = 3
D_FF = 4 * D_MODEL
CMP_BLOCK = 64
N_SELECT = 16
WINDOW = 512
PAGE_SIZE = 128
RET_CHUNK = 128
ROPE_BASE = 10000.0
EPS = 1e-6
NEG = -1e30
SCALE = HEAD_DIM ** -0.5

MAIN_WIDTH = 4 * RET_WIDTH + NSA_WIDTH + 6 * KV_WIDTH
GATE_PAD = 128
ATT_Q = 128
VMEM_LIMIT = 56 * 1024 * 1024

_LOG_GAMMA = [float(np.log1p(-np.exp2(-5.0 - h))) for h in range(RET_HEADS)]


def _params(*sem):
    return pltpu.CompilerParams(dimension_semantics=sem, vmem_limit_bytes=VMEM_LIMIT)


def _dot(a, b):
    return jnp.dot(a, b, preferred_element_type=F32)


def _dot_nt(a, b):
    return lax.dot_general(a, b, (((1,), (1,)), ((), ())), preferred_element_type=F32)


def _dot_tn(a, b):
    return lax.dot_general(a, b, (((0,), (0,)), ((), ())), preferred_element_type=F32)


def _rms(x, w):
    return x * lax.rsqrt(jnp.mean(x * x, axis=-1, keepdims=True) + EPS) * w


def _inproj_kernel(x_ref, lnw_ref, w_ref, wg_ref, cos_ref, sin_ref,
                   rq_ref, rk_ref, rv_ref, rg_ref, nq_ref,
                   ck_ref, cv_ref, sk_ref, sv_ref, wk_ref, wv_ref, gt_ref):
    h = _rms(x_ref[...], lnw_ref[...]).astype(BF16)
    rows = h.shape[0]

    def proj(a, b):
        return _dot(h, w_ref[:, a:b])

    cos = cos_ref[...]
    sin = sin_ref[...]
    lane = lax.broadcasted_iota(I32, (rows, RET_WIDTH), 1)
    first_half = (lane % HEAD_DIM) < (HEAD_DIM // 2)

    def rope(y):
        partner = jnp.where(first_half,
                            pltpu.roll(y, RET_WIDTH - HEAD_DIM // 2, 1),
                            pltpu.roll(y, HEAD_DIM // 2, 1))
        return y * cos + partner * sin

    rq_ref[...] = rope(proj(0, 512)).astype(BF16)
    rk_ref[...] = (rope(proj(512, 1024)) * SCALE).astype(BF16)
    rv_ref[...] = proj(1024, 1536).astype(BF16)
    rg_ref[...] = proj(1536, 2048)
    nq_ref[...] = proj(2048, 2560).astype(BF16)
    for n, ref in enumerate((ck_ref, cv_ref, sk_ref, sv_ref, wk_ref, wv_ref)):
        ref[...] = proj(2560 + n * KV_WIDTH, 2560 + (n + 1) * KV_WIDTH)
    gt_ref[...] = jax.nn.sigmoid(_dot(h, wg_ref[...]))


def _rope_tables(pos):
    half = HEAD_DIM // 2
    inv = ROPE_BASE ** (-jnp.arange(half, dtype=F32) / half)
    ang = pos.astype(F32)[:, None] * inv
    cos, sin = jnp.cos(ang), jnp.sin(ang)
    cos_h = jnp.concatenate([cos, cos], -1)
    sin_h = jnp.concatenate([-sin, sin], -1)
    return jnp.tile(cos_h, (1, RET_HEADS)), jnp.tile(sin_h, (1, RET_HEADS))


def _inproj(x2d, lnw, w_main, w_gate, cos_t, sin_t, tile):
    rows = x2d.shape[0]
    n_tab = cos_t.shape[0] // tile
    row = lambda i: (i, 0)
    const = lambda i: (0, 0)
    tab = lambda i: (i % n_tab, 0)
    widths = [(RET_WIDTH, BF16), (RET_WIDTH, BF16), (RET_WIDTH, BF16), (RET_WIDTH, F32), (NSA_WIDTH, BF16)]
    widths += [(KV_WIDTH, F32)] * 6 + [(GATE_PAD, F32)]
    return pl.pallas_call(
        _inproj_kernel,
        grid=(rows // tile,),
        in_specs=[pl.BlockSpec((tile, D_MODEL), row),
                  pl.BlockSpec((1, D_MODEL), const),
                  pl.BlockSpec((D_MODEL, MAIN_WIDTH), const),
                  pl.BlockSpec((D_MODEL, GATE_PAD), const),
                  pl.BlockSpec((tile, RET_WIDTH), tab),
                  pl.BlockSpec((tile, RET_WIDTH), tab)],
        out_specs=[pl.BlockSpec((tile, w), row) for w, _ in widths],
        out_shape=[jax.ShapeDtypeStruct((rows, w), dt) for w, dt in widths],
        compiler_params=_params("parallel"),
        name="inproj",
    )(x2d, lnw, w_main, w_gate, cos_t, sin_t)


def _group_norm_gate(outs, g, gnw):
    ys = []
    for o in outs:
        mu = jnp.mean(o, axis=-1, keepdims=True)
        d = o - mu
        var = jnp.mean(d * d, axis=-1, keepdims=True)
        ys.append(d * lax.rsqrt(var + EPS))
    y = jnp.concatenate(ys, axis=1) * gnw
    return g * jax.nn.sigmoid(g) * y


def _ret_prompt_kernel(q_ref, k_ref, v_ref, g_ref, gnw_ref, inner_ref, qdec_ref, kdec_ref,
                       y_ref, s_out_ref, s_scr):
    c = pl.program_id(1)

    @pl.when(c == 0)
    def _():
        s_scr[...] = jnp.zeros_like(s_scr)

    q = q_ref[0]
    k = k_ref[0]
    v = v_ref[0]
    kd = (k.astype(F32) * kdec_ref[...]).astype(BF16)
    qdec = qdec_ref[...]
    outs = []
    for h in range(RET_HEADS):
        sl = slice(h * HEAD_DIM, (h + 1) * HEAD_DIM)
        qh, kh, vh = q[:, sl], k[:, sl], v[:, sl]
        a = _dot_nt(qh, kh) * inner_ref[h]
        s = s_scr[h]
        o = _dot(a.astype(BF16), vh) + _dot(qh, s.astype(BF16)) * qdec[:, sl]
        s_scr[h] = s * float(np.exp(_LOG_GAMMA[h] * RET_CHUNK)) + _dot_tn(kd[:, sl], vh)
        outs.append(o)
    y_ref[0] = _group_norm_gate(outs, g_ref[0], gnw_ref[...]).astype(BF16)

    @pl.when(c == pl.num_programs(1) - 1)
    def _():
        s_out_ref[0] = s_scr[...]


def _ret_prompt(rq, rk, rv, rg, gnw):
    B, T, _ = rq.shape
    C = RET_CHUNK
    lg = jnp.asarray(_LOG_GAMMA, F32)
    i = jnp.arange(C, dtype=F32)
    diff = i[:, None] - i[None, :]
    inner = jnp.where(diff >= 0, jnp.exp(lg[:, None, None] * jnp.maximum(diff, 0.0)), 0.0)
    qdec = jnp.repeat(jnp.exp(lg[None, :] * (i[:, None] + 1.0)), HEAD_DIM, axis=1)
    kdec = jnp.repeat(jnp.exp(lg[None, :] * (C - 1.0 - i[:, None])), HEAD_DIM, axis=1)
    blk = pl.BlockSpec((1, C, RET_WIDTH), lambda b, c: (b, c, 0))
    c2 = lambda b, c: (0, 0)
    return pl.pallas_call(
        _ret_prompt_kernel,
        grid=(B, T // C),
        in_specs=[blk, blk, blk, blk,
                  pl.BlockSpec((1, RET_WIDTH), c2),
                  pl.BlockSpec((RET_HEADS, C, C), lambda b, c: (0, 0, 0)),
                  pl.BlockSpec((C, RET_WIDTH), c2),
                  pl.BlockSpec((C, RET_WIDTH), c2)],
        out_specs=[blk, pl.BlockSpec((1, RET_HEADS, HEAD_DIM, HEAD_DIM), lambda b, c: (b, 0, 0, 0))],
        out_shape=[jax.ShapeDtypeStruct((B, T, RET_WIDTH), BF16),
                   jax.ShapeDtypeStruct((B, RET_HEADS, HEAD_DIM, HEAD_DIM), F32)],
        scratch_shapes=[pltpu.VMEM((RET_HEADS, HEAD_DIM, HEAD_DIM), F32)],
        compiler_params=_params("parallel", "arbitrary"),
        name="ret_prompt",
    )(rq, rk, rv, rg, gnw, inner, qdec, kdec)


def _ret_sample_kernel(q_ref, k_ref, v_ref, g_ref, gnw_ref, s_ref, y_ref, s_out_ref):
    q = q_ref[0].astype(F32)
    k = k_ref[0].astype(F32)
    v = v_ref[0].astype(F32)
    eye = (lax.broadcasted_iota(I32, (HEAD_DIM, HEAD_DIM), 0)
           == lax.broadcasted_iota(I32, (HEAD_DIM, HEAD_DIM), 1))

    def column(row):
        return jnp.sum(jnp.where(eye, row, 0.0), axis=1, keepdims=True)

    outs = []
    for h in range(RET_HEADS):
        sl = slice(h * HEAD_DIM, (h + 1) * HEAD_DIM)
        qh, kh, vh = q[:, sl], k[:, sl], v[:, sl]
        gamma = float(np.exp(_LOG_GAMMA[h]))
        s = s_ref[0, h]
        o = (jnp.sum(qh * kh, axis=1, keepdims=True) * vh
             + jnp.sum(column(qh) * s, axis=0, keepdims=True) * gamma)
        s_out_ref[0, h] = s * gamma + column(kh) * vh
        outs.append(o)
    y_ref[0] = _group_norm_gate(outs, g_ref[0], gnw_ref[...]).astype(BF16)


def _ret_sample(rq, rk, rv, rg, gnw, state):
    DB = rq.shape[0]
    row = pl.BlockSpec((1, 1, RET_WIDTH), lambda b: (b, 0, 0))
    st = pl.BlockSpec((1, RET_HEADS, HEAD_DIM, HEAD_DIM), lambda b: (b, 0, 0, 0))
    r3 = lambda t: t.reshape(DB, 1, RET_WIDTH)
    y, s_new = pl.pallas_call(
        _ret_sample_kernel,
        grid=(DB,),
        in_specs=[row, row, row, row, pl.BlockSpec((1, RET_WIDTH), lambda b: (0, 0)), st],
        out_specs=[row, st],
        out_shape=[jax.ShapeDtypeStruct((DB, 1, RET_WIDTH), BF16),
                   jax.ShapeDtypeStruct(state.shape, F32)],
        compiler_params=_params("parallel"),
        name="ret_sample",
    )(r3(rq), r3(rk), r3(rv), r3(rg), gnw, state)
    return y.reshape(DB, RET_WIDTH), s_new


def _compress(load_rows, pos_ref, w1_ref, w2_ref):
    acc = None
    for l in range(CMP_BLOCK):
        x = (load_rows(l) + pos_ref[l:l + 1, :]).astype(BF16)
        part = _dot(x, w1_ref[l])
        acc = part if acc is None else acc + part
    h = jax.nn.gelu(acc)
    return _dot(h.astype(BF16), w2_ref[...])


def _block_diag2(w):
    z = jnp.zeros_like(w)
    return jnp.concatenate([jnp.concatenate([w, z], -1), jnp.concatenate([z, w], -1)], -2)


def _compress_weights(pos, w1, w2):
    pos2 = jnp.concatenate([pos, pos], -1)
    return pos2, _block_diag2(w1).astype(BF16), _block_diag2(w2).astype(BF16)


def _rank_lanes(imp, n_idx):
    rank = jnp.zeros(imp.shape, I32)
    for m in range(imp.shape[1]):
        col = imp[:, m:m + 1]
        ahead = (col > imp) | ((col == imp) & (n_idx > m))
        rank = rank + jnp.where(ahead, 1, 0)
    return rank


def _cmp_prompt_kernel(ck_ref, cv_ref, posk_ref, w1k_ref, w2k_ref, posv_ref, w1v_ref, w2v_ref,
                       nq_ref, gt_ref, ocmp_ref, bias_ref, kc_scr, vc_scr):
    i = pl.program_id(1)
    nb = kc_scr.shape[0]
    tq = nq_ref.shape[1]

    @pl.when(i == 0)
    def _():
        kc_scr[...] = _compress(lambda l: ck_ref[0, pl.ds(l, nb, stride=CMP_BLOCK), :],
                                posk_ref, w1k_ref, w2k_ref)
        vc_scr[...] = _compress(lambda l: cv_ref[0, pl.ds(l, nb, stride=CMP_BLOCK), :],
                                posv_ref, w1v_ref, w2v_ref)

    nq = nq_ref[0]
    gt = gt_ref[0]
    kc = kc_scr[...].astype(BF16)
    vc = vc_scr[...].astype(BF16)
    pos = i * tq + lax.broadcasted_iota(I32, (tq, nb), 0)
    n_idx = lax.broadcasted_iota(I32, (tq, nb), 1)
    valid = (n_idx + 1) * CMP_BLOCK - 1 <= pos
    cur = pos // CMP_BLOCK
    cand = n_idx < cur
    outs, biases = [], []
    for g in range(NSA_KV_HEADS):
        gsl = slice(g * HEAD_DIM, (g + 1) * HEAD_DIM)
        imp = jnp.zeros((tq, nb), F32)
        for r in range(NSA_GROUP):
            hh = g * NSA_GROUP + r
            q = nq[:, hh * HEAD_DIM:(hh + 1) * HEAD_DIM]
            s = jnp.where(valid, _dot_nt(q, kc[:, gsl]) * SCALE, NEG)
            e = jnp.exp(s - jnp.max(s, axis=-1, keepdims=True))
            p = jnp.where(valid, e / jnp.sum(e, axis=-1, keepdims=True), 0.0)
            imp = imp + p
            outs.append(_dot(p.astype(BF16), vc[:, gsl]) * gt[:, N_BRANCH * hh:N_BRANCH * hh + 1])
        imp = jnp.where(cand, imp, -1.0)
        rank = _rank_lanes(imp, n_idx)
        sel = (cand & (rank < N_SELECT - 1)) | (n_idx == cur)
        biases.append(jnp.where(sel, 0.0, NEG))
    ocmp_ref[0] = jnp.concatenate(outs, axis=1)
    bias_ref[0] = jnp.concatenate(biases, axis=1).astype(BF16)


def _cmp_prompt(ck, cv, cwk, cwv, nq, gates, tq):
    B, T, _ = ck.shape
    nb = T // CMP_BLOCK
    full = pl.BlockSpec((1, T, KV_WIDTH), lambda b, i: (b, 0, 0))
    c2 = lambda b, i: (0, 0)
    c3 = lambda b, i: (0, 0, 0)
    wspecs = [pl.BlockSpec((CMP_BLOCK, KV_WIDTH), c2),
              pl.BlockSpec((CMP_BLOCK, KV_WIDTH, KV_WIDTH), c3),
              pl.BlockSpec((KV_WIDTH, KV_WIDTH), c2)]
    blk = lambda w: pl.BlockSpec((1, tq, w), lambda b, i: (b, i, 0))
    return pl.pallas_call(
        _cmp_prompt_kernel,
        grid=(B, T // tq),
        in_specs=[full, full] + wspecs + wspecs + [blk(NSA_WIDTH), blk(GATE_PAD)],
        out_specs=[blk(NSA_WIDTH), blk(NSA_KV_HEADS * nb)],
        out_shape=[jax.ShapeDtypeStruct((B, T, NSA_WIDTH), F32),
                   jax.ShapeDtypeStruct((B, T, NSA_KV_HEADS * nb), BF16)],
        scratch_shapes=[pltpu.VMEM((nb, KV_WIDTH), F32), pltpu.VMEM((nb, KV_WIDTH), F32)],
        compiler_params=_params("parallel", "arbitrary"),
        name="cmp_prompt",
    )(ck, cv, *cwk, *cwv, nq, gates)


def _softmax_update(carry, s, v):
    m, l, acc = carry
    m_new = jnp.maximum(m, jnp.max(s, axis=-1, keepdims=True))
    alpha = jnp.exp(m - m_new)
    p = jnp.exp(s - m_new)
    l = alpha * l + jnp.sum(p, axis=-1, keepdims=True)
    acc = alpha * acc + _dot(p.astype(BF16), v)
    return m_new, l, acc


def _att_prompt_kernel(nq_ref, bias_ref, sk_ref, sv_ref, wk_ref, wv_ref, gt_ref, ocmp_ref, y_ref):
    i = pl.program_id(1)
    tq = nq_ref.shape[1]
    nb = bias_ref.shape[2] // NSA_KV_HEADS
    rows = NSA_GROUP * tq
    nq = nq_ref[0]
    bias = bias_ref[0]
    gt = gt_ref[0]
    ocmp = ocmp_ref[0]
    q_row = lax.broadcasted_iota(I32, (rows, tq), 0) % tq
    k_col = lax.broadcasted_iota(I32, (rows, tq), 1)
    blk_lane = lax.broadcasted_iota(I32, (tq, nb), 1)
    blk_row = lax.broadcasted_iota(I32, (tq, nb), 0) // CMP_BLOCK
    init = (jnp.full((rows, 1), NEG, F32), jnp.zeros((rows, 1), F32), jnp.zeros((rows, HEAD_DIM), F32))
    outs = []
    for g in range(NSA_KV_HEADS):
        gsl = slice(g * HEAD_DIM, (g + 1) * HEAD_DIM)
        heads = [nq[:, (g * NSA_GROUP + r) * HEAD_DIM:(g * NSA_GROUP + r + 1) * HEAD_DIM]
                 for r in range(NSA_GROUP)]
        q_stack = jnp.concatenate(heads, axis=0)
        bias_g = bias[:, g * nb:(g + 1) * nb]
        qa = jnp.concatenate([(q_stack.astype(F32) * SCALE).astype(BF16),
                              jnp.concatenate([bias_g] * NSA_GROUP, axis=0)], axis=1)

        def sel_step(c, carry):
            start = pl.multiple_of(c * tq, tq)
            k = sk_ref[0, pl.ds(start, tq), :][:, gsl].astype(BF16)
            v = sv_ref[0, pl.ds(start, tq), :][:, gsl].astype(BF16)
            onehot = jnp.where(blk_lane == c * (tq // CMP_BLOCK) + blk_row, 1.0, 0.0).astype(BF16)
            s = _dot_nt(qa, jnp.concatenate([k, onehot], axis=1))
            s = jnp.where((c == i) & (k_col > q_row), NEG, s)
            return _softmax_update(carry, s, v)

        _, l, acc = lax.fori_loop(0, i + 1, sel_step, init)
        o_sel = acc / l

        carry = init
        n_win = WINDOW // tq
        for j in range(n_win + 1):
            c = i - n_win + j
            start = pl.multiple_of(jnp.maximum(c, 0) * tq, tq)
            k = wk_ref[0, pl.ds(start, tq), :][:, gsl].astype(BF16)
            v = wv_ref[0, pl.ds(start, tq), :][:, gsl].astype(BF16)
            s = _dot_nt(q_stack, k) * SCALE
            keep = c >= 0
            if j == 0:
                keep = keep & (k_col > q_row)
            if j == n_win:
                keep = keep & (k_col <= q_row)
            s = jnp.where(keep, s, NEG)
            carry = _softmax_update(carry, s, v)
        _, l, acc = carry
        o_win = acc / l

        for r in range(NSA_GROUP):
            hh = g * NSA_GROUP + r
            rs = slice(r * tq, (r + 1) * tq)
            outs.append(ocmp[:, hh * HEAD_DIM:(hh + 1) * HEAD_DIM]
                        + gt[:, N_BRANCH * hh + 1:N_BRANCH * hh + 2] * o_sel[rs]
                        + gt[:, N_BRANCH * hh + 2:N_BRANCH * hh + 3] * o_win[rs])
    y_ref[0] = jnp.concatenate(outs, axis=1).astype(BF16)


def _att_prompt(nq, bias, sk, sv, wk, wv, gates, ocmp):
    B, T, _ = nq.shape
    tq = ATT_Q
    full = pl.BlockSpec((1, T, KV_WIDTH), lambda b, i: (b, 0, 0))
    blk = lambda w: pl.BlockSpec((1, tq, w), lambda b, i: (b, i, 0))
    return pl.pallas_call(
        _att_prompt_kernel,
        grid=(B, T // tq),
        in_specs=[blk(NSA_WIDTH), blk(bias.shape[2]), full, full, full, full, blk(GATE_PAD), blk(NSA_WIDTH)],
        out_specs=blk(NSA_WIDTH),
        out_shape=jax.ShapeDtypeStruct((B, T, NSA_WIDTH), BF16),
        compiler_params=_params("parallel", "arbitrary"),
        name="att_prompt",
    )(nq, bias, sk, sv, wk, wv, gates, ocmp)


def _finish_kernel(x_ref, ry_ref, ny_ref, wo_ref, l1_ref, l2_ref, l3_ref, wu_ref, wd_ref, o_ref):
    mix = _dot(ry_ref[...], wo_ref[0:RET_WIDTH, :]) + _dot(ny_ref[...], wo_ref[RET_WIDTH:, :])
    h = x_ref[...] + _rms(mix, l1_ref[...])
    u = _dot(_rms(h, l2_ref[...]).astype(BF16), wu_ref[...])
    a = jnp.square(jnp.maximum(u, 0.0)).astype(BF16)
    f = _dot(a, wd_ref[...])
    o_ref[...] = h + _rms(f, l3_ref[...])


def _finish(x2d, ret_y, nsa_y, w_o, l1, l2, l3, w_up, w_down, tile):
    rows = x2d.shape[0]
    row = lambda w: pl.BlockSpec((tile, w), lambda i: (i, 0))
    const = lambda a, b: pl.BlockSpec((a, b), lambda i: (0, 0))
    return pl.pallas_call(
        _finish_kernel,
        grid=(rows // tile,),
        in_specs=[row(D_MODEL), row(RET_WIDTH), row(NSA_WIDTH),
                  const(D_MODEL, D_MODEL), const(1, D_MODEL), const(1, D_MODEL), const(1, D_MODEL),
                  const(D_MODEL, D_FF), const(D_FF, D_MODEL)],
        out_specs=row(D_MODEL),
        out_shape=jax.ShapeDtypeStruct((rows, D_MODEL), F32),
        compiler_params=_params("parallel"),
        name="finish",
    )(x2d, ret_y, nsa_y, w_o, l1, l2, l3, w_up, w_down)


def _page_copy(pool_ref, buf_ref, sem_ref, pt_ref, b, p, slot, n_pages):
    dst = buf_ref.at[pl.ds((slot * n_pages + p) * PAGE_SIZE, PAGE_SIZE), :]
    return pltpu.make_async_copy(pool_ref.at[pt_ref[b, p]], dst, sem_ref.at[slot])


def _head_rows(q_row):
    q_row = q_row.astype(F32)
    z = jnp.zeros((1, HEAD_DIM), F32)
    rows = []
    for hh in range(NSA_HEADS):
        piece = q_row[:, hh * HEAD_DIM:(hh + 1) * HEAD_DIM]
        rows.append(jnp.concatenate([piece, z] if hh < NSA_GROUP else [z, piece], axis=1))
    return jnp.concatenate(rows, axis=0)


def _cmp_sample_kernel(pt_ref, nq_ref, poolk_ref, poolv_ref,
                       posk_ref, w1k_ref, w2k_ref, posv_ref, w1v_ref, w2v_ref,
                       ocmp_ref, idx_ref, kbuf, vbuf, ksem, vsem, *, past_len):
    b = pl.program_id(0)
    n_pages = pt_ref.shape[1]
    nb = past_len // CMP_BLOCK
    slot = b % 2

    def fetch(bb, sl, start):
        for p in range(n_pages):
            for pool, buf, sem in ((poolk_ref, kbuf, ksem), (poolv_ref, vbuf, vsem)):
                cp = _page_copy(pool, buf, sem, pt_ref, bb, p, sl, n_pages)
                cp.start() if start else cp.wait()

    @pl.when(b == 0)
    def _():
        fetch(0, 0, True)

    @pl.when(b + 1 < pl.num_programs(0))
    def _():
        fetch(b + 1, 1 - slot, True)

    fetch(b, slot, False)

    base = slot * n_pages * PAGE_SIZE

    def rows_of(buf):
        return lambda l: buf[pl.ds(base + l, nb, stride=CMP_BLOCK), :]

    kc = _compress(rows_of(kbuf), posk_ref, w1k_ref, w2k_ref).astype(BF16)
    vc = _compress(rows_of(vbuf), posv_ref, w1v_ref, w2v_ref).astype(BF16)

    qh = _head_rows(nq_ref[0]).astype(BF16)
    n_lane = lax.broadcasted_iota(I32, (NSA_HEADS, nb), 1)
    valid = (n_lane + 1) * CMP_BLOCK - 1 <= past_len
    s = jnp.where(valid, _dot_nt(qh, kc) * SCALE, NEG)
    e = jnp.exp(s - jnp.max(s, axis=-1, keepdims=True))
    p = jnp.where(valid, e / jnp.sum(e, axis=-1, keepdims=True), 0.0)
    ocmp_ref[0] = _dot(p.astype(BF16), vc)

    row_i = lax.broadcasted_iota(I32, (nb, nb), 0)
    col_i = lax.broadcasted_iota(I32, (nb, nb), 1)
    eye = row_i == col_i
    idx_rows = []
    for g in range(NSA_KV_HEADS):
        imp = jnp.sum(p[g * NSA_GROUP:(g + 1) * NSA_GROUP], axis=0, keepdims=True)
        imp = jnp.where(n_lane[0:1] < past_len // CMP_BLOCK, imp, -1.0)
        imp_col = jnp.sum(jnp.where(eye, imp, 0.0), axis=1, keepdims=True)
        ahead = (imp > imp_col) | ((imp == imp_col) & (col_i < row_i))
        rank_col = jnp.sum(jnp.where(ahead, 1, 0), axis=1, keepdims=True)
        idx_rows.append(jnp.sum(jnp.where(rank_col == col_i, row_i, 0), axis=0, keepdims=True))
    idx_ref[0] = jnp.concatenate(idx_rows, axis=0)


def _cmp_sample(page_table, nq, pool_k, pool_v, cwk, cwv):
    DB, n_pages = page_table.shape
    past_len = n_pages * PAGE_SIZE
    nb = past_len // CMP_BLOCK
    c2 = lambda b, pt: (0, 0)
    c3 = lambda b, pt: (0, 0, 0)
    wspecs = [pl.BlockSpec((CMP_BLOCK, KV_WIDTH), c2),
              pl.BlockSpec((CMP_BLOCK, KV_WIDTH, KV_WIDTH), c3),
              pl.BlockSpec((KV_WIDTH, KV_WIDTH), c2)]
    any_spec = pl.BlockSpec(memory_space=pl.ANY)
    grid_spec = pltpu.PrefetchScalarGridSpec(
        num_scalar_prefetch=1,
        grid=(DB,),
        in_specs=[pl.BlockSpec((1, 1, NSA_WIDTH), lambda b, pt: (b, 0, 0)), any_spec, any_spec]
        + wspecs + wspecs,
        out_specs=[pl.BlockSpec((1, NSA_HEADS, KV_WIDTH), lambda b, pt: (b, 0, 0)),
                   pl.BlockSpec((1, NSA_KV_HEADS, nb), lambda b, pt: (b, 0, 0))],
        scratch_shapes=[pltpu.VMEM((2 * n_pages * PAGE_SIZE, KV_WIDTH), F32),
                        pltpu.VMEM((2 * n_pages * PAGE_SIZE, KV_WIDTH), F32),
                        pltpu.SemaphoreType.DMA((2,)),
                        pltpu.SemaphoreType.DMA((2,))],
    )
    return pl.pallas_call(
        functools.partial(_cmp_sample_kernel, past_len=past_len),
        grid_spec=grid_spec,
        out_shape=[jax.ShapeDtypeStruct((DB, NSA_HEADS, KV_WIDTH), F32),
                   jax.ShapeDtypeStruct((DB, NSA_KV_HEADS, nb), I32)],
        compiler_params=_params("arbitrary"),
        name="cmp_sample",
    )(page_table, nq.reshape(DB, 1, NSA_WIDTH), pool_k, pool_v, *cwk, *cwv)


N_PICK = N_SELECT - 1


def _att_sample_kernel(idx_ref, pt_ref, nq_ref, knew_ref, vnew_ref, wknew_ref, wvnew_ref,
                       gt_ref, ocmp_ref, poolk_ref, poolv_ref, wk_ref, wv_ref,
                       y_ref, wk_out_ref, wv_out_ref, kbuf, vbuf, ksem, vsem, *, past_len):
    b = pl.program_id(0)

    def gather(start):
        for g in range(NSA_KV_HEADS):
            for s in range(N_PICK):
                blk = idx_ref[(b * NSA_KV_HEADS + g) * N_SELECT + s]
                page = pt_ref[b, blk // 2]
                src_rows = pl.ds(pl.multiple_of((blk % 2) * CMP_BLOCK, CMP_BLOCK), CMP_BLOCK)
                dst_rows = pl.ds((g * N_PICK + s) * CMP_BLOCK, CMP_BLOCK)
                for pool, buf, sem in ((poolk_ref, kbuf, ksem), (poolv_ref, vbuf, vsem)):
                    cp = pltpu.make_async_copy(pool.at[page, src_rows, :], buf.at[dst_rows, :], sem.at[0])
                    cp.start() if start else cp.wait()

    gather(True)

    qf = _head_rows(nq_ref[0])
    qh = qf.astype(BF16)
    row_g = lax.broadcasted_iota(I32, (NSA_HEADS, KV_WIDTH), 0) // NSA_GROUP
    lane_g = lax.broadcasted_iota(I32, (NSA_HEADS, KV_WIDTH), 1) // HEAD_DIM

    def bf_round(row):
        return row.astype(BF16).astype(F32)

    def attend(s_past, v_past, k_new, v_new):
        s_self = jnp.sum(qf * bf_round(k_new), axis=1, keepdims=True) * SCALE
        m = jnp.maximum(jnp.max(s_past, axis=-1, keepdims=True), s_self)
        p = jnp.exp(s_past - m)
        p_self = jnp.exp(s_self - m)
        l = jnp.sum(p, axis=-1, keepdims=True) + p_self
        o = _dot(p.astype(BF16), v_past) + bf_round(p_self) * bf_round(v_new)
        return o / l

    wk = wk_ref[0]
    wv = wv_ref[0]
    wb = wk.shape[0]
    j = lax.broadcasted_iota(I32, (NSA_HEADS, wb), 1)
    pos_k = past_len - wb + j
    keep = (past_len - pos_k < WINDOW) & (pos_k >= 0)
    s_win = jnp.where(keep, _dot_nt(qh, wk.astype(BF16)) * SCALE, NEG)
    o_win = attend(s_win, wv.astype(BF16), wknew_ref[0], wvnew_ref[0])

    row_w = lax.broadcasted_iota(I32, (wb, KV_WIDTH), 0)
    wk_out_ref[0] = jnp.where(row_w == wb - 1, wknew_ref[0], pltpu.roll(wk, wb - 1, 0))
    wv_out_ref[0] = jnp.where(row_w == wb - 1, wvnew_ref[0], pltpu.roll(wv, wb - 1, 0))

    gather(False)
    n_keys = N_PICK * CMP_BLOCK
    o_sel = None
    for g in range(NSA_KV_HEADS):
        k = kbuf[g * n_keys:(g + 1) * n_keys, :].astype(BF16)
        v = vbuf[g * n_keys:(g + 1) * n_keys, :].astype(BF16)
        o_g = attend(_dot_nt(qh, k) * SCALE, v, knew_ref[0], vnew_ref[0])
        o_sel = o_g if o_sel is None else jnp.where(row_g == g, o_g, o_sel)

    gt = gt_ref[0]
    lane = lax.broadcasted_iota(I32, (NSA_HEADS, GATE_PAD), 1)
    head = lax.broadcasted_iota(I32, (NSA_HEADS, GATE_PAD), 0)

    def gate(branch):
        return jnp.sum(jnp.where(lane == N_BRANCH * head + branch, gt, 0.0), axis=1, keepdims=True)

    y = gate(0) * ocmp_ref[0] + gate(1) * o_sel + gate(2) * o_win
    pieces = []
    for hh in range(NSA_HEADS):
        g = hh // NSA_GROUP
        pieces.append(y[hh:hh + 1, g * HEAD_DIM:(g + 1) * HEAD_DIM])
    y_ref[0] = jnp.concatenate(pieces, axis=1)


def _att_sample(idx, page_table, nq, sk, sv, wk_new, wv_new, gates, ocmp, pool_k, pool_v, buf_k, buf_v):
    DB, n_pages = page_table.shape
    past_len = n_pages * PAGE_SIZE
    wb = buf_k.shape[1]
    r3 = lambda t: t.reshape(DB, 1, t.shape[-1])
    row = lambda w: pl.BlockSpec((1, 1, w), lambda b, ix, pt: (b, 0, 0))
    any_spec = pl.BlockSpec(memory_space=pl.ANY)
    wspec = pl.BlockSpec((1, wb, KV_WIDTH), lambda b, ix, pt: (b, 0, 0))
    hspec = pl.BlockSpec((1, NSA_HEADS, KV_WIDTH), lambda b, ix, pt: (b, 0, 0))
    grid_spec = pltpu.PrefetchScalarGridSpec(
        num_scalar_prefetch=2,
        grid=(DB,),
        in_specs=[row(NSA_WIDTH), row(KV_WIDTH), row(KV_WIDTH), row(KV_WIDTH), row(KV_WIDTH),
                  row(GATE_PAD), hspec, any_spec, any_spec, wspec, wspec],
        out_specs=[row(NSA_WIDTH), wspec, wspec],
        scratch_shapes=[pltpu.VMEM((NSA_KV_HEADS * N_PICK * CMP_BLOCK, KV_WIDTH), F32),
                        pltpu.VMEM((NSA_KV_HEADS * N_PICK * CMP_BLOCK, KV_WIDTH), F32),
                        pltpu.SemaphoreType.DMA((1,)),
                        pltpu.SemaphoreType.DMA((1,))],
    )
    return pl.pallas_call(
        functools.partial(_att_sample_kernel, past_len=past_len),
        grid_spec=grid_spec,
        out_shape=[jax.ShapeDtypeStruct((DB, 1, NSA_WIDTH), F32),
                   jax.ShapeDtypeStruct(buf_k.shape, F32),
                   jax.ShapeDtypeStruct(buf_v.shape, F32)],
        compiler_params=_params("arbitrary"),
        name="att_sample",
    )(idx, page_table, r3(nq), r3(sk), r3(sv), r3(wk_new), r3(wv_new), r3(gates), ocmp,
      pool_k, pool_v, buf_k, buf_v)


def _row_tile(rows):
    return 256 if rows % 256 == 0 else 128


def kernel(x_prompt, x_sample, state_ret, cache_cmp_k, cache_cmp_v, cache_sel_k, cache_sel_v,
           cache_win_k, cache_win_v, page_table, ln_pre_mix, w_in, ret_gn_w,
           cmp_pos_k, cmp_w1_k, cmp_w2_k, cmp_pos_v, cmp_w1_v, cmp_w2_v,
           w_o, ln_post_mix, ln_pre_ffn, w_up, w_down, ln_post_ffn):
    B, T, _ = x_prompt.shape
    DB, Tn, _ = x_sample.shape
    n_pages = page_table.shape[1]
    past_len = n_pages * PAGE_SIZE
    assert w_in.shape[0] == 1 and Tn == 1
    assert T % ATT_Q == 0 and T >= WINDOW and past_len // CMP_BLOCK >= N_PICK
    wb = cache_win_k.shape[2]

    w_main = w_in[0, :, :MAIN_WIDTH].astype(BF16)
    w_gate = jnp.pad(w_in[0, :, MAIN_WIDTH:], ((0, 0), (0, GATE_PAD - NSA_HEADS * N_BRANCH))).astype(BF16)
    cwk = _compress_weights(cmp_pos_k[0], cmp_w1_k[0], cmp_w2_k[0])
    cwv = _compress_weights(cmp_pos_v[0], cmp_w1_v[0], cmp_w2_v[0])
    tail = (w_o[0].astype(BF16), ln_post_mix, ln_pre_ffn, ln_post_ffn, w_up[0].astype(BF16), w_down[0].astype(BF16))
    kv4 = lambda t, n: t.reshape(1, n, -1, NSA_KV_HEADS, HEAD_DIM)

    tile_p = _row_tile(B * T)
    cos_p, sin_p = _rope_tables(jnp.arange(T, dtype=I32))
    xp = x_prompt.reshape(B * T, D_MODEL)
    rq, rk, rv, rg, nq, ck, cv, sk, sv, wk, wv, gt = _inproj(xp, ln_pre_mix, w_main, w_gate, cos_p, sin_p, tile_p)
    b3 = lambda t: t.reshape(B, T, t.shape[-1])
    ret_y, rs_p = _ret_prompt(b3(rq), b3(rk), b3(rv), b3(rg), ret_gn_w)
    ocmp, bias = _cmp_prompt(b3(ck), b3(cv), cwk, cwv, b3(nq), b3(gt), _row_tile(T))
    nsa_y = _att_prompt(b3(nq), bias, b3(sk), b3(sv), b3(wk), b3(wv), b3(gt), ocmp)
    y_p = _finish(xp, ret_y.reshape(B * T, RET_WIDTH), nsa_y.reshape(B * T, NSA_WIDTH), *tail, tile_p)
    win_p = min(WINDOW, T)
    outs_p = (y_p.reshape(B, T, D_MODEL), rs_p[None],
              kv4(ck, B), kv4(cv, B), kv4(sk, B), kv4(sv, B),
              kv4(b3(wk)[:, T - win_p:], B), kv4(b3(wv)[:, T - win_p:], B))

    tile_s = DB
    pos_s = jnp.full((tile_s,), past_len, I32)
    cos_s, sin_s = _rope_tables(pos_s)
    xs = x_sample.reshape(DB, D_MODEL)
    rq, rk, rv, rg, nq, ck, cv, sk, sv, wk, wv, gt = _inproj(xs, ln_pre_mix, w_main, w_gate, cos_s, sin_s, tile_s)
    ret_y, rs_s = _ret_sample(rq, rk, rv, rg, ret_gn_w, state_ret[0])
    pool = lambda t: t.reshape(t.shape[1], PAGE_SIZE, KV_WIDTH)
    ocmp, idx = _cmp_sample(page_table, nq, pool(cache_cmp_k), pool(cache_cmp_v), cwk, cwv)
    idx_flat = idx[:, :, :N_SELECT].reshape(-1)
    y8, nwk, nwv = _att_sample(idx_flat, page_table, nq, sk, sv, wk, wv, gt, ocmp,
                               pool(cache_sel_k), pool(cache_sel_v),
                               cache_win_k.reshape(DB, wb, KV_WIDTH), cache_win_v.reshape(DB, wb, KV_WIDTH))
    nsa_y = y8.reshape(DB, NSA_WIDTH).astype(BF16)
    y_s = _finish(xs, ret_y, nsa_y, *tail, tile_s)
    outs_s = (y_s.reshape(DB, 1, D_MODEL), rs_s[None],
              kv4(ck, DB), kv4(cv, DB), kv4(sk, DB), kv4(sv, DB),
              nwk.reshape(1, DB, wb, NSA_KV_HEADS, HEAD_DIM), nwv.reshape(1, DB, wb, NSA_KV_HEADS, HEAD_DIM))

    return (outs_p[0], outs_s[0]) + outs_p[1:] + outs_s[1:]
```

```python
import functools

import numpy as np
import jax
import jax.numpy as jnp
from jax import lax
from jax.experimental import pallas as pl
from jax.experimental.pallas import tpu as pltpu

F32 = jnp.float32
BF16 = jnp.bfloat16
I32 = jnp.int32

D_MODEL = 1024
HEAD_DIM = 64
RET_WIDTH = 512
RET_HEADS = 8
NSA_WIDTH = 512
NSA_HEADS = 8
NSA_KV_HEADS = 2
NSA_GROUP = 4
KV_WIDTH = 128
N_BRANCH = 3
D_FF = 4 * D_MODEL
CMP_BLOCK = 64
N_SELECT = 16
WINDOW = 512
PAGE_SIZE = 128
RET_CHUNK = 128
ROPE_BASE = 10000.0
EPS = 1e-6
NEG = -1e30
SCALE = HEAD_DIM ** -0.5

MAIN_WIDTH = 4 * RET_WIDTH + NSA_WIDTH + 6 * KV_WIDTH
GATE_PAD = 128
ATT_Q = 128
VMEM_LIMIT = 56 * 1024 * 1024

_LOG_GAMMA = [float(np.log1p(-np.exp2(-5.0 - h))) for h in range(RET_HEADS)]


def _params(*sem):
    return pltpu.CompilerParams(dimension_semantics=sem, vmem_limit_bytes=VMEM_LIMIT)


def _dot(a, b):
    return jnp.dot(a, b, preferred_element_type=F32)


def _dot_nt(a, b):
    return lax.dot_general(a, b, (((1,), (1,)), ((), ())), preferred_element_type=F32)


def _dot_tn(a, b):
    return lax.dot_general(a, b, (((0,), (0,)), ((), ())), preferred_element_type=F32)


def _rms(x, w):
    return x * lax.rsqrt(jnp.mean(x * x, axis=-1, keepdims=True) + EPS) * w


def _inproj_kernel(x_ref, lnw_ref, w_ref, wg_ref, cos_ref, sin_ref,
                   rq_ref, rk_ref, rv_ref, rg_ref, nq_ref,
                   ck_ref, cv_ref, sk_ref, sv_ref, wk_ref, wv_ref, gt_ref):
    h = _rms(x_ref[...], lnw_ref[...]).astype(BF16)
    rows = h.shape[0]

    def proj(a, b):
        return _dot(h, w_ref[:, a:b])

    cos = cos_ref[...]
    sin = sin_ref[...]
    lane = lax.broadcasted_iota(I32, (rows, RET_WIDTH), 1)
    first_half = (lane % HEAD_DIM) < (HEAD_DIM // 2)

    def rope(y):
        partner = jnp.where(first_half,
                            pltpu.roll(y, RET_WIDTH - HEAD_DIM // 2, 1),
                            pltpu.roll(y, HEAD_DIM // 2, 1))
        return y * cos + partner * sin

    rq_ref[...] = rope(proj(0, 512)).astype(BF16)
    rk_ref[...] = (rope(proj(512, 1024)) * SCALE).astype(BF16)
    rv_ref[...] = proj(1024, 1536).astype(BF16)
    rg_ref[...] = proj(1536, 2048)
    nq_ref[...] = proj(2048, 2560).astype(BF16)
    for n, ref in enumerate((ck_ref, cv_ref, sk_ref, sv_ref, wk_ref, wv_ref)):
        ref[...] = proj(2560 + n * KV_WIDTH, 2560 + (n + 1) * KV_WIDTH)
    gt_ref[...] = jax.nn.sigmoid(_dot(h, wg_ref[...]))


def _rope_tables(pos):
    half = HEAD_DIM // 2
    inv = ROPE_BASE ** (-jnp.arange(half, dtype=F32) / half)
    ang = pos.astype(F32)[:, None] * inv
    cos, sin = jnp.cos(ang), jnp.sin(ang)
    cos_h = jnp.concatenate([cos, cos], -1)
    sin_h = jnp.concatenate([-sin, sin], -1)
    return jnp.tile(cos_h, (1, RET_HEADS)), jnp.tile(sin_h, (1, RET_HEADS))


def _inproj(x2d, lnw, w_main, w_gate, cos_t, sin_t, tile):
    rows = x2d.shape[0]
    n_tab = cos_t.shape[0] // tile
    row = lambda i: (i, 0)
    const = lambda i: (0, 0)
    tab = lambda i: (i % n_tab, 0)
    widths = [(RET_WIDTH, BF16), (RET_WIDTH, BF16), (RET_WIDTH, BF16), (RET_WIDTH, F32), (NSA_WIDTH, BF16)]
    widths += [(KV_WIDTH, F32)] * 6 + [(GATE_PAD, F32)]
    return pl.pallas_call(
        _inproj_kernel,
        grid=(rows // tile,),
        in_specs=[pl.BlockSpec((tile, D_MODEL), row),
                  pl.BlockSpec((1, D_MODEL), const),
                  pl.BlockSpec((D_MODEL, MAIN_WIDTH), const),
                  pl.BlockSpec((D_MODEL, GATE_PAD), const),
                  pl.BlockSpec((tile, RET_WIDTH), tab),
                  pl.BlockSpec((tile, RET_WIDTH), tab)],
        out_specs=[pl.BlockSpec((tile, w), row) for w, _ in widths],
        out_shape=[jax.ShapeDtypeStruct((rows, w), dt) for w, dt in widths],
        compiler_params=_params("parallel"),
        name="inproj",
    )(x2d, lnw, w_main, w_gate, cos_t, sin_t)


def _inproj_prompt_kernel(x_ref, lnw_ref, w_ref, wg_ref, cos_ref, sin_ref,
                          rq_ref, rk_ref, rv_ref, rg_ref, nqT_ref, ck_ref, cv_ref, sk_ref, wk_ref,
                          ckT_ref, cvT_ref, skT_ref, svT_ref, wkT_ref, wvT_ref, svTb_ref, wvTb_ref, gtT_ref):
    h = _rms(x_ref[...], lnw_ref[...]).astype(BF16)
    rows = h.shape[0]

    def proj(a, b):
        return _dot(h, w_ref[:, a:b])

    cos = cos_ref[...]
    sin = sin_ref[...]
    lane = lax.broadcasted_iota(I32, (rows, RET_WIDTH), 1)
    first_half = (lane % HEAD_DIM) < (HEAD_DIM // 2)

    def rope(y):
        partner = jnp.where(first_half,
                            pltpu.roll(y, RET_WIDTH - HEAD_DIM // 2, 1),
                            pltpu.roll(y, HEAD_DIM // 2, 1))
        return y * cos + partner * sin

    rq_ref[...] = rope(proj(0, 512)).astype(BF16)
    rk_ref[...] = (rope(proj(512, 1024)) * SCALE).astype(BF16)
    rv_ref[...] = proj(1024, 1536).astype(BF16)
    rg_ref[...] = proj(1536, 2048)
    for n in range(NSA_WIDTH // KV_WIDTH):
        nq = proj(2048 + n * KV_WIDTH, 2048 + (n + 1) * KV_WIDTH) * SCALE
        nqT_ref[0, n * KV_WIDTH:(n + 1) * KV_WIDTH, :] = nq.T.astype(BF16)
    kv = [proj(2560 + n * KV_WIDTH, 2560 + (n + 1) * KV_WIDTH) for n in range(6)]
    ck_ref[...] = kv[0]
    cv_ref[...] = kv[1]
    sk_ref[...] = kv[2].astype(BF16)
    wk_ref[...] = kv[4].astype(BF16)
    for n, ref in enumerate((ckT_ref, cvT_ref, skT_ref, svT_ref, wkT_ref, wvT_ref)):
        ref[0] = kv[n].T
    svTb_ref[0] = kv[3].T.astype(BF16)
    wvTb_ref[0] = kv[5].T.astype(BF16)
    gtT_ref[0] = jax.nn.sigmoid(_dot(h, wg_ref[...])).T


def _inproj_prompt(x2d, lnw, w_main, w_gate, cos_t, sin_t, B, T, tile):
    rows = x2d.shape[0]
    nt = T // tile
    row = lambda i: (i, 0)
    const = lambda i: (0, 0)
    tab = lambda i: (i % nt, 0)
    colT = lambda i: (i // nt, 0, i % nt)
    rm = lambda w, dt: (pl.BlockSpec((tile, w), row), jax.ShapeDtypeStruct((rows, w), dt))
    tr = lambda w, dt: (pl.BlockSpec((1, w, tile), colT), jax.ShapeDtypeStruct((B, w, T), dt))
    outs = [rm(RET_WIDTH, BF16), rm(RET_WIDTH, BF16), rm(RET_WIDTH, BF16), rm(RET_WIDTH, F32),
            tr(NSA_WIDTH, BF16), rm(KV_WIDTH, F32), rm(KV_WIDTH, F32), rm(KV_WIDTH, BF16), rm(KV_WIDTH, BF16),
            *[tr(KV_WIDTH, F32)] * 6,
            tr(KV_WIDTH, BF16), tr(KV_WIDTH, BF16), tr(GATE_PAD, F32)]
    return pl.pallas_call(
        _inproj_prompt_kernel,
        grid=(rows // tile,),
        in_specs=[pl.BlockSpec((tile, D_MODEL), row),
                  pl.BlockSpec((1, D_MODEL), const),
                  pl.BlockSpec((D_MODEL, MAIN_WIDTH), const),
                  pl.BlockSpec((D_MODEL, GATE_PAD), const),
                  pl.BlockSpec((tile, RET_WIDTH), tab),
                  pl.BlockSpec((tile, RET_WIDTH), tab)],
        out_specs=[o[0] for o in outs],
        out_shape=[o[1] for o in outs],
        compiler_params=_params("parallel"),
        name="inproj_prompt",
    )(x2d, lnw, w_main, w_gate, cos_t, sin_t)


def _group_norm_gate(outs, g, gnw):
    ys = []
    for o in outs:
        mu = jnp.mean(o, axis=-1, keepdims=True)
        d = o - mu
        var = jnp.mean(d * d, axis=-1, keepdims=True)
        ys.append(d * lax.rsqrt(var + EPS))
    y = jnp.concatenate(ys, axis=1) * gnw
    return g * jax.nn.sigmoid(g) * y


def _ret_prompt_kernel(q_ref, k_ref, v_ref, g_ref, gnw_ref, inner_ref, qdec_ref, kdec_ref,
                       y_ref, s_out_ref, s_scr):
    c = pl.program_id(1)

    @pl.when(c == 0)
    def _():
        s_scr[...] = jnp.zeros_like(s_scr)

    q = q_ref[0]
    k = k_ref[0]
    v = v_ref[0]
    kd = (k.astype(F32) * kdec_ref[...]).astype(BF16)
    qdec = qdec_ref[...]
    outs = []
    for h in range(RET_HEADS):
        sl = slice(h * HEAD_DIM, (h + 1) * HEAD_DIM)
        qh, kh, vh = q[:, sl], k[:, sl], v[:, sl]
        a = _dot_nt(qh, kh) * inner_ref[h]
        s = s_scr[h]
        o = _dot(a.astype(BF16), vh) + _dot(qh, s.astype(BF16)) * qdec[:, sl]
        s_scr[h] = s * float(np.exp(_LOG_GAMMA[h] * RET_CHUNK)) + _dot_tn(kd[:, sl], vh)
        outs.append(o)
    y_ref[0] = _group_norm_gate(outs, g_ref[0], gnw_ref[...]).astype(BF16)

    @pl.when(c == pl.num_programs(1) - 1)
    def _():
        s_out_ref[0] = s_scr[...]


def _ret_prompt(rq, rk, rv, rg, gnw):
    B, T, _ = rq.shape
    C = RET_CHUNK
    lg = jnp.asarray(_LOG_GAMMA, F32)
    i = jnp.arange(C, dtype=F32)
    diff = i[:, None] - i[None, :]
    inner = jnp.where(diff >= 0, jnp.exp(lg[:, None, None] * jnp.maximum(diff, 0.0)), 0.0)
    qdec = jnp.repeat(jnp.exp(lg[None, :] * (i[:, None] + 1.0)), HEAD_DIM, axis=1)
    kdec = jnp.repeat(jnp.exp(lg[None, :] * (C - 1.0 - i[:, None])), HEAD_DIM, axis=1)
    blk = pl.BlockSpec((1, C, RET_WIDTH), lambda b, c: (b, c, 0))
    c2 = lambda b, c: (0, 0)
    return pl.pallas_call(
        _ret_prompt_kernel,
        grid=(B, T // C),
        in_specs=[blk, blk, blk, blk,
                  pl.BlockSpec((1, RET_WIDTH), c2),
                  pl.BlockSpec((RET_HEADS, C, C), lambda b, c: (0, 0, 0)),
                  pl.BlockSpec((C, RET_WIDTH), c2),
                  pl.BlockSpec((C, RET_WIDTH), c2)],
        out_specs=[blk, pl.BlockSpec((1, RET_HEADS, HEAD_DIM, HEAD_DIM), lambda b, c: (b, 0, 0, 0))],
        out_shape=[jax.ShapeDtypeStruct((B, T, RET_WIDTH), BF16),
                   jax.ShapeDtypeStruct((B, RET_HEADS, HEAD_DIM, HEAD_DIM), F32)],
        scratch_shapes=[pltpu.VMEM((RET_HEADS, HEAD_DIM, HEAD_DIM), F32)],
        compiler_params=_params("parallel", "arbitrary"),
        name="ret_prompt",
    )(rq, rk, rv, rg, gnw, inner, qdec, kdec)


def _ret_sample_kernel(q_ref, k_ref, v_ref, g_ref, gnw_ref, s_ref, y_ref, s_out_ref):
    q = q_ref[0].astype(F32)
    k = k_ref[0].astype(F32)
    v = v_ref[0].astype(F32)
    eye = (lax.broadcasted_iota(I32, (HEAD_DIM, HEAD_DIM), 0)
           == lax.broadcasted_iota(I32, (HEAD_DIM, HEAD_DIM), 1))

    def column(row):
        return jnp.sum(jnp.where(eye, row, 0.0), axis=1, keepdims=True)

    outs = []
    for h in range(RET_HEADS):
        sl = slice(h * HEAD_DIM, (h + 1) * HEAD_DIM)
        qh, kh, vh = q[:, sl], k[:, sl], v[:, sl]
        gamma = float(np.exp(_LOG_GAMMA[h]))
        s = s_ref[0, h]
        o = (jnp.sum(qh * kh, axis=1, keepdims=True) * vh
             + jnp.sum(column(qh) * s, axis=0, keepdims=True) * gamma)
        s_out_ref[0, h] = s * gamma + column(kh) * vh
        outs.append(o)
    y_ref[0] = _group_norm_gate(outs, g_ref[0], gnw_ref[...]).astype(BF16)


def _ret_sample(rq, rk, rv, rg, gnw, state):
    DB = rq.shape[0]
    row = pl.BlockSpec((1, 1, RET_WIDTH), lambda b: (b, 0, 0))
    st = pl.BlockSpec((1, RET_HEADS, HEAD_DIM, HEAD_DIM), lambda b: (b, 0, 0, 0))
    r3 = lambda t: t.reshape(DB, 1, RET_WIDTH)
    y, s_new = pl.pallas_call(
        _ret_sample_kernel,
        grid=(DB,),
        in_specs=[row, row, row, row, pl.BlockSpec((1, RET_WIDTH), lambda b: (0, 0)), st],
        out_specs=[row, st],
        out_shape=[jax.ShapeDtypeStruct((DB, 1, RET_WIDTH), BF16),
                   jax.ShapeDtypeStruct(state.shape, F32)],
        compiler_params=_params("parallel"),
        name="ret_sample",
    )(r3(rq), r3(rk), r3(rv), r3(rg), gnw, state)
    return y.reshape(DB, RET_WIDTH), s_new


def _compress(load_rows, pos_ref, w1_ref, w2_ref):
    acc = None
    for l in range(CMP_BLOCK):
        x = (load_rows(l) + pos_ref[l:l + 1, :]).astype(BF16)
        part = _dot(x, w1_ref[l])
        acc = part if acc is None else acc + part
    h = jax.nn.gelu(acc)
    return _dot(h.astype(BF16), w2_ref[...])


def _block_diag2(w):
    z = jnp.zeros_like(w)
    return jnp.concatenate([jnp.concatenate([w, z], -1), jnp.concatenate([z, w], -1)], -2)


def _compress_weights(pos, w1, w2):
    pos2 = jnp.concatenate([pos, pos], -1)
    return pos2, _block_diag2(w1).astype(BF16), _block_diag2(w2).astype(BF16)


def _rank_rows(imp, n_idx):
    del n_idx
    nb, tq = imp.shape
    sub = 8
    groups = [imp[sub * j:sub * (j + 1)] for j in range(nb // sub)]
    ranks = [jnp.zeros((sub, tq), I32) for _ in groups]
    row_in_group = lax.broadcasted_iota(I32, (sub, tq), 0)
    for m in range(nb):
        row = imp[m:m + 1, :]
        jm, sm = divmod(m, sub)
        for j, grp in enumerate(groups):
            if j < jm:
                inc = jnp.where(row > grp, 1, 0)
            elif j > jm:
                inc = jnp.where(row >= grp, 1, 0)
            else:
                inc = jnp.where(row_in_group > sm, jnp.where(row >= grp, 1, 0), jnp.where(row > grp, 1, 0))
            ranks[j] = ranks[j] + inc
    return jnp.concatenate(ranks, axis=0)


def _kv_head_rows(x, g):
    z = jnp.zeros_like(x)
    return jnp.concatenate([x, z] if g == 0 else [z, x], axis=0)


MAX_BLOCKS = 64


def _cmp_prompt_kernel(ck_ref, cv_ref, posk_ref, w1k_ref, w2k_ref, posv_ref, w1v_ref, w2v_ref,
                       nqT_ref, gtT_ref, ocmpT_ref, biasT_ref, kc_scr, vc_scr):
    i = pl.program_id(1)
    nb = kc_scr.shape[0]
    tq = nqT_ref.shape[2]

    @pl.when(i == 0)
    def _():
        kc_scr[...] = _compress(lambda l: ck_ref[0, pl.ds(l, nb, stride=CMP_BLOCK), :],
                                posk_ref, w1k_ref, w2k_ref).astype(BF16)
        vc_scr[...] = _compress(lambda l: cv_ref[0, pl.ds(l, nb, stride=CMP_BLOCK), :],
                                posv_ref, w1v_ref, w2v_ref).astype(BF16)

    kc = kc_scr[...]
    vc = vc_scr[...]
    pos = i * tq + lax.broadcasted_iota(I32, (nb, tq), 1)
    n_idx = lax.broadcasted_iota(I32, (nb, tq), 0)
    valid = (n_idx + 1) * CMP_BLOCK - 1 <= pos
    cur = pos // CMP_BLOCK
    cand = n_idx < cur
    for g in range(NSA_KV_HEADS):
        imp = jnp.zeros((nb, tq), F32)
        for r in range(NSA_GROUP):
            hh = g * NSA_GROUP + r
            qT = _kv_head_rows(nqT_ref[0, hh * HEAD_DIM:(hh + 1) * HEAD_DIM, :], g)
            s = jnp.where(valid, _dot(kc, qT), NEG)
            e = jnp.exp(s - jnp.max(s, axis=0, keepdims=True))
            p = jnp.where(valid, e / jnp.sum(e, axis=0, keepdims=True), 0.0)
            imp = imp + p
            oT = _dot_tn(vc, p.astype(BF16))
            ocmpT_ref[0, hh * HEAD_DIM:(hh + 1) * HEAD_DIM, :] = (
                oT[g * HEAD_DIM:(g + 1) * HEAD_DIM] * gtT_ref[0, N_BRANCH * hh:N_BRANCH * hh + 1, :])
        imp = jnp.where(cand, imp, -1.0)
        rank = _rank_rows(imp, n_idx)
        sel = (cand & (rank < N_SELECT - 1)) | (n_idx == cur)
        biasT_ref[0, g * MAX_BLOCKS:g * MAX_BLOCKS + nb, :] = jnp.where(sel, 0.0, NEG).astype(BF16)
        if nb < MAX_BLOCKS:
            biasT_ref[0, g * MAX_BLOCKS + nb:(g + 1) * MAX_BLOCKS, :] = jnp.zeros((MAX_BLOCKS - nb, tq), BF16)


def _cmp_prompt(ck, cv, cwk, cwv, nqT, gtT, tq):
    B, T, _ = ck.shape
    nb = T // CMP_BLOCK
    full = pl.BlockSpec((1, T, KV_WIDTH), lambda b, i: (b, 0, 0))
    c2 = lambda b, i: (0, 0)
    c3 = lambda b, i: (0, 0, 0)
    wspecs = [pl.BlockSpec((CMP_BLOCK, KV_WIDTH), c2),
              pl.BlockSpec((CMP_BLOCK, KV_WIDTH, KV_WIDTH), c3),
              pl.BlockSpec((KV_WIDTH, KV_WIDTH), c2)]
    blk = lambda w: pl.BlockSpec((1, w, tq), lambda b, i: (b, 0, i))
    return pl.pallas_call(
        _cmp_prompt_kernel,
        grid=(B, T // tq),
        in_specs=[full, full] + wspecs + wspecs + [blk(NSA_WIDTH), blk(GATE_PAD)],
        out_specs=[blk(NSA_WIDTH), blk(NSA_KV_HEADS * MAX_BLOCKS)],
        out_shape=[jax.ShapeDtypeStruct((B, NSA_WIDTH, T), F32),
                   jax.ShapeDtypeStruct((B, NSA_KV_HEADS * MAX_BLOCKS, T), BF16)],
        scratch_shapes=[pltpu.VMEM((nb, KV_WIDTH), BF16), pltpu.VMEM((nb, KV_WIDTH), BF16)],
        compiler_params=_params("parallel", "arbitrary"),
        name="cmp_prompt",
    )(ck, cv, *cwk, *cwv, nqT, gtT)


ATT_K = 256


def _att_prompt_kernel(nqT_ref, biasT_ref, sk_ref, svT_ref, wk_ref, wvT_ref, gtT_ref, ocmpT_ref,
                       y_ref, ka_scr):
    i = pl.program_id(1)
    tq = nqT_ref.shape[2]
    T = sk_ref.shape[1]
    cols = NSA_GROUP * tq

    @pl.when(i == 0)
    def _():
        sk = sk_ref[0].astype(F32)
        lane = lax.broadcasted_iota(I32, (T, KV_WIDTH), 1)
        blk = lax.broadcasted_iota(I32, (T, KV_WIDTH), 0) // CMP_BLOCK
        ka_scr[0] = jnp.where(lane < HEAD_DIM, sk, jnp.where(lane - HEAD_DIM == blk, 1.0, 0.0)).astype(BF16)
        ka_scr[1] = jnp.where(lane >= HEAD_DIM, sk, jnp.where(lane == blk, 1.0, 0.0)).astype(BF16)

    q_pos = i * tq + lax.broadcasted_iota(I32, (ATT_K, cols), 1) % tq
    k_off = lax.broadcasted_iota(I32, (ATT_K, cols), 0)
    n_win = WINDOW // tq
    win_keys = (n_win + 1) * tq
    win_start = pl.multiple_of(jnp.maximum(i - n_win, 0) * tq, tq)
    rel = i * tq - win_start + lax.broadcasted_iota(I32, (win_keys, cols), 1) % tq
    j_win = lax.broadcasted_iota(I32, (win_keys, cols), 0)
    win_ok = (j_win <= rel) & (j_win > rel - WINDOW)
    wk = wk_ref[0, pl.ds(win_start, win_keys), :]
    wvT = wvT_ref[0, :, pl.ds(win_start, win_keys)]

    qaTs, qwTs = [], []
    for g in range(NSA_KV_HEADS):
        qTs = [nqT_ref[0, (g * NSA_GROUP + r) * HEAD_DIM:(g * NSA_GROUP + r + 1) * HEAD_DIM, :]
               for r in range(NSA_GROUP)]
        bT = biasT_ref[0, g * MAX_BLOCKS:(g + 1) * MAX_BLOCKS, :]
        qaTs.append(jnp.concatenate([jnp.concatenate([q, bT] if g == 0 else [bT, q], axis=0) for q in qTs],
                                    axis=1))
        qwTs.append(jnp.concatenate([_kv_head_rows(q, g) for q in qTs], axis=1))

    def sel_step(c, carries, last):
        start = pl.multiple_of(c * ATT_K, ATT_K)
        vT = svT_ref[0, :, pl.ds(start, ATT_K)]
        out = []
        for g, (m, l, acc) in enumerate(carries):
            s = _dot(ka_scr[g, pl.ds(start, ATT_K), :], qaTs[g])
            if last:
                s = jnp.where(start + k_off > q_pos, NEG, s)
            m_new = jnp.maximum(m, jnp.max(s, axis=0, keepdims=True))
            alpha = jnp.exp(m - m_new)
            p = jnp.exp(s - m_new)
            l = alpha * l + jnp.sum(p, axis=0, keepdims=True)
            acc = alpha * acc + _dot(vT, p.astype(BF16))
            out.append((m_new, l, acc))
        return tuple(out)

    init = (jnp.full((1, cols), NEG, F32), jnp.zeros((1, cols), F32), jnp.zeros((KV_WIDTH, cols), F32))
    n_before = (i * tq) // ATT_K
    carries = lax.fori_loop(0, n_before, lambda c, cr: sel_step(c, cr, False), (init, init))
    carries = sel_step(n_before, carries, True)

    heads_out = []
    for g in range(NSA_KV_HEADS):
        gs = slice(g * HEAD_DIM, (g + 1) * HEAD_DIM)
        _, l, acc = carries[g]
        o_sel = acc[gs] / l

        s = jnp.where(win_ok, _dot(wk, qwTs[g]), NEG)
        p = jnp.exp(s - jnp.max(s, axis=0, keepdims=True))
        o_win = _dot(wvT, p.astype(BF16))[gs] / jnp.sum(p, axis=0, keepdims=True)

        for r in range(NSA_GROUP):
            hh = g * NSA_GROUP + r
            cs = slice(r * tq, (r + 1) * tq)
            heads_out.append(ocmpT_ref[0, hh * HEAD_DIM:(hh + 1) * HEAD_DIM, :]
                             + gtT_ref[0, N_BRANCH * hh + 1:N_BRANCH * hh + 2, :] * o_sel[:, cs]
                             + gtT_ref[0, N_BRANCH * hh + 2:N_BRANCH * hh + 3, :] * o_win[:, cs])
    y_ref[0] = jnp.concatenate(heads_out, axis=0).T.astype(BF16)


def _att_prompt(nqT, biasT, sk, svT, wk, wvT, gtT, ocmpT):
    B, _, T = nqT.shape
    tq = ATT_Q
    rows_full = pl.BlockSpec((1, T, KV_WIDTH), lambda b, i: (b, 0, 0))
    cols_full = pl.BlockSpec((1, KV_WIDTH, T), lambda b, i: (b, 0, 0))
    blk = lambda w: pl.BlockSpec((1, w, tq), lambda b, i: (b, 0, i))
    return pl.pallas_call(
        _att_prompt_kernel,
        grid=(B, T // tq),
        in_specs=[blk(NSA_WIDTH), blk(NSA_KV_HEADS * MAX_BLOCKS), rows_full, cols_full, rows_full, cols_full,
                  blk(GATE_PAD), blk(NSA_WIDTH)],
        out_specs=pl.BlockSpec((1, tq, NSA_WIDTH), lambda b, i: (b, i, 0)),
        out_shape=jax.ShapeDtypeStruct((B, T, NSA_WIDTH), BF16),
        scratch_shapes=[pltpu.VMEM((NSA_KV_HEADS, T, KV_WIDTH), BF16)],
        compiler_params=_params("parallel", "arbitrary"),
        name="att_prompt",
    )(nqT, biasT, sk, svT, wk, wvT, gtT, ocmpT)


def _finish_kernel(x_ref, ry_ref, ny_ref, wo_ref, l1_ref, l2_ref, l3_ref, wu_ref, wd_ref, o_ref):
    mix = _dot(ry_ref[...], wo_ref[0:RET_WIDTH, :]) + _dot(ny_ref[...], wo_ref[RET_WIDTH:, :])
    h = x_ref[...] + _rms(mix, l1_ref[...])
    u = _dot(_rms(h, l2_ref[...]).astype(BF16), wu_ref[...])
    a = jnp.square(jnp.maximum(u, 0.0)).astype(BF16)
    f = _dot(a, wd_ref[...])
    o_ref[...] = h + _rms(f, l3_ref[...])


def _finish(x2d, ret_y, nsa_y, w_o, l1, l2, l3, w_up, w_down, tile):
    rows = x2d.shape[0]
    row = lambda w: pl.BlockSpec((tile, w), lambda i: (i, 0))
    const = lambda a, b: pl.BlockSpec((a, b), lambda i: (0, 0))
    return pl.pallas_call(
        _finish_kernel,
        grid=(rows // tile,),
        in_specs=[row(D_MODEL), row(RET_WIDTH), row(NSA_WIDTH),
                  const(D_MODEL, D_MODEL), const(1, D_MODEL), const(1, D_MODEL), const(1, D_MODEL),
                  const(D_MODEL, D_FF), const(D_FF, D_MODEL)],
        out_specs=row(D_MODEL),
        out_shape=jax.ShapeDtypeStruct((rows, D_MODEL), F32),
        compiler_params=_params("parallel"),
        name="finish",
    )(x2d, ret_y, nsa_y, w_o, l1, l2, l3, w_up, w_down)


def _page_copy(pool_ref, buf_ref, sem_ref, pt_ref, b, p, slot, n_pages):
    dst = buf_ref.at[pl.ds((slot * n_pages + p) * PAGE_SIZE, PAGE_SIZE), :]
    return pltpu.make_async_copy(pool_ref.at[pt_ref[b, p]], dst, sem_ref.at[slot])


def _head_rows(q_row):
    q_row = q_row.astype(F32)
    z = jnp.zeros((1, HEAD_DIM), F32)
    rows = []
    for hh in range(NSA_HEADS):
        piece = q_row[:, hh * HEAD_DIM:(hh + 1) * HEAD_DIM]
        rows.append(jnp.concatenate([piece, z] if hh < NSA_GROUP else [z, piece], axis=1))
    return jnp.concatenate(rows, axis=0)


def _cmp_sample_kernel(pt_ref, nq_ref, poolk_ref, poolv_ref,
                       posk_ref, w1k_ref, w2k_ref, posv_ref, w1v_ref, w2v_ref,
                       ocmp_ref, idx_ref, kbuf, vbuf, ksem, vsem, *, past_len):
    b = pl.program_id(0)
    n_pages = pt_ref.shape[1]
    nb = past_len // CMP_BLOCK
    slot = b % 2

    def fetch(bb, sl, start):
        for p in range(n_pages):
            for pool, buf, sem in ((poolk_ref, kbuf, ksem), (poolv_ref, vbuf, vsem)):
                cp = _page_copy(pool, buf, sem, pt_ref, bb, p, sl, n_pages)
                cp.start() if start else cp.wait()

    @pl.when(b == 0)
    def _():
        fetch(0, 0, True)

    @pl.when(b + 1 < pl.num_programs(0))
    def _():
        fetch(b + 1, 1 - slot, True)

    fetch(b, slot, False)

    base = slot * n_pages * PAGE_SIZE

    def rows_of(buf):
        return lambda l: buf[pl.ds(base + l, nb, stride=CMP_BLOCK), :]

    kc = _compress(rows_of(kbuf), posk_ref, w1k_ref, w2k_ref).astype(BF16)
    vc = _compress(rows_of(vbuf), posv_ref, w1v_ref, w2v_ref).astype(BF16)

    qh = _head_rows(nq_ref[0]).astype(BF16)
    n_lane = lax.broadcasted_iota(I32, (NSA_HEADS, nb), 1)
    valid = (n_lane + 1) * CMP_BLOCK - 1 <= past_len
    s = jnp.where(valid, _dot_nt(qh, kc) * SCALE, NEG)
    e = jnp.exp(s - jnp.max(s, axis=-1, keepdims=True))
    p = jnp.where(valid, e / jnp.sum(e, axis=-1, keepdims=True), 0.0)
    ocmp_ref[0] = _dot(p.astype(BF16), vc)

    row_i = lax.broadcasted_iota(I32, (nb, nb), 0)
    col_i = lax.broadcasted_iota(I32, (nb, nb), 1)
    eye = row_i == col_i
    idx_rows = []
    for g in range(NSA_KV_HEADS):
        imp = jnp.sum(p[g * NSA_GROUP:(g + 1) * NSA_GROUP], axis=0, keepdims=True)
        imp = jnp.where(n_lane[0:1] < past_len // CMP_BLOCK, imp, -1.0)
        imp_col = jnp.sum(jnp.where(eye, imp, 0.0), axis=1, keepdims=True)
        ahead = (imp > imp_col) | ((imp == imp_col) & (col_i < row_i))
        rank_col = jnp.sum(jnp.where(ahead, 1, 0), axis=1, keepdims=True)
        idx_rows.append(jnp.sum(jnp.where(rank_col == col_i, row_i, 0), axis=0, keepdims=True))
    idx_ref[0] = jnp.concatenate(idx_rows, axis=0)


def _cmp_sample(page_table, nq, pool_k, pool_v, cwk, cwv):
    DB, n_pages = page_table.shape
    past_len = n_pages * PAGE_SIZE
    nb = past_len // CMP_BLOCK
    c2 = lambda b, pt: (0, 0)
    c3 = lambda b, pt: (0, 0, 0)
    wspecs = [pl.BlockSpec((CMP_BLOCK, KV_WIDTH), c2),
              pl.BlockSpec((CMP_BLOCK, KV_WIDTH, KV_WIDTH), c3),
              pl.BlockSpec((KV_WIDTH, KV_WIDTH), c2)]
    any_spec = pl.BlockSpec(memory_space=pl.ANY)
    grid_spec = pltpu.PrefetchScalarGridSpec(
        num_scalar_prefetch=1,
        grid=(DB,),
        in_specs=[pl.BlockSpec((1, 1, NSA_WIDTH), lambda b, pt: (b, 0, 0)), any_spec, any_spec]
        + wspecs + wspecs,
        out_specs=[pl.BlockSpec((1, NSA_HEADS, KV_WIDTH), lambda b, pt: (b, 0, 0)),
                   pl.BlockSpec((1, NSA_KV_HEADS, nb), lambda b, pt: (b, 0, 0))],
        scratch_shapes=[pltpu.VMEM((2 * n_pages * PAGE_SIZE, KV_WIDTH), F32),
                        pltpu.VMEM((2 * n_pages * PAGE_SIZE, KV_WIDTH), F32),
                        pltpu.SemaphoreType.DMA((2,)),
                        pltpu.SemaphoreType.DMA((2,))],
    )
    return pl.pallas_call(
        functools.partial(_cmp_sample_kernel, past_len=past_len),
        grid_spec=grid_spec,
        out_shape=[jax.ShapeDtypeStruct((DB, NSA_HEADS, KV_WIDTH), F32),
                   jax.ShapeDtypeStruct((DB, NSA_KV_HEADS, nb), I32)],
        compiler_params=_params("arbitrary"),
        name="cmp_sample",
    )(page_table, nq.reshape(DB, 1, NSA_WIDTH), pool_k, pool_v, *cwk, *cwv)


N_PICK = N_SELECT - 1


def _att_sample_kernel(idx_ref, pt_ref, nq_ref, knew_ref, vnew_ref, wknew_ref, wvnew_ref,
                       gt_ref, ocmp_ref, poolk_ref, poolv_ref, wk_ref, wv_ref,
                       y_ref, wk_out_ref, wv_out_ref, kbuf, vbuf, ksem, vsem, *, past_len):
    b = pl.program_id(0)

    def gather(start):
        for g in range(NSA_KV_HEADS):
            for s in range(N_PICK):
                blk = idx_ref[(b * NSA_KV_HEADS + g) * N_SELECT + s]
                page = pt_ref[b, blk // 2]
                src_rows = pl.ds(pl.multiple_of((blk % 2) * CMP_BLOCK, CMP_BLOCK), CMP_BLOCK)
                dst_rows = pl.ds((g * N_PICK + s) * CMP_BLOCK, CMP_BLOCK)
                for pool, buf, sem in ((poolk_ref, kbuf, ksem), (poolv_ref, vbuf, vsem)):
                    cp = pltpu.make_async_copy(pool.at[page, src_rows, :], buf.at[dst_rows, :], sem.at[0])
                    cp.start() if start else cp.wait()

    gather(True)

    qf = _head_rows(nq_ref[0])
    qh = qf.astype(BF16)
    row_g = lax.broadcasted_iota(I32, (NSA_HEADS, KV_WIDTH), 0) // NSA_GROUP
    lane_g = lax.broadcasted_iota(I32, (NSA_HEADS, KV_WIDTH), 1) // HEAD_DIM

    def bf_round(row):
        return row.astype(BF16).astype(F32)

    def attend(s_past, v_past, k_new, v_new):
        s_self = jnp.sum(qf * bf_round(k_new), axis=1, keepdims=True) * SCALE
        m = jnp.maximum(jnp.max(s_past, axis=-1, keepdims=True), s_self)
        p = jnp.exp(s_past - m)
        p_self = jnp.exp(s_self - m)
        l = jnp.sum(p, axis=-1, keepdims=True) + p_self
        o = _dot(p.astype(BF16), v_past) + bf_round(p_self) * bf_round(v_new)
        return o / l

    wk = wk_ref[0]
    wv = wv_ref[0]
    wb = wk.shape[0]
    j = lax.broadcasted_iota(I32, (NSA_HEADS, wb), 1)
    pos_k = past_len - wb + j
    keep = (past_len - pos_k < WINDOW) & (pos_k >= 0)
    s_win = jnp.where(keep, _dot_nt(qh, wk.astype(BF16)) * SCALE, NEG)
    o_win = attend(s_win, wv.astype(BF16), wknew_ref[0], wvnew_ref[0])

    row_w = lax.broadcasted_iota(I32, (wb, KV_WIDTH), 0)
    wk_out_ref[0] = jnp.where(row_w == wb - 1, wknew_ref[0], pltpu.roll(wk, wb - 1, 0))
    wv_out_ref[0] = jnp.where(row_w == wb - 1, wvnew_ref[0], pltpu.roll(wv, wb - 1, 0))

    gather(False)
    n_keys = N_PICK * CMP_BLOCK
    o_sel = None
    for g in range(NSA_KV_HEADS):
        k = kbuf[g * n_keys:(g + 1) * n_keys, :].astype(BF16)
        v = vbuf[g * n_keys:(g + 1) * n_keys, :].astype(BF16)
        o_g = attend(_dot_nt(qh, k) * SCALE, v, knew_ref[0], vnew_ref[0])
        o_sel = o_g if o_sel is None else jnp.where(row_g == g, o_g, o_sel)

    gt = gt_ref[0]
    lane = lax.broadcasted_iota(I32, (NSA_HEADS, GATE_PAD), 1)
    head = lax.broadcasted_iota(I32, (NSA_HEADS, GATE_PAD), 0)

    def gate(branch):
        return jnp.sum(jnp.where(lane == N_BRANCH * head + branch, gt, 0.0), axis=1, keepdims=True)

    y = gate(0) * ocmp_ref[0] + gate(1) * o_sel + gate(2) * o_win
    pieces = []
    for hh in range(NSA_HEADS):
        g = hh // NSA_GROUP
        pieces.append(y[hh:hh + 1, g * HEAD_DIM:(g + 1) * HEAD_DIM])
    y_ref[0] = jnp.concatenate(pieces, axis=1)


def _att_sample(idx, page_table, nq, sk, sv, wk_new, wv_new, gates, ocmp, pool_k, pool_v, buf_k, buf_v):
    DB, n_pages = page_table.shape
    past_len = n_pages * PAGE_SIZE
    wb = buf_k.shape[1]
    r3 = lambda t: t.reshape(DB, 1, t.shape[-1])
    row = lambda w: pl.BlockSpec((1, 1, w), lambda b, ix, pt: (b, 0, 0))
    any_spec = pl.BlockSpec(memory_space=pl.ANY)
    wspec = pl.BlockSpec((1, wb, KV_WIDTH), lambda b, ix, pt: (b, 0, 0))
    hspec = pl.BlockSpec((1, NSA_HEADS, KV_WIDTH), lambda b, ix, pt: (b, 0, 0))
    grid_spec = pltpu.PrefetchScalarGridSpec(
        num_scalar_prefetch=2,
        grid=(DB,),
        in_specs=[row(NSA_WIDTH), row(KV_WIDTH), row(KV_WIDTH), row(KV_WIDTH), row(KV_WIDTH),
                  row(GATE_PAD), hspec, any_spec, any_spec, wspec, wspec],
        out_specs=[row(NSA_WIDTH), wspec, wspec],
        scratch_shapes=[pltpu.VMEM((NSA_KV_HEADS * N_PICK * CMP_BLOCK, KV_WIDTH), F32),
                        pltpu.VMEM((NSA_KV_HEADS * N_PICK * CMP_BLOCK, KV_WIDTH), F32),
                        pltpu.SemaphoreType.DMA((1,)),
                        pltpu.SemaphoreType.DMA((1,))],
    )
    return pl.pallas_call(
        functools.partial(_att_sample_kernel, past_len=past_len),
        grid_spec=grid_spec,
        out_shape=[jax.ShapeDtypeStruct((DB, 1, NSA_WIDTH), F32),
                   jax.ShapeDtypeStruct(buf_k.shape, F32),
                   jax.ShapeDtypeStruct(buf_v.shape, F32)],
        compiler_params=_params("arbitrary"),
        name="att_sample",
    )(idx, page_table, r3(nq), r3(sk), r3(sv), r3(wk_new), r3(wv_new), r3(gates), ocmp,
      pool_k, pool_v, buf_k, buf_v)


def _row_tile(rows):
    return 256 if rows % 256 == 0 else 128


def kernel(x_prompt, x_sample, state_ret, cache_cmp_k, cache_cmp_v, cache_sel_k, cache_sel_v,
           cache_win_k, cache_win_v, page_table, ln_pre_mix, w_in, ret_gn_w,
           cmp_pos_k, cmp_w1_k, cmp_w2_k, cmp_pos_v, cmp_w1_v, cmp_w2_v,
           w_o, ln_post_mix, ln_pre_ffn, w_up, w_down, ln_post_ffn):
    B, T, _ = x_prompt.shape
    DB, Tn, _ = x_sample.shape
    n_pages = page_table.shape[1]
    past_len = n_pages * PAGE_SIZE
    assert w_in.shape[0] == 1 and Tn == 1
    assert T % ATT_Q == 0 and T >= WINDOW and past_len // CMP_BLOCK >= N_PICK
    wb = cache_win_k.shape[2]

    w_main = w_in[0, :, :MAIN_WIDTH].astype(BF16)
    w_gate = jnp.pad(w_in[0, :, MAIN_WIDTH:], ((0, 0), (0, GATE_PAD - NSA_HEADS * N_BRANCH))).astype(BF16)
    cwk = _compress_weights(cmp_pos_k[0], cmp_w1_k[0], cmp_w2_k[0])
    cwv = _compress_weights(cmp_pos_v[0], cmp_w1_v[0], cmp_w2_v[0])
    tail = (w_o[0].astype(BF16), ln_post_mix, ln_pre_ffn, ln_post_ffn, w_up[0].astype(BF16), w_down[0].astype(BF16))
    kv4 = lambda t, n: t.reshape(1, n, -1, NSA_KV_HEADS, HEAD_DIM)

    tile_p = _row_tile(B * T)
    cos_p, sin_p = _rope_tables(jnp.arange(T, dtype=I32))
    xp = x_prompt.reshape(B * T, D_MODEL)
    (rq, rk, rv, rg, nqT, ck, cv, sk, wk, ckT, cvT, skT, svT, wkT, wvT, svTb, wvTb, gtT) = _inproj_prompt(
        xp, ln_pre_mix, w_main, w_gate, cos_p, sin_p, B, T, tile_p)
    b3 = lambda t: t.reshape(B, T, t.shape[-1])
    ret_y, rs_p = _ret_prompt(b3(rq), b3(rk), b3(rv), b3(rg), ret_gn_w)
    ocmpT, biasT = _cmp_prompt(b3(ck), b3(cv), cwk, cwv, nqT, gtT, _row_tile(T))
    nsa_y = _att_prompt(nqT, biasT, b3(sk), svTb, b3(wk), wvTb, gtT, ocmpT)
    y_p = _finish(xp, ret_y.reshape(B * T, RET_WIDTH), nsa_y.reshape(B * T, NSA_WIDTH), *tail, tile_p)
    win_p = min(WINDOW, T)
    kvT5 = lambda t: t.reshape(B, NSA_KV_HEADS, HEAD_DIM, t.shape[-1]).transpose(0, 3, 1, 2)[None]
    outs_p = (y_p.reshape(B, T, D_MODEL), rs_p[None],
              kvT5(ckT), kvT5(cvT), kvT5(skT), kvT5(svT),
              kvT5(wkT[:, :, T - win_p:]), kvT5(wvT[:, :, T - win_p:]))

    tile_s = DB
    pos_s = jnp.full((tile_s,), past_len, I32)
    cos_s, sin_s = _rope_tables(pos_s)
    xs = x_sample.reshape(DB, D_MODEL)
    rq, rk, rv, rg, nq, ck, cv, sk, sv, wk, wv, gt = _inproj(xs, ln_pre_mix, w_main, w_gate, cos_s, sin_s, tile_s)
    ret_y, rs_s = _ret_sample(rq, rk, rv, rg, ret_gn_w, state_ret[0])
    pool = lambda t: t.reshape(t.shape[1], PAGE_SIZE, KV_WIDTH)
    ocmp, idx = _cmp_sample(page_table, nq, pool(cache_cmp_k), pool(cache_cmp_v), cwk, cwv)
    idx_flat = idx[:, :, :N_SELECT].reshape(-1)
    y8, nwk, nwv = _att_sample(idx_flat, page_table, nq, sk, sv, wk, wv, gt, ocmp,
                               pool(cache_sel_k), pool(cache_sel_v),
                               cache_win_k.reshape(DB, wb, KV_WIDTH), cache_win_v.reshape(DB, wb, KV_WIDTH))
    nsa_y = y8.reshape(DB, NSA_WIDTH).astype(BF16)
    y_s = _finish(xs, ret_y, nsa_y, *tail, tile_s)
    outs_s = (y_s.reshape(DB, 1, D_MODEL), rs_s[None],
              kv4(ck, DB), kv4(cv, DB), kv4(sk, DB), kv4(sv, DB),
              nwk.reshape(1, DB, wb, NSA_KV_HEADS, HEAD_DIM), nwv.reshape(1, DB, wb, NSA_KV_HEADS, HEAD_DIM))

    return (outs_p[0], outs_s[0]) + outs_p[1:] + outs_s[1:]
```

```python
import functools

import numpy as np
import jax
import jax.numpy as jnp
from jax import lax
from jax.experimental import pallas as pl
from jax.experimental.pallas import tpu as pltpu

F32 = jnp.float32
BF16 = jnp.bfloat16
I32 = jnp.int32

D_MODEL = 1024
HEAD_DIM = 64
RET_WIDTH = 512
RET_HEADS = 8
NSA_WIDTH = 512
NSA_HEADS = 8
NSA_KV_HEADS = 2
NSA_GROUP = 4
KV_WIDTH = 128
N_BRANCH = 3
D_FF = 4 * D_MODEL
CMP_BLOCK = 64
N_SELECT = 16
N_PICK = N_SELECT - 1
WINDOW = 512
PAGE_SIZE = 128
RET_CHUNK = 128
ROPE_BASE = 10000.0
EPS = 1e-6
NEG = -1e30
SCALE = HEAD_DIM ** -0.5

MAIN_WIDTH = 4 * RET_WIDTH + NSA_WIDTH + 6 * KV_WIDTH
GATE_PAD = 128
ATT_Q = 128
ATT_K = 256
MAX_BLOCKS = 64
CMP_SAMPLES = 2
VMEM_LIMIT = 56 * 1024 * 1024

_LOG_GAMMA = [float(np.log1p(-np.exp2(-5.0 - h))) for h in range(RET_HEADS)]


def _params(*sem):
    return pltpu.CompilerParams(dimension_semantics=sem, vmem_limit_bytes=VMEM_LIMIT)


def _dot(a, b):
    return jnp.dot(a, b, preferred_element_type=F32)


def _dot_nt(a, b):
    return lax.dot_general(a, b, (((1,), (1,)), ((), ())), preferred_element_type=F32)


def _dot_tn(a, b):
    return lax.dot_general(a, b, (((0,), (0,)), ((), ())), preferred_element_type=F32)


def _rms(x, w):
    return x * lax.rsqrt(jnp.mean(x * x, axis=-1, keepdims=True) + EPS) * w


def _projector(x_ref, lnw_ref, w_ref, cos_ref, sin_ref):
    h = _rms(x_ref[...], lnw_ref[...]).astype(BF16)
    rows = h.shape[0]

    def proj(a, b):
        return _dot(h, w_ref[:, a:b])

    cos = cos_ref[...]
    sin = sin_ref[...]
    lane = lax.broadcasted_iota(I32, (rows, RET_WIDTH), 1)
    first_half = (lane % HEAD_DIM) < (HEAD_DIM // 2)

    def rope(y):
        partner = jnp.where(first_half,
                            pltpu.roll(y, RET_WIDTH - HEAD_DIM // 2, 1),
                            pltpu.roll(y, HEAD_DIM // 2, 1))
        return y * cos + partner * sin

    return h, proj, rope


def _rope_tables(pos):
    half = HEAD_DIM // 2
    inv = ROPE_BASE ** (-jnp.arange(half, dtype=F32) / half)
    ang = pos.astype(F32)[:, None] * inv
    cos, sin = jnp.cos(ang), jnp.sin(ang)
    cos_h = jnp.concatenate([cos, cos], -1)
    sin_h = jnp.concatenate([-sin, sin], -1)
    return jnp.tile(cos_h, (1, RET_HEADS)), jnp.tile(sin_h, (1, RET_HEADS))


def _inproj_prompt_kernel(x_ref, lnw_ref, w_ref, wg_ref, cos_ref, sin_ref,
                          rq_ref, rk_ref, rv_ref, rg_ref, nqT_ref, ck_ref, cv_ref, sk_ref, wk_ref,
                          ckT_ref, cvT_ref, skT_ref, svT_ref, wkT_ref, wvT_ref, svTb_ref, wvTb_ref, gtT_ref):
    h, proj, rope = _projector(x_ref, lnw_ref, w_ref, cos_ref, sin_ref)
    rq_ref[...] = rope(proj(0, 512)).astype(BF16)
    rk_ref[...] = (rope(proj(512, 1024)) * SCALE).astype(BF16)
    rv_ref[...] = proj(1024, 1536).astype(BF16)
    rg_ref[...] = proj(1536, 2048)
    for n in range(NSA_WIDTH // KV_WIDTH):
        nq = proj(2048 + n * KV_WIDTH, 2048 + (n + 1) * KV_WIDTH) * SCALE
        nqT_ref[0, n * KV_WIDTH:(n + 1) * KV_WIDTH, :] = nq.T.astype(BF16)
    kv = [proj(2560 + n * KV_WIDTH, 2560 + (n + 1) * KV_WIDTH) for n in range(6)]
    ck_ref[...] = kv[0]
    cv_ref[...] = kv[1]
    sk_ref[...] = kv[2].astype(BF16)
    wk_ref[...] = kv[4].astype(BF16)
    for n, ref in enumerate((ckT_ref, cvT_ref, skT_ref, svT_ref, wkT_ref, wvT_ref)):
        ref[0] = kv[n].T
    svTb_ref[0] = kv[3].T.astype(BF16)
    wvTb_ref[0] = kv[5].T.astype(BF16)
    gtT_ref[0] = jax.nn.sigmoid(_dot(h, wg_ref[...])).T


def _inproj_prompt(x2d, lnw, w_main, w_gate, cos_t, sin_t, B, T, tile):
    rows = x2d.shape[0]
    nt = T // tile
    row = lambda i: (i, 0)
    const = lambda i: (0, 0)
    tab = lambda i: (i % nt, 0)
    colT = lambda i: (i // nt, 0, i % nt)
    rm = lambda w, dt: (pl.BlockSpec((tile, w), row), jax.ShapeDtypeStruct((rows, w), dt))
    tr = lambda w, dt: (pl.BlockSpec((1, w, tile), colT), jax.ShapeDtypeStruct((B, w, T), dt))
    outs = [rm(RET_WIDTH, BF16), rm(RET_WIDTH, BF16), rm(RET_WIDTH, BF16), rm(RET_WIDTH, F32),
            tr(NSA_WIDTH, BF16), rm(KV_WIDTH, F32), rm(KV_WIDTH, F32), rm(KV_WIDTH, BF16), rm(KV_WIDTH, BF16),
            *[tr(KV_WIDTH, F32)] * 6,
            tr(KV_WIDTH, BF16), tr(KV_WIDTH, BF16), tr(GATE_PAD, F32)]
    return pl.pallas_call(
        _inproj_prompt_kernel,
        grid=(rows // tile,),
        in_specs=[pl.BlockSpec((tile, D_MODEL), row),
                  pl.BlockSpec((1, D_MODEL), const),
                  pl.BlockSpec((D_MODEL, MAIN_WIDTH), const),
                  pl.BlockSpec((D_MODEL, GATE_PAD), const),
                  pl.BlockSpec((tile, RET_WIDTH), tab),
                  pl.BlockSpec((tile, RET_WIDTH), tab)],
        out_specs=[o[0] for o in outs],
        out_shape=[o[1] for o in outs],
        compiler_params=_params("parallel"),
        name="inproj_prompt",
    )(x2d, lnw, w_main, w_gate, cos_t, sin_t)


def _inproj_sample_kernel(x_ref, lnw_ref, w_ref, wg_ref, cos_ref, sin_ref,
                          rqT_ref, rkT_ref, rvT_ref, rgT_ref, nq_ref,
                          ckT_ref, cvT_ref, skT_ref, svT_ref, sk_ref, sv_ref, wk_ref, wv_ref, gt_ref):
    h, proj, rope = _projector(x_ref, lnw_ref, w_ref, cos_ref, sin_ref)
    rqT_ref[...] = rope(proj(0, 512)).T
    rkT_ref[...] = (rope(proj(512, 1024)) * SCALE).T
    rvT_ref[...] = proj(1024, 1536).T
    rgT_ref[...] = proj(1536, 2048).T
    nq_ref[...] = proj(2048, 2560) * SCALE
    kv = [proj(2560 + n * KV_WIDTH, 2560 + (n + 1) * KV_WIDTH) for n in range(6)]
    for n, ref in enumerate((ckT_ref, cvT_ref, skT_ref, svT_ref)):
        ref[...] = kv[n].T
    for n, ref in enumerate((sk_ref, sv_ref, wk_ref, wv_ref)):
        ref[...] = kv[2 + n]
    gt_ref[...] = jax.nn.sigmoid(_dot(h, wg_ref[...]))


def _inproj_sample(x2d, lnw, w_main, w_gate, cos_t, sin_t):
    rows = x2d.shape[0]
    full = lambda a, b: pl.BlockSpec((a, b), lambda i: (0, 0))
    rm = lambda w: (full(rows, w), jax.ShapeDtypeStruct((rows, w), F32))
    tr = lambda w: (full(w, rows), jax.ShapeDtypeStruct((w, rows), F32))
    outs = [tr(RET_WIDTH)] * 4 + [rm(NSA_WIDTH)] + [tr(KV_WIDTH)] * 4 + [rm(KV_WIDTH)] * 4 + [rm(GATE_PAD)]
    return pl.pallas_call(
        _inproj_sample_kernel,
        grid=(1,),
        in_specs=[full(rows, D_MODEL), full(1, D_MODEL), full(D_MODEL, MAIN_WIDTH), full(D_MODEL, GATE_PAD),
                  full(rows, RET_WIDTH), full(rows, RET_WIDTH)],
        out_specs=[o[0] for o in outs],
        out_shape=[o[1] for o in outs],
        compiler_params=_params("arbitrary"),
        name="inproj_sample",
    )(x2d, lnw, w_main, w_gate, cos_t, sin_t)


def _group_norm_gate(outs, g, gnw):
    ys = []
    for o in outs:
        mu = jnp.mean(o, axis=-1, keepdims=True)
        d = o - mu
        var = jnp.mean(d * d, axis=-1, keepdims=True)
        ys.append(d * lax.rsqrt(var + EPS))
    y = jnp.concatenate(ys, axis=1) * gnw
    return g * jax.nn.sigmoid(g) * y


def _ret_prompt_kernel(q_ref, k_ref, v_ref, g_ref, gnw_ref, inner_ref, qdec_ref, kdec_ref,
                       y_ref, s_out_ref, s_scr):
    c = pl.program_id(1)

    @pl.when(c == 0)
    def _():
        s_scr[...] = jnp.zeros_like(s_scr)

    q = q_ref[0]
    k = k_ref[0]
    v = v_ref[0]
    kd = (k.astype(F32) * kdec_ref[...]).astype(BF16)
    qdec = qdec_ref[...]
    outs = []
    for h in range(RET_HEADS):
        sl = slice(h * HEAD_DIM, (h + 1) * HEAD_DIM)
        qh, kh, vh = q[:, sl], k[:, sl], v[:, sl]
        a = _dot_nt(qh, kh) * inner_ref[h]
        s = s_scr[h]
        o = _dot(a.astype(BF16), vh) + _dot(qh, s.astype(BF16)) * qdec[:, sl]
        s_scr[h] = s * float(np.exp(_LOG_GAMMA[h] * RET_CHUNK)) + _dot_tn(kd[:, sl], vh)
        outs.append(o)
    y_ref[0] = _group_norm_gate(outs, g_ref[0], gnw_ref[...]).astype(BF16)

    @pl.when(c == pl.num_programs(1) - 1)
    def _():
        s_out_ref[0] = s_scr[...]


def _ret_prompt(rq, rk, rv, rg, gnw):
    B, T, _ = rq.shape
    C = RET_CHUNK
    lg = jnp.asarray(_LOG_GAMMA, F32)
    i = jnp.arange(C, dtype=F32)
    diff = i[:, None] - i[None, :]
    inner = jnp.where(diff >= 0, jnp.exp(lg[:, None, None] * jnp.maximum(diff, 0.0)), 0.0)
    qdec = jnp.repeat(jnp.exp(lg[None, :] * (i[:, None] + 1.0)), HEAD_DIM, axis=1)
    kdec = jnp.repeat(jnp.exp(lg[None, :] * (C - 1.0 - i[:, None])), HEAD_DIM, axis=1)
    blk = pl.BlockSpec((1, C, RET_WIDTH), lambda b, c: (b, c, 0))
    c2 = lambda b, c: (0, 0)
    return pl.pallas_call(
        _ret_prompt_kernel,
        grid=(B, T // C),
        in_specs=[blk, blk, blk, blk,
                  pl.BlockSpec((1, RET_WIDTH), c2),
                  pl.BlockSpec((RET_HEADS, C, C), lambda b, c: (0, 0, 0)),
                  pl.BlockSpec((C, RET_WIDTH), c2),
                  pl.BlockSpec((C, RET_WIDTH), c2)],
        out_specs=[blk, pl.BlockSpec((1, RET_HEADS, HEAD_DIM, HEAD_DIM), lambda b, c: (b, 0, 0, 0))],
        out_shape=[jax.ShapeDtypeStruct((B, T, RET_WIDTH), BF16),
                   jax.ShapeDtypeStruct((B, RET_HEADS, HEAD_DIM, HEAD_DIM), F32)],
        scratch_shapes=[pltpu.VMEM((RET_HEADS, HEAD_DIM, HEAD_DIM), F32)],
        compiler_params=_params("parallel", "arbitrary"),
        name="ret_prompt",
    )(rq, rk, rv, rg, gnw, inner, qdec, kdec)


def _ret_sample_kernel(q_ref, k_ref, v_ref, g_ref, gnw_ref, gam_ref, s_ref, y_ref, s_out_ref):
    q = q_ref[...]
    k = k_ref[...]
    v = v_ref[...]
    g = g_ref[...]
    gamma = gam_ref[0]
    acc = jnp.zeros(v.shape, F32)
    for d in range(HEAD_DIM):
        s_d = s_ref[0, d]
        acc = acc + s_d * q[d:d + 1, :]
        s_out_ref[0, d] = s_d * gamma + k[d:d + 1, :] * v
    o = acc * gamma + jnp.sum(q * k, axis=0, keepdims=True) * v
    mu = jnp.mean(o, axis=0, keepdims=True)
    dlt = o - mu
    var = jnp.mean(dlt * dlt, axis=0, keepdims=True)
    y_ref[...] = g * jax.nn.sigmoid(g) * (dlt * lax.rsqrt(var + EPS) * gnw_ref[...])


def _ret_sample(rqT, rkT, rvT, rgT, gnw, state_t):
    DB = rqT.shape[1]
    blk = pl.BlockSpec((HEAD_DIM, DB), lambda h: (h, 0))
    st = pl.BlockSpec((1, HEAD_DIM, HEAD_DIM, DB), lambda h: (h, 0, 0, 0))
    gam = jnp.broadcast_to(jnp.exp(jnp.asarray(_LOG_GAMMA, F32))[:, None, None], (RET_HEADS, 1, DB))
    return pl.pallas_call(
        _ret_sample_kernel,
        grid=(RET_HEADS,),
        in_specs=[blk, blk, blk, blk,
                  pl.BlockSpec((HEAD_DIM, 1), lambda h: (h, 0)),
                  pl.BlockSpec((1, 1, DB), lambda h: (h, 0, 0)),
                  st],
        out_specs=[blk, st],
        out_shape=[jax.ShapeDtypeStruct((RET_WIDTH, DB), F32),
                   jax.ShapeDtypeStruct(state_t.shape, F32)],
        compiler_params=_params("parallel"),
        name="ret_sample",
    )(rqT, rkT, rvT, rgT, gnw.reshape(RET_WIDTH, 1), gam, state_t)


def _compress(load_rows, pos_ref, w1_ref, w2_ref):
    acc = None
    for j2 in range(CMP_BLOCK // 2):
        x = jnp.concatenate([load_rows(2 * j2) + pos_ref[2 * j2:2 * j2 + 1, :],
                             load_rows(2 * j2 + 1) + pos_ref[2 * j2 + 1:2 * j2 + 2, :]], axis=1).astype(BF16)
        part = _dot(x, w1_ref[j2])
        acc = part if acc is None else acc + part
    h = jax.nn.gelu(acc)
    return _dot(h.astype(BF16), w2_ref[...])


def _block_diag2(w):
    z = jnp.zeros_like(w)
    return jnp.concatenate([jnp.concatenate([w, z], -1), jnp.concatenate([z, w], -1)], -2)


def _compress_weights(pos, w1, w2, token_major):
    if not token_major:
        pos, w1 = pos.T, w1.transpose(1, 0, 2)
    pos2 = jnp.concatenate([pos, pos], -1)
    w1p = _block_diag2(w1).reshape(CMP_BLOCK // 2, 2 * KV_WIDTH, KV_WIDTH)
    return pos2, w1p.astype(BF16), _block_diag2(w2).astype(BF16)


def _rank_rows(imp):
    nb, tq = imp.shape
    sub = 8
    groups = [imp[sub * j:sub * (j + 1)] for j in range(nb // sub)]
    ranks = [jnp.zeros((sub, tq), I32) for _ in groups]
    row_in_group = lax.broadcasted_iota(I32, (sub, tq), 0)
    for m in range(nb):
        row = imp[m:m + 1, :]
        jm, sm = divmod(m, sub)
        for j, grp in enumerate(groups):
            if j < jm:
                inc = jnp.where(row > grp, 1, 0)
            elif j > jm:
                inc = jnp.where(row >= grp, 1, 0)
            else:
                inc = jnp.where(row_in_group > sm, jnp.where(row >= grp, 1, 0), jnp.where(row > grp, 1, 0))
            ranks[j] = ranks[j] + inc
    return jnp.concatenate(ranks, axis=0)


def _kv_head_rows(x, g):
    z = jnp.zeros_like(x)
    return jnp.concatenate([x, z] if g == 0 else [z, x], axis=0)


def _cmp_prompt_kernel(ck_ref, cv_ref, posk_ref, w1k_ref, w2k_ref, posv_ref, w1v_ref, w2v_ref,
                       nqT_ref, gtT_ref, ocmpT_ref, biasT_ref, kc_scr, vc_scr):
    i = pl.program_id(1)
    nb = kc_scr.shape[0]
    tq = nqT_ref.shape[2]

    @pl.when(i == 0)
    def _():
        kc_scr[...] = _compress(lambda l: ck_ref[0, pl.ds(l, nb, stride=CMP_BLOCK), :],
                                posk_ref, w1k_ref, w2k_ref).astype(BF16)
        vc_scr[...] = _compress(lambda l: cv_ref[0, pl.ds(l, nb, stride=CMP_BLOCK), :],
                                posv_ref, w1v_ref, w2v_ref).astype(BF16)

    kc = kc_scr[...]
    vc = vc_scr[...]
    pos = i * tq + lax.broadcasted_iota(I32, (nb, tq), 1)
    n_idx = lax.broadcasted_iota(I32, (nb, tq), 0)
    valid = (n_idx + 1) * CMP_BLOCK - 1 <= pos
    cur = pos // CMP_BLOCK
    cand = n_idx < cur
    for g in range(NSA_KV_HEADS):
        imp = jnp.zeros((nb, tq), F32)
        for r in range(NSA_GROUP):
            hh = g * NSA_GROUP + r
            qT = _kv_head_rows(nqT_ref[0, hh * HEAD_DIM:(hh + 1) * HEAD_DIM, :], g)
            s = jnp.where(valid, _dot(kc, qT), NEG)
            e = jnp.exp(s - jnp.max(s, axis=0, keepdims=True))
            p = jnp.where(valid, e / jnp.sum(e, axis=0, keepdims=True), 0.0)
            imp = imp + p
            oT = _dot_tn(vc, p.astype(BF16))
            ocmpT_ref[0, hh * HEAD_DIM:(hh + 1) * HEAD_DIM, :] = (
                oT[g * HEAD_DIM:(g + 1) * HEAD_DIM] * gtT_ref[0, N_BRANCH * hh:N_BRANCH * hh + 1, :])
        imp = jnp.where(cand, imp, -1.0)
        rank = _rank_rows(imp)
        sel = (cand & (rank < N_PICK)) | (n_idx == cur)
        biasT_ref[0, g * MAX_BLOCKS:g * MAX_BLOCKS + nb, :] = jnp.where(sel, 0.0, NEG).astype(BF16)
        if nb < MAX_BLOCKS:
            biasT_ref[0, g * MAX_BLOCKS + nb:(g + 1) * MAX_BLOCKS, :] = jnp.zeros((MAX_BLOCKS - nb, tq), BF16)


def _cmp_weight_specs(c2, c3):
    return [pl.BlockSpec((CMP_BLOCK, KV_WIDTH), c2),
            pl.BlockSpec((CMP_BLOCK // 2, 2 * KV_WIDTH, KV_WIDTH), c3),
            pl.BlockSpec((KV_WIDTH, KV_WIDTH), c2)]


def _cmp_prompt(ck, cv, cwk, cwv, nqT, gtT, tq):
    B, T, _ = ck.shape
    nb = T // CMP_BLOCK
    full = pl.BlockSpec((1, T, KV_WIDTH), lambda b, i: (b, 0, 0))
    wspecs = _cmp_weight_specs(lambda b, i: (0, 0), lambda b, i: (0, 0, 0))
    blk = lambda w: pl.BlockSpec((1, w, tq), lambda b, i: (b, 0, i))
    return pl.pallas_call(
        _cmp_prompt_kernel,
        grid=(B, T // tq),
        in_specs=[full, full] + wspecs + wspecs + [blk(NSA_WIDTH), blk(GATE_PAD)],
        out_specs=[blk(NSA_WIDTH), blk(NSA_KV_HEADS * MAX_BLOCKS)],
        out_shape=[jax.ShapeDtypeStruct((B, NSA_WIDTH, T), F32),
                   jax.ShapeDtypeStruct((B, NSA_KV_HEADS * MAX_BLOCKS, T), BF16)],
        scratch_shapes=[pltpu.VMEM((nb, KV_WIDTH), BF16), pltpu.VMEM((nb, KV_WIDTH), BF16)],
        compiler_params=_params("parallel", "arbitrary"),
        name="cmp_prompt",
    )(ck, cv, *cwk, *cwv, nqT, gtT)


def _att_prompt_kernel(nqT_ref, biasT_ref, sk_ref, svT_ref, wk_ref, wvT_ref, gtT_ref, ocmpT_ref,
                       y_ref, ka_scr):
    i = pl.program_id(1)
    tq = nqT_ref.shape[2]
    T = sk_ref.shape[1]
    cols = NSA_GROUP * tq

    @pl.when(i == 0)
    def _():
        sk = sk_ref[0].astype(F32)
        lane = lax.broadcasted_iota(I32, (T, KV_WIDTH), 1)
        blk = lax.broadcasted_iota(I32, (T, KV_WIDTH), 0) // CMP_BLOCK
        ka_scr[0] = jnp.where(lane < HEAD_DIM, sk, jnp.where(lane - HEAD_DIM == blk, 1.0, 0.0)).astype(BF16)
        ka_scr[1] = jnp.where(lane >= HEAD_DIM, sk, jnp.where(lane == blk, 1.0, 0.0)).astype(BF16)

    q_pos = i * tq + lax.broadcasted_iota(I32, (ATT_K, cols), 1) % tq
    k_off = lax.broadcasted_iota(I32, (ATT_K, cols), 0)
    n_win = WINDOW // tq
    win_keys = (n_win + 1) * tq
    win_start = pl.multiple_of(jnp.maximum(i - n_win, 0) * tq, tq)
    rel = i * tq - win_start + lax.broadcasted_iota(I32, (win_keys, cols), 1) % tq
    j_win = lax.broadcasted_iota(I32, (win_keys, cols), 0)
    win_ok = (j_win <= rel) & (j_win > rel - WINDOW)
    wk = wk_ref[0, pl.ds(win_start, win_keys), :]
    wvT = wvT_ref[0, :, pl.ds(win_start, win_keys)]

    qaTs, qwTs = [], []
    for g in range(NSA_KV_HEADS):
        qTs = [nqT_ref[0, (g * NSA_GROUP + r) * HEAD_DIM:(g * NSA_GROUP + r + 1) * HEAD_DIM, :]
               for r in range(NSA_GROUP)]
        bT = biasT_ref[0, g * MAX_BLOCKS:(g + 1) * MAX_BLOCKS, :]
        qaTs.append(jnp.concatenate([jnp.concatenate([q, bT] if g == 0 else [bT, q], axis=0) for q in qTs],
                                    axis=1))
        qwTs.append(jnp.concatenate([_kv_head_rows(q, g) for q in qTs], axis=1))

    def sel_step(c, carries, last):
        start = pl.multiple_of(c * ATT_K, ATT_K)
        vT = svT_ref[0, :, pl.ds(start, ATT_K)]
        out = []
        for g, (m, l, acc) in enumerate(carries):
            s = _dot(ka_scr[g, pl.ds(start, ATT_K), :], qaTs[g])
            if last:
                s = jnp.where(start + k_off > q_pos, NEG, s)
            m_new = jnp.maximum(m, jnp.max(s, axis=0, keepdims=True))
            alpha = jnp.exp(m - m_new)
            p = jnp.exp(s - m_new)
            l = alpha * l + jnp.sum(p, axis=0, keepdims=True)
            acc = alpha * acc + _dot(vT, p.astype(BF16))
            out.append((m_new, l, acc))
        return tuple(out)

    init = (jnp.full((1, cols), NEG, F32), jnp.zeros((1, cols), F32), jnp.zeros((KV_WIDTH, cols), F32))
    n_before = (i * tq) // ATT_K
    carries = lax.fori_loop(0, n_before, lambda c, cr: sel_step(c, cr, False), (init, init))
    carries = sel_step(n_before, carries, True)

    heads_out = []
    for g in range(NSA_KV_HEADS):
        gs = slice(g * HEAD_DIM, (g + 1) * HEAD_DIM)
        _, l, acc = carries[g]
        o_sel = acc[gs] / l

        s = jnp.where(win_ok, _dot(wk, qwTs[g]), NEG)
        p = jnp.exp(s - jnp.max(s, axis=0, keepdims=True))
        o_win = _dot(wvT, p.astype(BF16))[gs] / jnp.sum(p, axis=0, keepdims=True)

        for r in range(NSA_GROUP):
            hh = g * NSA_GROUP + r
            cs = slice(r * tq, (r + 1) * tq)
            heads_out.append(ocmpT_ref[0, hh * HEAD_DIM:(hh + 1) * HEAD_DIM, :]
                             + gtT_ref[0, N_BRANCH * hh + 1:N_BRANCH * hh + 2, :] * o_sel[:, cs]
                             + gtT_ref[0, N_BRANCH * hh + 2:N_BRANCH * hh + 3, :] * o_win[:, cs])
    y_ref[0] = jnp.concatenate(heads_out, axis=0).T.astype(BF16)


def _att_prompt(nqT, biasT, sk, svT, wk, wvT, gtT, ocmpT):
    B, _, T = nqT.shape
    tq = ATT_Q
    rows_full = pl.BlockSpec((1, T, KV_WIDTH), lambda b, i: (b, 0, 0))
    cols_full = pl.BlockSpec((1, KV_WIDTH, T), lambda b, i: (b, 0, 0))
    blk = lambda w: pl.BlockSpec((1, w, tq), lambda b, i: (b, 0, i))
    return pl.pallas_call(
        _att_prompt_kernel,
        grid=(B, T // tq),
        in_specs=[blk(NSA_WIDTH), blk(NSA_KV_HEADS * MAX_BLOCKS), rows_full, cols_full, rows_full, cols_full,
                  blk(GATE_PAD), blk(NSA_WIDTH)],
        out_specs=pl.BlockSpec((1, tq, NSA_WIDTH), lambda b, i: (b, i, 0)),
        out_shape=jax.ShapeDtypeStruct((B, T, NSA_WIDTH), BF16),
        scratch_shapes=[pltpu.VMEM((NSA_KV_HEADS, T, KV_WIDTH), BF16)],
        compiler_params=_params("parallel", "arbitrary"),
        name="att_prompt",
    )(nqT, biasT, sk, svT, wk, wvT, gtT, ocmpT)


def _finish_kernel(x_ref, ry_ref, ny_ref, wo_ref, l1_ref, l2_ref, l3_ref, wu_ref, wd_ref, o_ref, *, ret_t):
    if ret_t:
        mix = _dot_tn(ry_ref[...].astype(BF16), wo_ref[0:RET_WIDTH, :])
    else:
        mix = _dot(ry_ref[...], wo_ref[0:RET_WIDTH, :])
    mix = mix + _dot(ny_ref[...].astype(BF16), wo_ref[RET_WIDTH:, :])
    h = x_ref[...] + _rms(mix, l1_ref[...])
    u = _dot(_rms(h, l2_ref[...]).astype(BF16), wu_ref[...])
    a = jnp.square(jnp.maximum(u, 0.0)).astype(BF16)
    f = _dot(a, wd_ref[...])
    o_ref[...] = h + _rms(f, l3_ref[...])


def _finish(x2d, ret_y, nsa_y, w_o, l1, l2, l3, w_up, w_down, tile, ret_t=False):
    rows = x2d.shape[0]
    row = lambda w: pl.BlockSpec((tile, w), lambda i: (i, 0))
    const = lambda a, b: pl.BlockSpec((a, b), lambda i: (0, 0))
    ret_spec = pl.BlockSpec((RET_WIDTH, tile), lambda i: (0, i)) if ret_t else row(RET_WIDTH)
    return pl.pallas_call(
        functools.partial(_finish_kernel, ret_t=ret_t),
        grid=(rows // tile,),
        in_specs=[row(D_MODEL), ret_spec, row(NSA_WIDTH),
                  const(D_MODEL, D_MODEL), const(1, D_MODEL), const(1, D_MODEL), const(1, D_MODEL),
                  const(D_MODEL, D_FF), const(D_FF, D_MODEL)],
        out_specs=row(D_MODEL),
        out_shape=jax.ShapeDtypeStruct((rows, D_MODEL), F32),
        compiler_params=_params("parallel"),
        name="finish",
    )(x2d, ret_y, nsa_y, w_o, l1, l2, l3, w_up, w_down)


def _column(row):
    n = row.shape[1]
    eye = lax.broadcasted_iota(I32, (n, n), 0) == lax.broadcasted_iota(I32, (n, n), 1)
    return jnp.sum(jnp.where(eye, row, 0.0), axis=1, keepdims=True)


def _cmp_sample_kernel(pt_ref, nq_ref, poolk_ref, poolv_ref,
                       posk_ref, w1k_ref, w2k_ref, posv_ref, w1v_ref, w2v_ref,
                       ocmp_ref, idx_ref, kbuf, vbuf, ksem, vsem):
    step = pl.program_id(0)
    n_pages = pt_ref.shape[1]
    nbk = 2 * n_pages
    slot = step % 2

    def fetch(st, sl, start):
        for j in range(CMP_SAMPLES):
            for p in range(n_pages):
                rows = pl.ds(((sl * CMP_SAMPLES + j) * n_pages + p) * KV_WIDTH, KV_WIDTH)
                page = pt_ref[st * CMP_SAMPLES + j, p]
                for pool, buf, sem in ((poolk_ref, kbuf, ksem), (poolv_ref, vbuf, vsem)):
                    cp = pltpu.make_async_copy(pool.at[page], buf.at[rows, :], sem.at[sl])
                    cp.start() if start else cp.wait()

    @pl.when(step == 0)
    def _():
        fetch(0, 0, True)

    @pl.when(step + 1 < pl.num_programs(0))
    def _():
        fetch(step + 1, 1 - slot, True)

    fetch(step, slot, False)

    step_pages = CMP_SAMPLES * n_pages
    base = slot * step_pages * KV_WIDTH

    def rows_of(buf):
        def load(d):
            parts = [buf[pl.ds(base + g * HEAD_DIM + d, step_pages, stride=KV_WIDTH), :]
                     for g in range(NSA_KV_HEADS)]
            return jnp.concatenate(parts, axis=0)
        return load

    kc = _compress(rows_of(kbuf), posk_ref, w1k_ref, w2k_ref).astype(BF16)
    vc = _compress(rows_of(vbuf), posv_ref, w1v_ref, w2v_ref).astype(BF16)

    j_lane = lax.broadcasted_iota(I32, (nbk, nbk), 1)
    j_sub = lax.broadcasted_iota(I32, (nbk, nbk), 0)
    n_lane = 2 * (j_lane % n_pages) + j_lane // n_pages
    n_sub = 2 * (j_sub % n_pages) + j_sub // n_pages
    z = jnp.zeros((1, HEAD_DIM), F32)
    for j in range(CMP_SAMPLES):
        q = nq_ref[j]
        o_rows, idx_rows = [], []
        for g in range(NSA_KV_HEADS):
            blk_rows = slice((g * CMP_SAMPLES + j) * n_pages, (g * CMP_SAMPLES + j + 1) * n_pages)
            pieces = [q[:, (g * NSA_GROUP + r) * HEAD_DIM:(g * NSA_GROUP + r + 1) * HEAD_DIM]
                      for r in range(NSA_GROUP)]
            qh = jnp.concatenate([jnp.concatenate([pc, z], axis=1) for pc in pieces]
                                 + [jnp.concatenate([z, pc], axis=1) for pc in pieces], axis=0)
            s = _dot_nt(qh.astype(BF16), kc[blk_rows])
            m = jnp.max(s, axis=1, keepdims=True)
            m = jnp.maximum(m[:NSA_GROUP], m[NSA_GROUP:])
            e = jnp.exp(s - jnp.concatenate([m, m], axis=0))
            l = jnp.sum(e, axis=1, keepdims=True)
            l = l[:NSA_GROUP] + l[NSA_GROUP:]
            p = e / jnp.concatenate([l, l], axis=0)
            out = _dot(p.astype(BF16), vc[blk_rows])
            o_rows.append(out[:NSA_GROUP, :HEAD_DIM] + out[NSA_GROUP:, HEAD_DIM:])
            imp = jnp.concatenate([jnp.sum(p[:NSA_GROUP], axis=0, keepdims=True),
                                   jnp.sum(p[NSA_GROUP:], axis=0, keepdims=True)], axis=1)
            imp_col = _column(imp)
            ahead = (imp > imp_col) | ((imp == imp_col) & (n_lane < n_sub))
            rank_col = jnp.sum(jnp.where(ahead, 1, 0), axis=1, keepdims=True)
            idx_rows.append(jnp.sum(jnp.where(rank_col == j_lane, n_sub, 0), axis=0, keepdims=True))
        ocmp_ref[j] = jnp.concatenate([o[r:r + 1] for o in o_rows for r in range(NSA_GROUP)], axis=1)
        idx_ref[j] = jnp.concatenate(idx_rows, axis=0)


def _cmp_sample(page_table, nq, pool_k, pool_v, cwk, cwv):
    DB, n_pages = page_table.shape
    nbk = 2 * n_pages
    wspecs = _cmp_weight_specs(lambda s, pt: (0, 0), lambda s, pt: (0, 0, 0))
    any_spec = pl.BlockSpec(memory_space=pl.ANY)
    buf_rows = 2 * CMP_SAMPLES * n_pages * KV_WIDTH
    grid_spec = pltpu.PrefetchScalarGridSpec(
        num_scalar_prefetch=1,
        grid=(DB // CMP_SAMPLES,),
        in_specs=[pl.BlockSpec((CMP_SAMPLES, 1, NSA_WIDTH), lambda s, pt: (s, 0, 0)), any_spec, any_spec]
        + wspecs + wspecs,
        out_specs=[pl.BlockSpec((CMP_SAMPLES, 1, NSA_WIDTH), lambda s, pt: (s, 0, 0)),
                   pl.BlockSpec((CMP_SAMPLES, NSA_KV_HEADS, nbk), lambda s, pt: (s, 0, 0))],
        scratch_shapes=[pltpu.VMEM((buf_rows, PAGE_SIZE), F32),
                        pltpu.VMEM((buf_rows, PAGE_SIZE), F32),
                        pltpu.SemaphoreType.DMA((2,)),
                        pltpu.SemaphoreType.DMA((2,))],
    )
    return pl.pallas_call(
        _cmp_sample_kernel,
        grid_spec=grid_spec,
        out_shape=[jax.ShapeDtypeStruct((DB, 1, NSA_WIDTH), F32),
                   jax.ShapeDtypeStruct((DB, NSA_KV_HEADS, nbk), I32)],
        compiler_params=_params("arbitrary"),
        name="cmp_sample",
    )(page_table, nq.reshape(DB, 1, NSA_WIDTH), pool_k, pool_v, *cwk, *cwv)


def _att_sample_kernel(idx_ref, pt_ref, nq_ref, knew_ref, vnew_ref, wknew_ref, wvnew_ref,
                       gt_ref, ocmp_ref, poolk_ref, poolv_ref, wkT_ref, wvT_ref,
                       y_ref, wk_out_ref, wv_out_ref, kbuf, vbuf, ksem, vsem, *, past_len):
    b = pl.program_id(0)
    slot = b % 2

    def gather(bb, sl, start):
        for g in range(NSA_KV_HEADS):
            for s in range(N_PICK):
                blk = idx_ref[(bb * NSA_KV_HEADS + g) * N_SELECT + s]
                page = pt_ref[bb, blk // 2]
                for pool, buf, sem in ((poolk_ref, kbuf, ksem), (poolv_ref, vbuf, vsem)):
                    cp = pltpu.make_async_copy(pool.at[page, pl.ds(g * HEAD_DIM, HEAD_DIM), :],
                                               buf.at[sl, g, :, pl.ds(s * PAGE_SIZE, PAGE_SIZE)], sem.at[sl])
                    cp.start() if start else cp.wait()

    @pl.when(b == 0)
    def _():
        gather(0, 0, True)

    @pl.when(b + 1 < pl.num_programs(0))
    def _():
        gather(b + 1, 1 - slot, True)

    q = nq_ref[0]
    pieces = [q[:, hh * HEAD_DIM:(hh + 1) * HEAD_DIM] for hh in range(NSA_HEADS)]
    z = jnp.zeros((1, HEAD_DIM), F32)
    q_wide = jnp.concatenate([jnp.concatenate([pc, z] if hh < NSA_GROUP else [z, pc], axis=1)
                              for hh, pc in enumerate(pieces)], axis=0)

    def bf_round(x):
        return x.astype(BF16).astype(F32)

    def attend(qf, s_past, vT_past, k_new, v_new):
        s_self = jnp.sum(bf_round(qf) * bf_round(k_new), axis=1, keepdims=True)
        m = jnp.maximum(jnp.max(s_past, axis=-1, keepdims=True), s_self)
        p = jnp.exp(s_past - m)
        p_self = jnp.exp(s_self - m)
        l = jnp.sum(p, axis=-1, keepdims=True) + p_self
        o = _dot_nt(p.astype(BF16), vT_past) + bf_round(p_self) * bf_round(v_new)
        return o / l

    wkT = wkT_ref[0]
    wvT = wvT_ref[0]
    wb = wkT.shape[1]
    j = lax.broadcasted_iota(I32, (NSA_HEADS, wb), 1)
    pos_k = past_len - wb + j
    keep = (past_len - pos_k < WINDOW) & (pos_k >= 0)
    s_win = jnp.where(keep, _dot(q_wide.astype(BF16), wkT.astype(BF16)), NEG)
    o_win = attend(q_wide, s_win, wvT.astype(BF16), wknew_ref[0], wvnew_ref[0])

    lane_w = lax.broadcasted_iota(I32, (KV_WIDTH, wb), 1)
    wk_out_ref[0] = jnp.where(lane_w == wb - 1, _column(wknew_ref[0]), pltpu.roll(wkT, wb - 1, 1))
    wv_out_ref[0] = jnp.where(lane_w == wb - 1, _column(wvnew_ref[0]), pltpu.roll(wvT, wb - 1, 1))

    gather(b, slot, False)
    n_keys = N_PICK * PAGE_SIZE
    half_of_lane = (lax.broadcasted_iota(I32, (1, n_keys), 1) % PAGE_SIZE) // CMP_BLOCK
    zeros4 = jnp.zeros((NSA_GROUP, HEAD_DIM), F32)
    gt = gt_ref[0]
    ocmp = ocmp_ref[0]
    y_pieces = []
    for g in range(NSA_KV_HEADS):
        gs = slice(g * HEAD_DIM, (g + 1) * HEAD_DIM)
        want = jnp.concatenate(
            [jnp.full((1, PAGE_SIZE), idx_ref[(b * NSA_KV_HEADS + g) * N_SELECT + s] % 2, I32)
             for s in range(N_PICK)], axis=1)
        qg = jnp.concatenate(pieces[g * NSA_GROUP:(g + 1) * NSA_GROUP] + [zeros4], axis=0)
        s_sel = jnp.where(half_of_lane == want, _dot(qg.astype(BF16), kbuf[slot, g].astype(BF16)), NEG)
        o_sel = attend(qg, s_sel, vbuf[slot, g].astype(BF16), knew_ref[0][:, gs], vnew_ref[0][:, gs])
        for r in range(NSA_GROUP):
            hh = g * NSA_GROUP + r
            y_pieces.append(gt[:, N_BRANCH * hh:N_BRANCH * hh + 1] * ocmp[:, hh * HEAD_DIM:(hh + 1) * HEAD_DIM]
                            + gt[:, N_BRANCH * hh + 1:N_BRANCH * hh + 2] * o_sel[r:r + 1]
                            + gt[:, N_BRANCH * hh + 2:N_BRANCH * hh + 3] * o_win[hh:hh + 1, gs])
    y_ref[0] = jnp.concatenate(y_pieces, axis=1)


def _att_sample(idx, page_table, nq, sk, sv, wk_new, wv_new, gates, ocmp, pool_k, pool_v, buf_k, buf_v):
    DB, n_pages = page_table.shape
    past_len = n_pages * PAGE_SIZE
    wb = buf_k.shape[2]
    r3 = lambda t: t.reshape(DB, 1, t.shape[-1])
    row = lambda w: pl.BlockSpec((1, 1, w), lambda b, ix, pt: (b, 0, 0))
    any_spec = pl.BlockSpec(memory_space=pl.ANY)
    wspec = pl.BlockSpec((1, KV_WIDTH, wb), lambda b, ix, pt: (b, 0, 0))
    buf_shape = (2, NSA_KV_HEADS, HEAD_DIM, N_PICK * PAGE_SIZE)
    grid_spec = pltpu.PrefetchScalarGridSpec(
        num_scalar_prefetch=2,
        grid=(DB,),
        in_specs=[row(NSA_WIDTH), row(KV_WIDTH), row(KV_WIDTH), row(KV_WIDTH), row(KV_WIDTH),
                  row(GATE_PAD), row(NSA_WIDTH), any_spec, any_spec, wspec, wspec],
        out_specs=[row(NSA_WIDTH), wspec, wspec],
        scratch_shapes=[pltpu.VMEM(buf_shape, F32), pltpu.VMEM(buf_shape, F32),
                        pltpu.SemaphoreType.DMA((2,)), pltpu.SemaphoreType.DMA((2,))],
    )
    return pl.pallas_call(
        functools.partial(_att_sample_kernel, past_len=past_len),
        grid_spec=grid_spec,
        out_shape=[jax.ShapeDtypeStruct((DB, 1, NSA_WIDTH), F32),
                   jax.ShapeDtypeStruct(buf_k.shape, F32),
                   jax.ShapeDtypeStruct(buf_v.shape, F32)],
        compiler_params=_params("arbitrary"),
        name="att_sample",
    )(idx, page_table, r3(nq), r3(sk), r3(sv), r3(wk_new), r3(wv_new), r3(gates), ocmp,
      pool_k, pool_v, buf_k, buf_v)


def _row_tile(rows):
    return 256 if rows % 256 == 0 else 128


def kernel(x_prompt, x_sample, state_ret, cache_cmp_k, cache_cmp_v, cache_sel_k, cache_sel_v,
           cache_win_k, cache_win_v, page_table, ln_pre_mix, w_in, ret_gn_w,
           cmp_pos_k, cmp_w1_k, cmp_w2_k, cmp_pos_v, cmp_w1_v, cmp_w2_v,
           w_o, ln_post_mix, ln_pre_ffn, w_up, w_down, ln_post_ffn):
    B, T, _ = x_prompt.shape
    DB, Tn, _ = x_sample.shape
    n_pages = page_table.shape[1]
    past_len = n_pages * PAGE_SIZE
    wb = cache_win_k.shape[2]
    assert w_in.shape[0] == 1 and Tn == 1
    assert T % ATT_K == 0 and T >= WINDOW + ATT_Q and T // CMP_BLOCK <= MAX_BLOCKS
    assert 2 * n_pages >= N_PICK and DB % CMP_SAMPLES == 0

    w_main = w_in[0, :, :MAIN_WIDTH].astype(BF16)
    w_gate = jnp.pad(w_in[0, :, MAIN_WIDTH:], ((0, 0), (0, GATE_PAD - NSA_HEADS * N_BRANCH))).astype(BF16)
    tail = (w_o[0].astype(BF16), ln_post_mix, ln_pre_ffn, ln_post_ffn, w_up[0].astype(BF16), w_down[0].astype(BF16))

    tile_p = _row_tile(B * T)
    cos_p, sin_p = _rope_tables(jnp.arange(T, dtype=I32))
    xp = x_prompt.reshape(B * T, D_MODEL)
    (rq, rk, rv, rg, nqT, ck, cv, sk, wk, ckT, cvT, skT, svT, wkT, wvT, svTb, wvTb, gtT) = _inproj_prompt(
        xp, ln_pre_mix, w_main, w_gate, cos_p, sin_p, B, T, tile_p)
    b3 = lambda t: t.reshape(B, T, t.shape[-1])
    ret_y, rs_p = _ret_prompt(b3(rq), b3(rk), b3(rv), b3(rg), ret_gn_w)
    cwk = _compress_weights(cmp_pos_k[0], cmp_w1_k[0], cmp_w2_k[0], True)
    cwv = _compress_weights(cmp_pos_v[0], cmp_w1_v[0], cmp_w2_v[0], True)
    ocmpT, biasT = _cmp_prompt(b3(ck), b3(cv), cwk, cwv, nqT, gtT, _row_tile(T))
    nsa_y = _att_prompt(nqT, biasT, b3(sk), svTb, b3(wk), wvTb, gtT, ocmpT)
    y_p = _finish(xp, ret_y.reshape(B * T, RET_WIDTH), nsa_y.reshape(B * T, NSA_WIDTH), *tail, tile_p)
    win_p = min(WINDOW, T)
    kv5 = lambda t: t.reshape(t.shape[0], NSA_KV_HEADS, HEAD_DIM, t.shape[-1]).transpose(0, 3, 1, 2)[None]
    outs_p = (y_p.reshape(B, T, D_MODEL), rs_p[None],
              kv5(ckT), kv5(cvT), kv5(skT), kv5(svT),
              kv5(wkT[:, :, T - win_p:]), kv5(wvT[:, :, T - win_p:]))

    cos_s, sin_s = _rope_tables(jnp.full((DB,), past_len, I32))
    xs = x_sample.reshape(DB, D_MODEL)
    (rqT, rkT, rvT, rgT, nq, ckT, cvT, skT, svT, sk, sv, wk, wv, gt) = _inproj_sample(
        xs, ln_pre_mix, w_main, w_gate, cos_s, sin_s)
    ret_yT, rs_s = _ret_sample(rqT, rkT, rvT, rgT, ret_gn_w, state_ret[0].transpose(1, 2, 3, 0))
    feat_major = lambda t: t[0].transpose(0, 2, 3, 1).reshape(t.shape[1], KV_WIDTH, t.shape[2])
    cwk = _compress_weights(cmp_pos_k[0], cmp_w1_k[0], cmp_w2_k[0], False)
    cwv = _compress_weights(cmp_pos_v[0], cmp_w1_v[0], cmp_w2_v[0], False)
    ocmp, idx = _cmp_sample(page_table, nq, feat_major(cache_cmp_k), feat_major(cache_cmp_v), cwk, cwv)
    idx_flat = idx[:, :, :N_SELECT].reshape(-1)
    y_row, nwk, nwv = _att_sample(idx_flat, page_table, nq, sk, sv, wk, wv, gt, ocmp,
                                  feat_major(cache_sel_k), feat_major(cache_sel_v),
                                  feat_major(cache_win_k), feat_major(cache_win_v))
    y_s = _finish(xs, ret_yT, y_row.reshape(DB, NSA_WIDTH), *tail, DB, ret_t=True)
    new5 = lambda t: t.reshape(NSA_KV_HEADS, HEAD_DIM, DB).transpose(2, 0, 1).reshape(1, DB, 1, NSA_KV_HEADS, HEAD_DIM)
    outs_s = (y_s.reshape(DB, 1, D_MODEL), rs_s.transpose(3, 0, 1, 2)[None],
              new5(ckT), new5(cvT), new5(skT), new5(svT), kv5(nwk), kv5(nwv))

    return (outs_p[0], outs_s[0]) + outs_p[1:] + outs_s[1:]
```

```python
import functools

import numpy as np
import jax
import jax.numpy as jnp
from jax import lax
from jax.experimental import pallas as pl
from jax.experimental.pallas import tpu as pltpu

F32 = jnp.float32
BF16 = jnp.bfloat16
I32 = jnp.int32

D_MODEL = 1024
HEAD_DIM = 64
RET_WIDTH = 512
RET_HEADS = 8
NSA_WIDTH = 512
NSA_HEADS = 8
NSA_KV_HEADS = 2
NSA_GROUP = 4
KV_WIDTH = 128
N_BRANCH = 3
D_FF = 4 * D_MODEL
CMP_BLOCK = 64
N_SELECT = 16
N_PICK = N_SELECT - 1
WINDOW = 512
PAGE_SIZE = 128
RET_CHUNK = 128
ROPE_BASE = 10000.0
EPS = 1e-6
NEG = -1e30
SCALE = HEAD_DIM ** -0.5

MAIN_WIDTH = 4 * RET_WIDTH + NSA_WIDTH + 6 * KV_WIDTH
GATE_PAD = 128
ATT_Q = 128
ATT_K = 256
MAX_BLOCKS = 64
CMP_SAMPLES = 2
VMEM_LIMIT = 56 * 1024 * 1024

_LOG_GAMMA = [float(np.log1p(-np.exp2(-5.0 - h))) for h in range(RET_HEADS)]


def _params(*sem):
    return pltpu.CompilerParams(dimension_semantics=sem, vmem_limit_bytes=VMEM_LIMIT)


def _dot(a, b):
    return jnp.dot(a, b, preferred_element_type=F32)


def _dot_nt(a, b):
    return lax.dot_general(a, b, (((1,), (1,)), ((), ())), preferred_element_type=F32)


def _dot_tn(a, b):
    return lax.dot_general(a, b, (((0,), (0,)), ((), ())), preferred_element_type=F32)


def _rms(x, w):
    return x * lax.rsqrt(jnp.mean(x * x, axis=-1, keepdims=True) + EPS) * w


def _projector(x_ref, lnw_ref, w_ref, cos_ref, sin_ref):
    h = _rms(x_ref[...], lnw_ref[...]).astype(BF16)
    rows = h.shape[0]

    def proj(a, b):
        return _dot(h, w_ref[:, a:b])

    cos = cos_ref[...]
    sin = sin_ref[...]
    lane = lax.broadcasted_iota(I32, (rows, RET_WIDTH), 1)
    first_half = (lane % HEAD_DIM) < (HEAD_DIM // 2)

    def rope(y):
        partner = jnp.where(first_half,
                            pltpu.roll(y, RET_WIDTH - HEAD_DIM // 2, 1),
                            pltpu.roll(y, HEAD_DIM // 2, 1))
        return y * cos + partner * sin

    return h, proj, rope


def _rope_tables(pos):
    half = HEAD_DIM // 2
    inv = ROPE_BASE ** (-jnp.arange(half, dtype=F32) / half)
    ang = pos.astype(F32)[:, None] * inv
    cos, sin = jnp.cos(ang), jnp.sin(ang)
    cos_h = jnp.concatenate([cos, cos], -1)
    sin_h = jnp.concatenate([-sin, sin], -1)
    return jnp.tile(cos_h, (1, RET_HEADS)), jnp.tile(sin_h, (1, RET_HEADS))


def _inproj_prompt_kernel(x_ref, lnw_ref, w_ref, wg_ref, cos_ref, sin_ref,
                          rq_ref, rk_ref, rv_ref, rg_ref, nqT_ref, ck_ref, cv_ref, sk_ref, wk_ref,
                          ckT_ref, cvT_ref, skT_ref, svT_ref, wkT_ref, wvT_ref, svTb_ref, wvTb_ref, gtT_ref):
    h, proj, rope = _projector(x_ref, lnw_ref, w_ref, cos_ref, sin_ref)
    rq_ref[...] = rope(proj(0, 512)).astype(BF16)
    rk_ref[...] = (rope(proj(512, 1024)) * SCALE).astype(BF16)
    rv_ref[...] = proj(1024, 1536).astype(BF16)
    rg_ref[...] = proj(1536, 2048)
    for n in range(NSA_WIDTH // KV_WIDTH):
        nq = proj(2048 + n * KV_WIDTH, 2048 + (n + 1) * KV_WIDTH) * SCALE
        nqT_ref[0, n * KV_WIDTH:(n + 1) * KV_WIDTH, :] = nq.T.astype(BF16)
    kv = [proj(2560 + n * KV_WIDTH, 2560 + (n + 1) * KV_WIDTH) for n in range(6)]
    ck_ref[...] = kv[0]
    cv_ref[...] = kv[1]
    sk_ref[...] = kv[2].astype(BF16)
    wk_ref[...] = kv[4].astype(BF16)
    for n, ref in enumerate((ckT_ref, cvT_ref, skT_ref, svT_ref, wkT_ref, wvT_ref)):
        ref[0] = kv[n].T
    svTb_ref[0] = kv[3].T.astype(BF16)
    wvTb_ref[0] = kv[5].T.astype(BF16)
    gtT_ref[0] = jax.nn.sigmoid(_dot(h, wg_ref[...])).T


def _inproj_prompt(x2d, lnw, w_main, w_gate, cos_t, sin_t, B, T, tile):
    rows = x2d.shape[0]
    nt = T // tile
    row = lambda i: (i, 0)
    const = lambda i: (0, 0)
    tab = lambda i: (i % nt, 0)
    colT = lambda i: (i // nt, 0, i % nt)
    rm = lambda w, dt: (pl.BlockSpec((tile, w), row), jax.ShapeDtypeStruct((rows, w), dt))
    tr = lambda w, dt: (pl.BlockSpec((1, w, tile), colT), jax.ShapeDtypeStruct((B, w, T), dt))
    outs = [rm(RET_WIDTH, BF16), rm(RET_WIDTH, BF16), rm(RET_WIDTH, BF16), rm(RET_WIDTH, F32),
            tr(NSA_WIDTH, BF16), rm(KV_WIDTH, F32), rm(KV_WIDTH, F32), rm(KV_WIDTH, BF16), rm(KV_WIDTH, BF16),
            *[tr(KV_WIDTH, F32)] * 6,
            tr(KV_WIDTH, BF16), tr(KV_WIDTH, BF16), tr(GATE_PAD, F32)]
    return pl.pallas_call(
        _inproj_prompt_kernel,
        grid=(rows // tile,),
        in_specs=[pl.BlockSpec((tile, D_MODEL), row),
                  pl.BlockSpec((1, D_MODEL), const),
                  pl.BlockSpec((D_MODEL, MAIN_WIDTH), const),
                  pl.BlockSpec((D_MODEL, GATE_PAD), const),
                  pl.BlockSpec((tile, RET_WIDTH), tab),
                  pl.BlockSpec((tile, RET_WIDTH), tab)],
        out_specs=[o[0] for o in outs],
        out_shape=[o[1] for o in outs],
        compiler_params=_params("parallel"),
        name="inproj_prompt",
    )(x2d, lnw, w_main, w_gate, cos_t, sin_t)


def _inproj_sample_kernel(x_ref, lnw_ref, w_ref, wg_ref, cos_ref, sin_ref,
                          rqT_ref, rkT_ref, rvT_ref, rgT_ref, nq_ref,
                          ckT_ref, cvT_ref, skT_ref, svT_ref, sk_ref, sv_ref, wk_ref, wv_ref, gt_ref):
    h, proj, rope = _projector(x_ref, lnw_ref, w_ref, cos_ref, sin_ref)
    rqT_ref[...] = rope(proj(0, 512)).T
    rkT_ref[...] = (rope(proj(512, 1024)) * SCALE).T
    rvT_ref[...] = proj(1024, 1536).T
    rgT_ref[...] = proj(1536, 2048).T
    nq_ref[...] = proj(2048, 2560) * SCALE
    kv = [proj(2560 + n * KV_WIDTH, 2560 + (n + 1) * KV_WIDTH) for n in range(6)]
    for n, ref in enumerate((ckT_ref, cvT_ref, skT_ref, svT_ref)):
        ref[...] = kv[n].T
    for n, ref in enumerate((sk_ref, sv_ref, wk_ref, wv_ref)):
        ref[...] = kv[2 + n]
    gt_ref[...] = jax.nn.sigmoid(_dot(h, wg_ref[...]))


def _inproj_sample(x2d, lnw, w_main, w_gate, cos_t, sin_t):
    rows = x2d.shape[0]
    full = lambda a, b: pl.BlockSpec((a, b), lambda i: (0, 0))
    rm = lambda w: (full(rows, w), jax.ShapeDtypeStruct((rows, w), F32))
    tr = lambda w: (full(w, rows), jax.ShapeDtypeStruct((w, rows), F32))
    outs = [tr(RET_WIDTH)] * 4 + [rm(NSA_WIDTH)] + [tr(KV_WIDTH)] * 4 + [rm(KV_WIDTH)] * 4 + [rm(GATE_PAD)]
    return pl.pallas_call(
        _inproj_sample_kernel,
        grid=(1,),
        in_specs=[full(rows, D_MODEL), full(1, D_MODEL), full(D_MODEL, MAIN_WIDTH), full(D_MODEL, GATE_PAD),
                  full(rows, RET_WIDTH), full(rows, RET_WIDTH)],
        out_specs=[o[0] for o in outs],
        out_shape=[o[1] for o in outs],
        compiler_params=_params("arbitrary"),
        name="inproj_sample",
    )(x2d, lnw, w_main, w_gate, cos_t, sin_t)


def _group_norm_gate(outs, g, gnw):
    ys = []
    for o in outs:
        mu = jnp.mean(o, axis=-1, keepdims=True)
        d = o - mu
        var = jnp.mean(d * d, axis=-1, keepdims=True)
        ys.append(d * lax.rsqrt(var + EPS))
    y = jnp.concatenate(ys, axis=1) * gnw
    return g * jax.nn.sigmoid(g) * y


def _ret_prompt_kernel(q_ref, k_ref, v_ref, g_ref, gnw_ref, inner_ref, qdec_ref, kdec_ref,
                       y_ref, s_out_ref, s_scr):
    c = pl.program_id(1)

    @pl.when(c == 0)
    def _():
        s_scr[...] = jnp.zeros_like(s_scr)

    q = q_ref[0]
    k = k_ref[0]
    v = v_ref[0]
    kd = (k.astype(F32) * kdec_ref[...]).astype(BF16)
    qdec = qdec_ref[...]
    outs = []
    for h in range(RET_HEADS):
        sl = slice(h * HEAD_DIM, (h + 1) * HEAD_DIM)
        qh, kh, vh = q[:, sl], k[:, sl], v[:, sl]
        a = _dot_nt(qh, kh) * inner_ref[h]
        s = s_scr[h]
        o = _dot(a.astype(BF16), vh) + _dot(qh, s.astype(BF16)) * qdec[:, sl]
        s_scr[h] = s * float(np.exp(_LOG_GAMMA[h] * RET_CHUNK)) + _dot_tn(kd[:, sl], vh)
        outs.append(o)
    y_ref[0] = _group_norm_gate(outs, g_ref[0], gnw_ref[...]).astype(BF16)

    @pl.when(c == pl.num_programs(1) - 1)
    def _():
        s_out_ref[0] = s_scr[...]


def _ret_prompt(rq, rk, rv, rg, gnw):
    B, T, _ = rq.shape
    C = RET_CHUNK
    lg = jnp.asarray(_LOG_GAMMA, F32)
    i = jnp.arange(C, dtype=F32)
    diff = i[:, None] - i[None, :]
    inner = jnp.where(diff >= 0, jnp.exp(lg[:, None, None] * jnp.maximum(diff, 0.0)), 0.0)
    qdec = jnp.repeat(jnp.exp(lg[None, :] * (i[:, None] + 1.0)), HEAD_DIM, axis=1)
    kdec = jnp.repeat(jnp.exp(lg[None, :] * (C - 1.0 - i[:, None])), HEAD_DIM, axis=1)
    blk = pl.BlockSpec((1, C, RET_WIDTH), lambda b, c: (b, c, 0))
    c2 = lambda b, c: (0, 0)
    return pl.pallas_call(
        _ret_prompt_kernel,
        grid=(B, T // C),
        in_specs=[blk, blk, blk, blk,
                  pl.BlockSpec((1, RET_WIDTH), c2),
                  pl.BlockSpec((RET_HEADS, C, C), lambda b, c: (0, 0, 0)),
                  pl.BlockSpec((C, RET_WIDTH), c2),
                  pl.BlockSpec((C, RET_WIDTH), c2)],
        out_specs=[blk, pl.BlockSpec((1, RET_HEADS, HEAD_DIM, HEAD_DIM), lambda b, c: (b, 0, 0, 0))],
        out_shape=[jax.ShapeDtypeStruct((B, T, RET_WIDTH), BF16),
                   jax.ShapeDtypeStruct((B, RET_HEADS, HEAD_DIM, HEAD_DIM), F32)],
        scratch_shapes=[pltpu.VMEM((RET_HEADS, HEAD_DIM, HEAD_DIM), F32)],
        compiler_params=_params("parallel", "arbitrary"),
        name="ret_prompt",
    )(rq, rk, rv, rg, gnw, inner, qdec, kdec)


def _ret_sample_kernel(q_ref, k_ref, v_ref, g_ref, gnw_ref, gam_ref, s_ref, y_ref, s_out_ref):
    q = q_ref[...]
    k = k_ref[...]
    v = v_ref[...]
    g = g_ref[...]
    gamma = gam_ref[0]
    acc = jnp.zeros(v.shape, F32)
    for d in range(HEAD_DIM):
        s_d = s_ref[0, d]
        acc = acc + s_d * q[d:d + 1, :]
        s_out_ref[0, d] = s_d * gamma + k[d:d + 1, :] * v
    o = acc * gamma + jnp.sum(q * k, axis=0, keepdims=True) * v
    mu = jnp.mean(o, axis=0, keepdims=True)
    dlt = o - mu
    var = jnp.mean(dlt * dlt, axis=0, keepdims=True)
    y_ref[...] = g * jax.nn.sigmoid(g) * (dlt * lax.rsqrt(var + EPS) * gnw_ref[...])


def _ret_sample(rqT, rkT, rvT, rgT, gnw, state_t):
    DB = rqT.shape[1]
    blk = pl.BlockSpec((HEAD_DIM, DB), lambda h: (h, 0))
    st = pl.BlockSpec((1, HEAD_DIM, HEAD_DIM, DB), lambda h: (h, 0, 0, 0))
    gam = jnp.broadcast_to(jnp.exp(jnp.asarray(_LOG_GAMMA, F32))[:, None, None], (RET_HEADS, 1, DB))
    return pl.pallas_call(
        _ret_sample_kernel,
        grid=(RET_HEADS,),
        in_specs=[blk, blk, blk, blk,
                  pl.BlockSpec((HEAD_DIM, 1), lambda h: (h, 0)),
                  pl.BlockSpec((1, 1, DB), lambda h: (h, 0, 0)),
                  st],
        out_specs=[blk, st],
        out_shape=[jax.ShapeDtypeStruct((RET_WIDTH, DB), F32),
                   jax.ShapeDtypeStruct(state_t.shape, F32)],
        compiler_params=_params("parallel"),
        name="ret_sample",
    )(rqT, rkT, rvT, rgT, gnw.reshape(RET_WIDTH, 1), gam, state_t)


def _compress(load_rows, pos_ref, w1_ref, w2_ref):
    acc = None
    for j2 in range(CMP_BLOCK // 2):
        x = jnp.concatenate([load_rows(2 * j2) + pos_ref[2 * j2:2 * j2 + 1, :],
                             load_rows(2 * j2 + 1) + pos_ref[2 * j2 + 1:2 * j2 + 2, :]], axis=1).astype(BF16)
        part = _dot(x, w1_ref[j2])
        acc = part if acc is None else acc + part
    h = jax.nn.gelu(acc)
    return _dot(h.astype(BF16), w2_ref[...])


def _block_diag2(w):
    z = jnp.zeros_like(w)
    return jnp.concatenate([jnp.concatenate([w, z], -1), jnp.concatenate([z, w], -1)], -2)


def _compress_weights(pos, w1, w2, token_major):
    if not token_major:
        pos, w1 = pos.T, w1.transpose(1, 0, 2)
    pos2 = jnp.concatenate([pos, pos], -1)
    w1p = _block_diag2(w1).reshape(CMP_BLOCK // 2, 2 * KV_WIDTH, KV_WIDTH)
    return pos2, w1p.astype(BF16), _block_diag2(w2).astype(BF16)


def _rank_rows(imp):
    nb, tq = imp.shape
    sub = 8
    groups = [imp[sub * j:sub * (j + 1)] for j in range(nb // sub)]
    ranks = [jnp.zeros((sub, tq), I32) for _ in groups]
    row_in_group = lax.broadcasted_iota(I32, (sub, tq), 0)
    for m in range(nb):
        row = imp[m:m + 1, :]
        jm, sm = divmod(m, sub)
        for j, grp in enumerate(groups):
            if j < jm:
                inc = jnp.where(row > grp, 1, 0)
            elif j > jm:
                inc = jnp.where(row >= grp, 1, 0)
            else:
                inc = jnp.where(row_in_group > sm, jnp.where(row >= grp, 1, 0), jnp.where(row > grp, 1, 0))
            ranks[j] = ranks[j] + inc
    return jnp.concatenate(ranks, axis=0)


def _kv_head_rows(x, g):
    z = jnp.zeros_like(x)
    return jnp.concatenate([x, z] if g == 0 else [z, x], axis=0)


def _cmp_prompt_kernel(ck_ref, cv_ref, posk_ref, w1k_ref, w2k_ref, posv_ref, w1v_ref, w2v_ref,
                       nqT_ref, gtT_ref, ocmpT_ref, biasT_ref, kc_scr, vc_scr):
    i = pl.program_id(1)
    nb = kc_scr.shape[0]
    tq = nqT_ref.shape[2]

    @pl.when(i == 0)
    def _():
        kc_scr[...] = _compress(lambda l: ck_ref[0, pl.ds(l, nb, stride=CMP_BLOCK), :],
                                posk_ref, w1k_ref, w2k_ref).astype(BF16)
        vc_scr[...] = _compress(lambda l: cv_ref[0, pl.ds(l, nb, stride=CMP_BLOCK), :],
                                posv_ref, w1v_ref, w2v_ref).astype(BF16)

    kc = kc_scr[...]
    vc = vc_scr[...]
    pos = i * tq + lax.broadcasted_iota(I32, (nb, tq), 1)
    n_idx = lax.broadcasted_iota(I32, (nb, tq), 0)
    valid = (n_idx + 1) * CMP_BLOCK - 1 <= pos
    cur = pos // CMP_BLOCK
    cand = n_idx < cur
    for g in range(NSA_KV_HEADS):
        imp = jnp.zeros((nb, tq), F32)
        for r in range(NSA_GROUP):
            hh = g * NSA_GROUP + r
            qT = _kv_head_rows(nqT_ref[0, hh * HEAD_DIM:(hh + 1) * HEAD_DIM, :], g)
            s = jnp.where(valid, _dot(kc, qT), NEG)
            e = jnp.exp(s - jnp.max(s, axis=0, keepdims=True))
            p = jnp.where(valid, e / jnp.sum(e, axis=0, keepdims=True), 0.0)
            imp = imp + p
            oT = _dot_tn(vc, p.astype(BF16))
            ocmpT_ref[0, hh * HEAD_DIM:(hh + 1) * HEAD_DIM, :] = (
                oT[g * HEAD_DIM:(g + 1) * HEAD_DIM] * gtT_ref[0, N_BRANCH * hh:N_BRANCH * hh + 1, :])
        imp = jnp.where(cand, imp, -1.0)
        rank = _rank_rows(imp)
        sel = (cand & (rank < N_PICK)) | (n_idx == cur)
        biasT_ref[0, g * MAX_BLOCKS:g * MAX_BLOCKS + nb, :] = jnp.where(sel, 0.0, NEG).astype(BF16)
        if nb < MAX_BLOCKS:
            biasT_ref[0, g * MAX_BLOCKS + nb:(g + 1) * MAX_BLOCKS, :] = jnp.zeros((MAX_BLOCKS - nb, tq), BF16)


def _cmp_weight_specs(c2, c3):
    return [pl.BlockSpec((CMP_BLOCK, KV_WIDTH), c2),
            pl.BlockSpec((CMP_BLOCK // 2, 2 * KV_WIDTH, KV_WIDTH), c3),
            pl.BlockSpec((KV_WIDTH, KV_WIDTH), c2)]


def _cmp_prompt(ck, cv, cwk, cwv, nqT, gtT, tq):
    B, T, _ = ck.shape
    nb = T // CMP_BLOCK
    full = pl.BlockSpec((1, T, KV_WIDTH), lambda b, i: (b, 0, 0))
    wspecs = _cmp_weight_specs(lambda b, i: (0, 0), lambda b, i: (0, 0, 0))
    blk = lambda w: pl.BlockSpec((1, w, tq), lambda b, i: (b, 0, i))
    return pl.pallas_call(
        _cmp_prompt_kernel,
        grid=(B, T // tq),
        in_specs=[full, full] + wspecs + wspecs + [blk(NSA_WIDTH), blk(GATE_PAD)],
        out_specs=[blk(NSA_WIDTH), blk(NSA_KV_HEADS * MAX_BLOCKS)],
        out_shape=[jax.ShapeDtypeStruct((B, NSA_WIDTH, T), F32),
                   jax.ShapeDtypeStruct((B, NSA_KV_HEADS * MAX_BLOCKS, T), BF16)],
        scratch_shapes=[pltpu.VMEM((nb, KV_WIDTH), BF16), pltpu.VMEM((nb, KV_WIDTH), BF16)],
        compiler_params=_params("parallel", "arbitrary"),
        name="cmp_prompt",
    )(ck, cv, *cwk, *cwv, nqT, gtT)


def _att_prompt_kernel(nqT_ref, biasT_ref, sk_ref, svT_ref, wk_ref, wvT_ref, gtT_ref, ocmpT_ref,
                       y_ref, ka_scr, va_scr):
    i = pl.program_id(1)
    tq = nqT_ref.shape[2]
    T = sk_ref.shape[1]
    cols = NSA_GROUP * tq

    @pl.when(i == 0)
    def _():
        sk = sk_ref[0].astype(F32)
        lane = lax.broadcasted_iota(I32, (T, KV_WIDTH), 1)
        blk = lax.broadcasted_iota(I32, (T, KV_WIDTH), 0) // CMP_BLOCK
        ka_scr[0] = jnp.where(lane < HEAD_DIM, sk, jnp.where(lane - HEAD_DIM == blk, 1.0, 0.0)).astype(BF16)
        ka_scr[1] = jnp.where(lane >= HEAD_DIM, sk, jnp.where(lane == blk, 1.0, 0.0)).astype(BF16)
        svT = svT_ref[0].astype(F32)
        row = lax.broadcasted_iota(I32, (KV_WIDTH, T), 0)
        va_scr[0] = jnp.where(row < HEAD_DIM, svT, 1.0).astype(BF16)
        va_scr[1] = jnp.where(row >= HEAD_DIM, svT, 1.0).astype(BF16)

    q_pos = i * tq + lax.broadcasted_iota(I32, (ATT_K, cols), 1) % tq
    k_off = lax.broadcasted_iota(I32, (ATT_K, cols), 0)
    n_win = WINDOW // tq
    win_keys = (n_win + 1) * tq
    win_start = pl.multiple_of(jnp.maximum(i - n_win, 0) * tq, tq)
    rel = i * tq - win_start + lax.broadcasted_iota(I32, (win_keys, tq), 1)
    j_win = lax.broadcasted_iota(I32, (win_keys, tq), 0)
    win_bias = jnp.where((j_win <= rel) & (j_win > rel - WINDOW), 0.0, NEG)
    win_bias = jnp.concatenate([win_bias] * NSA_GROUP, axis=1)
    wk = wk_ref[0, pl.ds(win_start, win_keys), :]
    wvT = wvT_ref[0, :, pl.ds(win_start, win_keys)]

    qaTs, qwTs = [], []
    for g in range(NSA_KV_HEADS):
        qTs = [nqT_ref[0, (g * NSA_GROUP + r) * HEAD_DIM:(g * NSA_GROUP + r + 1) * HEAD_DIM, :]
               for r in range(NSA_GROUP)]
        bT = biasT_ref[0, g * MAX_BLOCKS:(g + 1) * MAX_BLOCKS, :]
        qaTs.append(jnp.concatenate([jnp.concatenate([q, bT] if g == 0 else [bT, q], axis=0) for q in qTs],
                                    axis=1))
        qwTs.append(jnp.concatenate([_kv_head_rows(q, g) for q in qTs], axis=1))

    def scores(c):
        start = pl.multiple_of(c * ATT_K, ATT_K)
        return tuple(_dot(ka_scr[g, pl.ds(start, ATT_K), :], qaTs[g]) for g in range(NSA_KV_HEADS))

    def update(c, carries, ss):
        start = pl.multiple_of(c * ATT_K, ATT_K)
        out = []
        for g, ((m, acc), s) in enumerate(zip(carries, ss)):
            m_new = jnp.maximum(m, jnp.max(s, axis=0, keepdims=True))
            alpha = jnp.exp(m - m_new)
            p = jnp.exp(s - m_new)
            acc = alpha * acc + _dot(va_scr[g, :, pl.ds(start, ATT_K)], p.astype(BF16))
            out.append((m_new, acc))
        return tuple(out)

    def sel_step(c, state):
        carries, ss = state
        ss_next = scores(c + 1)
        return update(c, carries, ss), ss_next

    init = (jnp.full((1, cols), NEG, F32), jnp.zeros((KV_WIDTH, cols), F32))
    n_before = (i * tq) // ATT_K
    carries, ss = lax.fori_loop(0, n_before, sel_step, ((init, init), scores(0)))
    last_start = n_before * ATT_K
    ss = tuple(jnp.where(last_start + k_off > q_pos, NEG, s) for s in ss)
    carries = update(n_before, carries, ss)

    heads_out = []
    for g in range(NSA_KV_HEADS):
        gs = slice(g * HEAD_DIM, (g + 1) * HEAD_DIM)
        _, acc = carries[g]
        ones_row = (1 - g) * HEAD_DIM
        o_sel = acc[gs] / acc[ones_row:ones_row + 1]

        s = _dot(wk, qwTs[g]) + win_bias
        p = jnp.exp(s - jnp.max(s, axis=0, keepdims=True))
        o_win = _dot(wvT, p.astype(BF16))[gs] / jnp.sum(p, axis=0, keepdims=True)

        for r in range(NSA_GROUP):
            hh = g * NSA_GROUP + r
            cs = slice(r * tq, (r + 1) * tq)
            heads_out.append(ocmpT_ref[0, hh * HEAD_DIM:(hh + 1) * HEAD_DIM, :]
                             + gtT_ref[0, N_BRANCH * hh + 1:N_BRANCH * hh + 2, :] * o_sel[:, cs]
                             + gtT_ref[0, N_BRANCH * hh + 2:N_BRANCH * hh + 3, :] * o_win[:, cs])
    y_ref[0] = jnp.concatenate(heads_out, axis=0).T.astype(BF16)


def _att_prompt(nqT, biasT, sk, svT, wk, wvT, gtT, ocmpT):
    B, _, T = nqT.shape
    tq = ATT_Q
    rows_full = pl.BlockSpec((1, T, KV_WIDTH), lambda b, i: (b, 0, 0))
    cols_full = pl.BlockSpec((1, KV_WIDTH, T), lambda b, i: (b, 0, 0))
    blk = lambda w: pl.BlockSpec((1, w, tq), lambda b, i: (b, 0, i))
    return pl.pallas_call(
        _att_prompt_kernel,
        grid=(B, T // tq),
        in_specs=[blk(NSA_WIDTH), blk(NSA_KV_HEADS * MAX_BLOCKS), rows_full, cols_full, rows_full, cols_full,
                  blk(GATE_PAD), blk(NSA_WIDTH)],
        out_specs=pl.BlockSpec((1, tq, NSA_WIDTH), lambda b, i: (b, i, 0)),
        out_shape=jax.ShapeDtypeStruct((B, T, NSA_WIDTH), BF16),
        scratch_shapes=[pltpu.VMEM((NSA_KV_HEADS, T, KV_WIDTH), BF16),
                        pltpu.VMEM((NSA_KV_HEADS, KV_WIDTH, T), BF16)],
        compiler_params=_params("parallel", "arbitrary"),
        name="att_prompt",
    )(nqT, biasT, sk, svT, wk, wvT, gtT, ocmpT)


def _finish_kernel(x_ref, ry_ref, ny_ref, wo_ref, l1_ref, l2_ref, l3_ref, wu_ref, wd_ref, o_ref, *, ret_t):
    if ret_t:
        mix = _dot_tn(ry_ref[...].astype(BF16), wo_ref[0:RET_WIDTH, :])
    else:
        mix = _dot(ry_ref[...], wo_ref[0:RET_WIDTH, :])
    mix = mix + _dot(ny_ref[...].astype(BF16), wo_ref[RET_WIDTH:, :])
    h = x_ref[...] + _rms(mix, l1_ref[...])
    u = _dot(_rms(h, l2_ref[...]).astype(BF16), wu_ref[...])
    a = jnp.square(jnp.maximum(u, 0.0)).astype(BF16)
    f = _dot(a, wd_ref[...])
    o_ref[...] = h + _rms(f, l3_ref[...])


def _finish(x2d, ret_y, nsa_y, w_o, l1, l2, l3, w_up, w_down, tile, ret_t=False):
    rows = x2d.shape[0]
    row = lambda w: pl.BlockSpec((tile, w), lambda i: (i, 0))
    const = lambda a, b: pl.BlockSpec((a, b), lambda i: (0, 0))
    ret_spec = pl.BlockSpec((RET_WIDTH, tile), lambda i: (0, i)) if ret_t else row(RET_WIDTH)
    return pl.pallas_call(
        functools.partial(_finish_kernel, ret_t=ret_t),
        grid=(rows // tile,),
        in_specs=[row(D_MODEL), ret_spec, row(NSA_WIDTH),
                  const(D_MODEL, D_MODEL), const(1, D_MODEL), const(1, D_MODEL), const(1, D_MODEL),
                  const(D_MODEL, D_FF), const(D_FF, D_MODEL)],
        out_specs=row(D_MODEL),
        out_shape=jax.ShapeDtypeStruct((rows, D_MODEL), F32),
        compiler_params=_params("parallel"),
        name="finish",
    )(x2d, ret_y, nsa_y, w_o, l1, l2, l3, w_up, w_down)


def _column(row):
    n = row.shape[1]
    eye = lax.broadcasted_iota(I32, (n, n), 0) == lax.broadcasted_iota(I32, (n, n), 1)
    return jnp.sum(jnp.where(eye, row, 0.0), axis=1, keepdims=True)


def _cmp_sample_kernel(pt_ref, nq_ref, poolk_ref, poolv_ref,
                       posk_ref, w1k_ref, w2k_ref, posv_ref, w1v_ref, w2v_ref,
                       ocmp_ref, idx_ref, kbuf, vbuf, ksem, vsem):
    step = pl.program_id(0)
    n_pages = pt_ref.shape[1]
    nbk = 2 * n_pages
    slot = step % 2

    def fetch(st, sl, start):
        for j in range(CMP_SAMPLES):
            for p in range(n_pages):
                rows = pl.ds(((sl * CMP_SAMPLES + j) * n_pages + p) * KV_WIDTH, KV_WIDTH)
                page = pt_ref[st * CMP_SAMPLES + j, p]
                for pool, buf, sem in ((poolk_ref, kbuf, ksem), (poolv_ref, vbuf, vsem)):
                    cp = pltpu.make_async_copy(pool.at[page], buf.at[rows, :], sem.at[sl])
                    cp.start() if start else cp.wait()

    @pl.when(step == 0)
    def _():
        fetch(0, 0, True)

    @pl.when(step + 1 < pl.num_programs(0))
    def _():
        fetch(step + 1, 1 - slot, True)

    fetch(step, slot, False)

    step_pages = CMP_SAMPLES * n_pages
    base = slot * step_pages * KV_WIDTH

    def rows_of(buf):
        def load(d):
            parts = [buf[pl.ds(base + g * HEAD_DIM + d, step_pages, stride=KV_WIDTH), :]
                     for g in range(NSA_KV_HEADS)]
            return jnp.concatenate(parts, axis=0)
        return load

    kc = _compress(rows_of(kbuf), posk_ref, w1k_ref, w2k_ref).astype(BF16)
    vc = _compress(rows_of(vbuf), posv_ref, w1v_ref, w2v_ref).astype(BF16)

    j_lane = lax.broadcasted_iota(I32, (nbk, nbk), 1)
    j_sub = lax.broadcasted_iota(I32, (nbk, nbk), 0)
    n_lane = 2 * (j_lane % n_pages) + j_lane // n_pages
    n_sub = 2 * (j_sub % n_pages) + j_sub // n_pages
    z = jnp.zeros((1, HEAD_DIM), F32)
    for j in range(CMP_SAMPLES):
        q = nq_ref[j]
        o_rows, idx_rows = [], []
        for g in range(NSA_KV_HEADS):
            blk_rows = slice((g * CMP_SAMPLES + j) * n_pages, (g * CMP_SAMPLES + j + 1) * n_pages)
            pieces = [q[:, (g * NSA_GROUP + r) * HEAD_DIM:(g * NSA_GROUP + r + 1) * HEAD_DIM]
                      for r in range(NSA_GROUP)]
            qh = jnp.concatenate([jnp.concatenate([pc, z], axis=1) for pc in pieces]
                                 + [jnp.concatenate([z, pc], axis=1) for pc in pieces], axis=0)
            s = _dot_nt(qh.astype(BF16), kc[blk_rows])
            m = jnp.max(s, axis=1, keepdims=True)
            m = jnp.maximum(m[:NSA_GROUP], m[NSA_GROUP:])
            e = jnp.exp(s - jnp.concatenate([m, m], axis=0))
            l = jnp.sum(e, axis=1, keepdims=True)
            l = l[:NSA_GROUP] + l[NSA_GROUP:]
            p = e / jnp.concatenate([l, l], axis=0)
            out = _dot(p.astype(BF16), vc[blk_rows])
            o_rows.append(out[:NSA_GROUP, :HEAD_DIM] + out[NSA_GROUP:, HEAD_DIM:])
            imp = jnp.concatenate([jnp.sum(p[:NSA_GROUP], axis=0, keepdims=True),
                                   jnp.sum(p[NSA_GROUP:], axis=0, keepdims=True)], axis=1)
            imp_col = _column(imp)
            ahead = (imp > imp_col) | ((imp == imp_col) & (n_lane < n_sub))
            rank_col = jnp.sum(jnp.where(ahead, 1, 0), axis=1, keepdims=True)
            idx_rows.append(jnp.sum(jnp.where(rank_col == j_lane, n_sub, 0), axis=0, keepdims=True))
        ocmp_ref[j] = jnp.concatenate([o[r:r + 1] for o in o_rows for r in range(NSA_GROUP)], axis=1)
        idx_ref[j] = jnp.concatenate(idx_rows, axis=0)


def _cmp_sample(page_table, nq, pool_k, pool_v, cwk, cwv):
    DB, n_pages = page_table.shape
    nbk = 2 * n_pages
    wspecs = _cmp_weight_specs(lambda s, pt: (0, 0), lambda s, pt: (0, 0, 0))
    any_spec = pl.BlockSpec(memory_space=pl.ANY)
    buf_rows = 2 * CMP_SAMPLES * n_pages * KV_WIDTH
    grid_spec = pltpu.PrefetchScalarGridSpec(
        num_scalar_prefetch=1,
        grid=(DB // CMP_SAMPLES,),
        in_specs=[pl.BlockSpec((CMP_SAMPLES, 1, NSA_WIDTH), lambda s, pt: (s, 0, 0)), any_spec, any_spec]
        + wspecs + wspecs,
        out_specs=[pl.BlockSpec((CMP_SAMPLES, 1, NSA_WIDTH), lambda s, pt: (s, 0, 0)),
                   pl.BlockSpec((CMP_SAMPLES, NSA_KV_HEADS, nbk), lambda s, pt: (s, 0, 0))],
        scratch_shapes=[pltpu.VMEM((buf_rows, PAGE_SIZE), F32),
                        pltpu.VMEM((buf_rows, PAGE_SIZE), F32),
                        pltpu.SemaphoreType.DMA((2,)),
                        pltpu.SemaphoreType.DMA((2,))],
    )
    return pl.pallas_call(
        _cmp_sample_kernel,
        grid_spec=grid_spec,
        out_shape=[jax.ShapeDtypeStruct((DB, 1, NSA_WIDTH), F32),
                   jax.ShapeDtypeStruct((DB, NSA_KV_HEADS, nbk), I32)],
        compiler_params=_params("arbitrary"),
        name="cmp_sample",
    )(page_table, nq.reshape(DB, 1, NSA_WIDTH), pool_k, pool_v, *cwk, *cwv)


def _att_sample_kernel(idx_ref, pt_ref, nq_ref, knew_ref, vnew_ref, wknew_ref, wvnew_ref,
                       gt_ref, ocmp_ref, poolk_ref, poolv_ref, wkT_ref, wvT_ref,
                       y_ref, wk_out_ref, wv_out_ref, kbuf, vbuf, ksem, vsem, *, past_len):
    b = pl.program_id(0)
    slot = b % 2

    def gather(bb, sl, start):
        for g in range(NSA_KV_HEADS):
            for s in range(N_PICK):
                blk = idx_ref[(bb * NSA_KV_HEADS + g) * N_SELECT + s]
                page = pt_ref[bb, blk // 2]
                for pool, buf, sem in ((poolk_ref, kbuf, ksem), (poolv_ref, vbuf, vsem)):
                    cp = pltpu.make_async_copy(pool.at[page, pl.ds(g * HEAD_DIM, HEAD_DIM), :],
                                               buf.at[sl, g, :, pl.ds(s * PAGE_SIZE, PAGE_SIZE)], sem.at[sl])
                    cp.start() if start else cp.wait()

    @pl.when(b == 0)
    def _():
        gather(0, 0, True)

    @pl.when(b + 1 < pl.num_programs(0))
    def _():
        gather(b + 1, 1 - slot, True)

    q = nq_ref[0]
    pieces = [q[:, hh * HEAD_DIM:(hh + 1) * HEAD_DIM] for hh in range(NSA_HEADS)]
    z = jnp.zeros((1, HEAD_DIM), F32)
    q_wide = jnp.concatenate([jnp.concatenate([pc, z] if hh < NSA_GROUP else [z, pc], axis=1)
                              for hh, pc in enumerate(pieces)], axis=0)

    def bf_round(x):
        return x.astype(BF16).astype(F32)

    def attend(qf, s_past, vT_past, k_new, v_new):
        s_self = jnp.sum(bf_round(qf) * bf_round(k_new), axis=1, keepdims=True)
        m = jnp.maximum(jnp.max(s_past, axis=-1, keepdims=True), s_self)
        p = jnp.exp(s_past - m)
        p_self = jnp.exp(s_self - m)
        l = jnp.sum(p, axis=-1, keepdims=True) + p_self
        o = _dot_nt(p.astype(BF16), vT_past) + bf_round(p_self) * bf_round(v_new)
        return o / l

    wkT = wkT_ref[0]
    wvT = wvT_ref[0]
    wb = wkT.shape[1]
    j = lax.broadcasted_iota(I32, (NSA_HEADS, wb), 1)
    pos_k = past_len - wb + j
    keep = (past_len - pos_k < WINDOW) & (pos_k >= 0)
    s_win = jnp.where(keep, _dot(q_wide.astype(BF16), wkT.astype(BF16)), NEG)
    o_win = attend(q_wide, s_win, wvT.astype(BF16), wknew_ref[0], wvnew_ref[0])

    lane_w = lax.broadcasted_iota(I32, (KV_WIDTH, wb), 1)
    wk_out_ref[0] = jnp.where(lane_w == wb - 1, _column(wknew_ref[0]), pltpu.roll(wkT, wb - 1, 1))
    wv_out_ref[0] = jnp.where(lane_w == wb - 1, _column(wvnew_ref[0]), pltpu.roll(wvT, wb - 1, 1))

    gather(b, slot, False)
    n_keys = N_PICK * PAGE_SIZE
    half_of_lane = (lax.broadcasted_iota(I32, (1, n_keys), 1) % PAGE_SIZE) // CMP_BLOCK
    zeros4 = jnp.zeros((NSA_GROUP, HEAD_DIM), F32)
    gt = gt_ref[0]
    ocmp = ocmp_ref[0]
    y_pieces = []
    for g in range(NSA_KV_HEADS):
        gs = slice(g * HEAD_DIM, (g + 1) * HEAD_DIM)
        want = jnp.concatenate(
            [jnp.full((1, PAGE_SIZE), idx_ref[(b * NSA_KV_HEADS + g) * N_SELECT + s] % 2, I32)
             for s in range(N_PICK)], axis=1)
        qg = jnp.concatenate(pieces[g * NSA_GROUP:(g + 1) * NSA_GROUP] + [zeros4], axis=0)
        s_sel = jnp.where(half_of_lane == want, _dot(qg.astype(BF16), kbuf[slot, g].astype(BF16)), NEG)
        o_sel = attend(qg, s_sel, vbuf[slot, g].astype(BF16), knew_ref[0][:, gs], vnew_ref[0][:, gs])
        for r in range(NSA_GROUP):
            hh = g * NSA_GROUP + r
            y_pieces.append(gt[:, N_BRANCH * hh:N_BRANCH * hh + 1] * ocmp[:, hh * HEAD_DIM:(hh + 1) * HEAD_DIM]
                            + gt[:, N_BRANCH * hh + 1:N_BRANCH * hh + 2] * o_sel[r:r + 1]
                            + gt[:, N_BRANCH * hh + 2:N_BRANCH * hh + 3] * o_win[hh:hh + 1, gs])
    y_ref[0] = jnp.concatenate(y_pieces, axis=1)


def _att_sample(idx, page_table, nq, sk, sv, wk_new, wv_new, gates, ocmp, pool_k, pool_v, buf_k, buf_v):
    DB, n_pages = page_table.shape
    past_len = n_pages * PAGE_SIZE
    wb = buf_k.shape[2]
    r3 = lambda t: t.reshape(DB, 1, t.shape[-1])
    row = lambda w: pl.BlockSpec((1, 1, w), lambda b, ix, pt: (b, 0, 0))
    any_spec = pl.BlockSpec(memory_space=pl.ANY)
    wspec = pl.BlockSpec((1, KV_WIDTH, wb), lambda b, ix, pt: (b, 0, 0))
    buf_shape = (2, NSA_KV_HEADS, HEAD_DIM, N_PICK * PAGE_SIZE)
    grid_spec = pltpu.PrefetchScalarGridSpec(
        num_scalar_prefetch=2,
        grid=(DB,),
        in_specs=[row(NSA_WIDTH), row(KV_WIDTH), row(KV_WIDTH), row(KV_WIDTH), row(KV_WIDTH),
                  row(GATE_PAD), row(NSA_WIDTH), any_spec, any_spec, wspec, wspec],
        out_specs=[row(NSA_WIDTH), wspec, wspec],
        scratch_shapes=[pltpu.VMEM(buf_shape, F32), pltpu.VMEM(buf_shape, F32),
                        pltpu.SemaphoreType.DMA((2,)), pltpu.SemaphoreType.DMA((2,))],
    )
    return pl.pallas_call(
        functools.partial(_att_sample_kernel, past_len=past_len),
        grid_spec=grid_spec,
        out_shape=[jax.ShapeDtypeStruct((DB, 1, NSA_WIDTH), F32),
                   jax.ShapeDtypeStruct(buf_k.shape, F32),
                   jax.ShapeDtypeStruct(buf_v.shape, F32)],
        compiler_params=_params("arbitrary"),
        name="att_sample",
    )(idx, page_table, r3(nq), r3(sk), r3(sv), r3(wk_new), r3(wv_new), r3(gates), ocmp,
      pool_k, pool_v, buf_k, buf_v)


def _row_tile(rows):
    return 256 if rows % 256 == 0 else 128


def kernel(x_prompt, x_sample, state_ret, cache_cmp_k, cache_cmp_v, cache_sel_k, cache_sel_v,
           cache_win_k, cache_win_v, page_table, ln_pre_mix, w_in, ret_gn_w,
           cmp_pos_k, cmp_w1_k, cmp_w2_k, cmp_pos_v, cmp_w1_v, cmp_w2_v,
           w_o, ln_post_mix, ln_pre_ffn, w_up, w_down, ln_post_ffn):
    B, T, _ = x_prompt.shape
    DB, Tn, _ = x_sample.shape
    n_pages = page_table.shape[1]
    past_len = n_pages * PAGE_SIZE
    wb = cache_win_k.shape[2]
    assert w_in.shape[0] == 1 and Tn == 1
    assert T % ATT_K == 0 and T >= WINDOW + ATT_Q and T // CMP_BLOCK <= MAX_BLOCKS
    assert 2 * n_pages >= N_PICK and DB % CMP_SAMPLES == 0

    w_main = w_in[0, :, :MAIN_WIDTH].astype(BF16)
    w_gate = jnp.pad(w_in[0, :, MAIN_WIDTH:], ((0, 0), (0, GATE_PAD - NSA_HEADS * N_BRANCH))).astype(BF16)
    tail = (w_o[0].astype(BF16), ln_post_mix, ln_pre_ffn, ln_post_ffn, w_up[0].astype(BF16), w_down[0].astype(BF16))

    tile_p = _row_tile(B * T)
    cos_p, sin_p = _rope_tables(jnp.arange(T, dtype=I32))
    xp = x_prompt.reshape(B * T, D_MODEL)
    (rq, rk, rv, rg, nqT, ck, cv, sk, wk, ckT, cvT, skT, svT, wkT, wvT, svTb, wvTb, gtT) = _inproj_prompt(
        xp, ln_pre_mix, w_main, w_gate, cos_p, sin_p, B, T, tile_p)
    b3 = lambda t: t.reshape(B, T, t.shape[-1])
    ret_y, rs_p = _ret_prompt(b3(rq), b3(rk), b3(rv), b3(rg), ret_gn_w)
    cwk = _compress_weights(cmp_pos_k[0], cmp_w1_k[0], cmp_w2_k[0], True)
    cwv = _compress_weights(cmp_pos_v[0], cmp_w1_v[0], cmp_w2_v[0], True)
    ocmpT, biasT = _cmp_prompt(b3(ck), b3(cv), cwk, cwv, nqT, gtT, _row_tile(T))
    nsa_y = _att_prompt(nqT, biasT, b3(sk), svTb, b3(wk), wvTb, gtT, ocmpT)
    y_p = _finish(xp, ret_y.reshape(B * T, RET_WIDTH), nsa_y.reshape(B * T, NSA_WIDTH), *tail, tile_p)
    win_p = min(WINDOW, T)
    kv5 = lambda t: t.reshape(t.shape[0], NSA_KV_HEADS, HEAD_DIM, t.shape[-1]).transpose(0, 3, 1, 2)[None]
    outs_p = (y_p.reshape(B, T, D_MODEL), rs_p[None],
              kv5(ckT), kv5(cvT), kv5(skT), kv5(svT),
              kv5(wkT[:, :, T - win_p:]), kv5(wvT[:, :, T - win_p:]))

    cos_s, sin_s = _rope_tables(jnp.full((DB,), past_len, I32))
    xs = x_sample.reshape(DB, D_MODEL)
    (rqT, rkT, rvT, rgT, nq, ckT, cvT, skT, svT, sk, sv, wk, wv, gt) = _inproj_sample(
        xs, ln_pre_mix, w_main, w_gate, cos_s, sin_s)
    ret_yT, rs_s = _ret_sample(rqT, rkT, rvT, rgT, ret_gn_w, state_ret[0].transpose(1, 2, 3, 0))
    feat_major = lambda t: t[0].transpose(0, 2, 3, 1).reshape(t.shape[1], KV_WIDTH, t.shape[2])
    cwk = _compress_weights(cmp_pos_k[0], cmp_w1_k[0], cmp_w2_k[0], False)
    cwv = _compress_weights(cmp_pos_v[0], cmp_w1_v[0], cmp_w2_v[0], False)
    ocmp, idx = _cmp_sample(page_table, nq, feat_major(cache_cmp_k), feat_major(cache_cmp_v), cwk, cwv)
    idx_flat = idx[:, :, :N_SELECT].reshape(-1)
    y_row, nwk, nwv = _att_sample(idx_flat, page_table, nq, sk, sv, wk, wv, gt, ocmp,
                                  feat_major(cache_sel_k), feat_major(cache_sel_v),
                                  feat_major(cache_win_k), feat_major(cache_win_v))
    y_s = _finish(xs, ret_yT, y_row.reshape(DB, NSA_WIDTH), *tail, DB, ret_t=True)
    new5 = lambda t: t.reshape(NSA_KV_HEADS, HEAD_DIM, DB).transpose(2, 0, 1).reshape(1, DB, 1, NSA_KV_HEADS, HEAD_DIM)
    outs_s = (y_s.reshape(DB, 1, D_MODEL), rs_s.transpose(3, 0, 1, 2)[None],
              new5(ckT), new5(cvT), new5(skT), new5(svT), kv5(nwk), kv5(nwv))

    return (outs_p[0], outs_s[0]) + outs_p[1:] + outs_s[1:]
```

```python
import functools

import numpy as np
import jax
import jax.numpy as jnp
from jax import lax
from jax.experimental import pallas as pl
from jax.experimental.pallas import tpu as pltpu

F32 = jnp.float32
BF16 = jnp.bfloat16
I32 = jnp.int32

D_MODEL = 1024
HEAD_DIM = 64
RET_WIDTH = 512
RET_HEADS = 8
NSA_WIDTH = 512
NSA_HEADS = 8
NSA_KV_HEADS = 2
NSA_GROUP = 4
KV_WIDTH = 128
N_BRANCH = 3
D_FF = 4 * D_MODEL
CMP_BLOCK = 64
N_SELECT = 16
N_PICK = N_SELECT - 1
WINDOW = 512
PAGE_SIZE = 128
RET_CHUNK = 128
ROPE_BASE = 10000.0
EPS = 1e-6
NEG = -1e30
SCALE = HEAD_DIM ** -0.5

MAIN_WIDTH = 4 * RET_WIDTH + NSA_WIDTH + 6 * KV_WIDTH
GATE_PAD = 128
ATT_Q = 128
ATT_K = 256
MAX_BLOCKS = 64
CMP_SAMPLES = 2
SUBLANES = 8
PAGE_TILES = KV_WIDTH // SUBLANES
VMEM_LIMIT = 56 * 1024 * 1024

_LOG_GAMMA = [float(np.log1p(-np.exp2(-5.0 - h))) for h in range(RET_HEADS)]


def _params(*sem):
    return pltpu.CompilerParams(dimension_semantics=sem, vmem_limit_bytes=VMEM_LIMIT)


def _dot(a, b):
    return jnp.dot(a, b, preferred_element_type=F32)


def _dot_nt(a, b):
    return lax.dot_general(a, b, (((1,), (1,)), ((), ())), preferred_element_type=F32)


def _dot_tn(a, b):
    return lax.dot_general(a, b, (((0,), (0,)), ((), ())), preferred_element_type=F32)


def _rms(x, w):
    return x * lax.rsqrt(jnp.mean(x * x, axis=-1, keepdims=True) + EPS) * w


def _projector(x_ref, lnw_ref, w_ref, cos_ref, sin_ref):
    h = _rms(x_ref[...], lnw_ref[...]).astype(BF16)
    rows = h.shape[0]

    def proj(a, b):
        return _dot(h, w_ref[:, a:b])

    cos = cos_ref[...]
    sin = sin_ref[...]
    lane = lax.broadcasted_iota(I32, (rows, RET_WIDTH), 1)
    first_half = (lane % HEAD_DIM) < (HEAD_DIM // 2)

    def rope(y):
        partner = jnp.where(first_half,
                            pltpu.roll(y, RET_WIDTH - HEAD_DIM // 2, 1),
                            pltpu.roll(y, HEAD_DIM // 2, 1))
        return y * cos + partner * sin

    return h, proj, rope


def _rope_tables(pos):
    half = HEAD_DIM // 2
    inv = ROPE_BASE ** (-jnp.arange(half, dtype=F32) / half)
    ang = pos.astype(F32)[:, None] * inv
    cos, sin = jnp.cos(ang), jnp.sin(ang)
    cos_h = jnp.concatenate([cos, cos], -1)
    sin_h = jnp.concatenate([-sin, sin], -1)
    return jnp.tile(cos_h, (1, RET_HEADS)), jnp.tile(sin_h, (1, RET_HEADS))


def _inproj_prompt_kernel(x_ref, lnw_ref, w_ref, wg_ref, cos_ref, sin_ref,
                          rq_ref, rk_ref, rv_ref, rg_ref, nqT_ref, ck_ref, cv_ref, sk_ref, wk_ref,
                          ckT_ref, cvT_ref, skT_ref, svT_ref, wkT_ref, wvT_ref, svTb_ref, wvTb_ref, gtT_ref):
    h, proj, rope = _projector(x_ref, lnw_ref, w_ref, cos_ref, sin_ref)
    rq_ref[...] = rope(proj(0, 512)).astype(BF16)
    rk_ref[...] = (rope(proj(512, 1024)) * SCALE).astype(BF16)
    rv_ref[...] = proj(1024, 1536).astype(BF16)
    rg_ref[...] = proj(1536, 2048)
    for n in range(NSA_WIDTH // KV_WIDTH):
        nq = proj(2048 + n * KV_WIDTH, 2048 + (n + 1) * KV_WIDTH) * SCALE
        nqT_ref[0, n * KV_WIDTH:(n + 1) * KV_WIDTH, :] = nq.T.astype(BF16)
    kv = [proj(2560 + n * KV_WIDTH, 2560 + (n + 1) * KV_WIDTH) for n in range(6)]
    ck_ref[...] = kv[0]
    cv_ref[...] = kv[1]
    sk_ref[...] = kv[2].astype(BF16)
    wk_ref[...] = kv[4].astype(BF16)
    for n, ref in enumerate((ckT_ref, cvT_ref, skT_ref, svT_ref, wkT_ref, wvT_ref)):
        ref[0] = kv[n].T
    svTb_ref[0] = kv[3].T.astype(BF16)
    wvTb_ref[0] = kv[5].T.astype(BF16)
    gtT_ref[0] = jax.nn.sigmoid(_dot(h, wg_ref[...])).T


def _inproj_prompt(x2d, lnw, w_main, w_gate, cos_t, sin_t, B, T, tile):
    rows = x2d.shape[0]
    nt = T // tile
    row = lambda i: (i, 0)
    const = lambda i: (0, 0)
    tab = lambda i: (i % nt, 0)
    colT = lambda i: (i // nt, 0, i % nt)
    rm = lambda w, dt: (pl.BlockSpec((tile, w), row), jax.ShapeDtypeStruct((rows, w), dt))
    tr = lambda w, dt: (pl.BlockSpec((1, w, tile), colT), jax.ShapeDtypeStruct((B, w, T), dt))
    outs = [rm(RET_WIDTH, BF16), rm(RET_WIDTH, BF16), rm(RET_WIDTH, BF16), rm(RET_WIDTH, F32),
            tr(NSA_WIDTH, BF16), rm(KV_WIDTH, F32), rm(KV_WIDTH, F32), rm(KV_WIDTH, BF16), rm(KV_WIDTH, BF16),
            *[tr(KV_WIDTH, F32)] * 6,
            tr(KV_WIDTH, BF16), tr(KV_WIDTH, BF16), tr(GATE_PAD, F32)]
    return pl.pallas_call(
        _inproj_prompt_kernel,
        grid=(rows // tile,),
        in_specs=[pl.BlockSpec((tile, D_MODEL), row),
                  pl.BlockSpec((1, D_MODEL), const),
                  pl.BlockSpec((D_MODEL, MAIN_WIDTH), const),
                  pl.BlockSpec((D_MODEL, GATE_PAD), const),
                  pl.BlockSpec((tile, RET_WIDTH), tab),
                  pl.BlockSpec((tile, RET_WIDTH), tab)],
        out_specs=[o[0] for o in outs],
        out_shape=[o[1] for o in outs],
        compiler_params=_params("parallel"),
        name="inproj_prompt",
    )(x2d, lnw, w_main, w_gate, cos_t, sin_t)


def _inproj_sample_kernel(x_ref, lnw_ref, w_ref, wg_ref, cos_ref, sin_ref,
                          rqT_ref, rkT_ref, rvT_ref, rgT_ref, nq_ref,
                          ckT_ref, cvT_ref, skT_ref, svT_ref, sk_ref, sv_ref, wk_ref, wv_ref, gt_ref):
    h, proj, rope = _projector(x_ref, lnw_ref, w_ref, cos_ref, sin_ref)
    rqT_ref[...] = rope(proj(0, 512)).T
    rkT_ref[...] = (rope(proj(512, 1024)) * SCALE).T
    rvT_ref[...] = proj(1024, 1536).T
    rgT_ref[...] = proj(1536, 2048).T
    nq_ref[...] = proj(2048, 2560) * SCALE
    kv = [proj(2560 + n * KV_WIDTH, 2560 + (n + 1) * KV_WIDTH) for n in range(6)]
    for n, ref in enumerate((ckT_ref, cvT_ref, skT_ref, svT_ref)):
        ref[...] = kv[n].T
    for n, ref in enumerate((sk_ref, sv_ref, wk_ref, wv_ref)):
        ref[...] = kv[2 + n]
    gt_ref[...] = jax.nn.sigmoid(_dot(h, wg_ref[...]))


def _inproj_sample(x2d, lnw, w_main, w_gate, cos_t, sin_t):
    rows = x2d.shape[0]
    full = lambda a, b: pl.BlockSpec((a, b), lambda i: (0, 0))
    rm = lambda w: (full(rows, w), jax.ShapeDtypeStruct((rows, w), F32))
    tr = lambda w: (full(w, rows), jax.ShapeDtypeStruct((w, rows), F32))
    outs = [tr(RET_WIDTH)] * 4 + [rm(NSA_WIDTH)] + [tr(KV_WIDTH)] * 4 + [rm(KV_WIDTH)] * 4 + [rm(GATE_PAD)]
    return pl.pallas_call(
        _inproj_sample_kernel,
        grid=(1,),
        in_specs=[full(rows, D_MODEL), full(1, D_MODEL), full(D_MODEL, MAIN_WIDTH), full(D_MODEL, GATE_PAD),
                  full(rows, RET_WIDTH), full(rows, RET_WIDTH)],
        out_specs=[o[0] for o in outs],
        out_shape=[o[1] for o in outs],
        compiler_params=_params("arbitrary"),
        name="inproj_sample",
    )(x2d, lnw, w_main, w_gate, cos_t, sin_t)


def _group_norm_gate(outs, g, gnw):
    ys = []
    for o in outs:
        mu = jnp.mean(o, axis=-1, keepdims=True)
        d = o - mu
        var = jnp.mean(d * d, axis=-1, keepdims=True)
        ys.append(d * lax.rsqrt(var + EPS))
    y = jnp.concatenate(ys, axis=1) * gnw
    return g * jax.nn.sigmoid(g) * y


def _ret_prompt_kernel(q_ref, k_ref, v_ref, g_ref, gnw_ref, inner_ref, qdec_ref, kdec_ref,
                       y_ref, s_out_ref, s_scr):
    c = pl.program_id(1)

    @pl.when(c == 0)
    def _():
        s_scr[...] = jnp.zeros_like(s_scr)

    q = q_ref[0]
    k = k_ref[0]
    v = v_ref[0]
    kd = (k.astype(F32) * kdec_ref[...]).astype(BF16)
    qdec = qdec_ref[...]
    outs = []
    for h in range(RET_HEADS):
        sl = slice(h * HEAD_DIM, (h + 1) * HEAD_DIM)
        qh, kh, vh = q[:, sl], k[:, sl], v[:, sl]
        a = _dot_nt(qh, kh) * inner_ref[h]
        s = s_scr[h]
        o = _dot(a.astype(BF16), vh) + _dot(qh, s.astype(BF16)) * qdec[:, sl]
        s_scr[h] = s * float(np.exp(_LOG_GAMMA[h] * RET_CHUNK)) + _dot_tn(kd[:, sl], vh)
        outs.append(o)
    y_ref[0] = _group_norm_gate(outs, g_ref[0], gnw_ref[...]).astype(BF16)

    @pl.when(c == pl.num_programs(1) - 1)
    def _():
        s_out_ref[0] = s_scr[...]


def _ret_prompt(rq, rk, rv, rg, gnw):
    B, T, _ = rq.shape
    C = RET_CHUNK
    lg = jnp.asarray(_LOG_GAMMA, F32)
    i = jnp.arange(C, dtype=F32)
    diff = i[:, None] - i[None, :]
    inner = jnp.where(diff >= 0, jnp.exp(lg[:, None, None] * jnp.maximum(diff, 0.0)), 0.0)
    qdec = jnp.repeat(jnp.exp(lg[None, :] * (i[:, None] + 1.0)), HEAD_DIM, axis=1)
    kdec = jnp.repeat(jnp.exp(lg[None, :] * (C - 1.0 - i[:, None])), HEAD_DIM, axis=1)
    blk = pl.BlockSpec((1, C, RET_WIDTH), lambda b, c: (b, c, 0))
    c2 = lambda b, c: (0, 0)
    return pl.pallas_call(
        _ret_prompt_kernel,
        grid=(B, T // C),
        in_specs=[blk, blk, blk, blk,
                  pl.BlockSpec((1, RET_WIDTH), c2),
                  pl.BlockSpec((RET_HEADS, C, C), lambda b, c: (0, 0, 0)),
                  pl.BlockSpec((C, RET_WIDTH), c2),
                  pl.BlockSpec((C, RET_WIDTH), c2)],
        out_specs=[blk, pl.BlockSpec((1, RET_HEADS, HEAD_DIM, HEAD_DIM), lambda b, c: (b, 0, 0, 0))],
        out_shape=[jax.ShapeDtypeStruct((B, T, RET_WIDTH), BF16),
                   jax.ShapeDtypeStruct((B, RET_HEADS, HEAD_DIM, HEAD_DIM), F32)],
        scratch_shapes=[pltpu.VMEM((RET_HEADS, HEAD_DIM, HEAD_DIM), F32)],
        compiler_params=_params("parallel", "arbitrary"),
        name="ret_prompt",
    )(rq, rk, rv, rg, gnw, inner, qdec, kdec)


def _ret_sample_kernel(q_ref, k_ref, v_ref, g_ref, gnw_ref, gam_ref, s_ref, y_ref, s_out_ref):
    q = q_ref[...]
    k = k_ref[...]
    v = v_ref[...]
    g = g_ref[...]
    gamma = gam_ref[0]
    acc = jnp.zeros(v.shape, F32)
    for d in range(HEAD_DIM):
        s_d = s_ref[0, d]
        acc = acc + s_d * q[d:d + 1, :]
        s_out_ref[0, d] = s_d * gamma + k[d:d + 1, :] * v
    o = acc * gamma + jnp.sum(q * k, axis=0, keepdims=True) * v
    mu = jnp.mean(o, axis=0, keepdims=True)
    dlt = o - mu
    var = jnp.mean(dlt * dlt, axis=0, keepdims=True)
    y_ref[...] = g * jax.nn.sigmoid(g) * (dlt * lax.rsqrt(var + EPS) * gnw_ref[...])


def _ret_sample(rqT, rkT, rvT, rgT, gnw, state_t):
    DB = rqT.shape[1]
    blk = pl.BlockSpec((HEAD_DIM, DB), lambda h: (h, 0))
    st = pl.BlockSpec((1, HEAD_DIM, HEAD_DIM, DB), lambda h: (h, 0, 0, 0))
    gam = jnp.broadcast_to(jnp.exp(jnp.asarray(_LOG_GAMMA, F32))[:, None, None], (RET_HEADS, 1, DB))
    return pl.pallas_call(
        _ret_sample_kernel,
        grid=(RET_HEADS,),
        in_specs=[blk, blk, blk, blk,
                  pl.BlockSpec((HEAD_DIM, 1), lambda h: (h, 0)),
                  pl.BlockSpec((1, 1, DB), lambda h: (h, 0, 0)),
                  st],
        out_specs=[blk, st],
        out_shape=[jax.ShapeDtypeStruct((RET_WIDTH, DB), F32),
                   jax.ShapeDtypeStruct(state_t.shape, F32)],
        compiler_params=_params("parallel"),
        name="ret_sample",
    )(rqT, rkT, rvT, rgT, gnw.reshape(RET_WIDTH, 1), gam, state_t)


def _compress(load_rows, pos_ref, w1_ref, w2_ref):
    acc = None
    for j2 in range(CMP_BLOCK // 2):
        x = jnp.concatenate([load_rows(2 * j2) + pos_ref[2 * j2:2 * j2 + 1, :],
                             load_rows(2 * j2 + 1) + pos_ref[2 * j2 + 1:2 * j2 + 2, :]], axis=1).astype(BF16)
        part = _dot(x, w1_ref[j2])
        acc = part if acc is None else acc + part
    h = jax.nn.gelu(acc)
    return _dot(h.astype(BF16), w2_ref[...])


def _block_diag2(w):
    z = jnp.zeros_like(w)
    return jnp.concatenate([jnp.concatenate([w, z], -1), jnp.concatenate([z, w], -1)], -2)


def _compress_weights(pos, w1, w2, token_major):
    if not token_major:
        pos, w1 = pos.T, w1.transpose(1, 0, 2)
    pos2 = jnp.concatenate([pos, pos], -1)
    w1p = _block_diag2(w1).reshape(CMP_BLOCK // 2, 2 * KV_WIDTH, KV_WIDTH)
    return pos2, w1p.astype(BF16), _block_diag2(w2).astype(BF16)


def _rank_rows(imp):
    nb, tq = imp.shape
    sub = 8
    groups = [imp[sub * j:sub * (j + 1)] for j in range(nb // sub)]
    ranks = [jnp.zeros((sub, tq), I32) for _ in groups]
    row_in_group = lax.broadcasted_iota(I32, (sub, tq), 0)
    for m in range(nb):
        row = imp[m:m + 1, :]
        jm, sm = divmod(m, sub)
        for j, grp in enumerate(groups):
            if j < jm:
                inc = jnp.where(row > grp, 1, 0)
            elif j > jm:
                inc = jnp.where(row >= grp, 1, 0)
            else:
                inc = jnp.where(row_in_group > sm, jnp.where(row >= grp, 1, 0), jnp.where(row > grp, 1, 0))
            ranks[j] = ranks[j] + inc
    return jnp.concatenate(ranks, axis=0)


def _kv_head_rows(x, g):
    z = jnp.zeros_like(x)
    return jnp.concatenate([x, z] if g == 0 else [z, x], axis=0)


def _cmp_prompt_kernel(ck_ref, cv_ref, posk_ref, w1k_ref, w2k_ref, posv_ref, w1v_ref, w2v_ref,
                       nqT_ref, gtT_ref, ocmpT_ref, biasT_ref, kc_scr, vc_scr):
    i = pl.program_id(1)
    nb = kc_scr.shape[0]
    tq = nqT_ref.shape[2]

    @pl.when(i == 0)
    def _():
        kc_scr[...] = _compress(lambda l: ck_ref[0, pl.ds(l, nb, stride=CMP_BLOCK), :],
                                posk_ref, w1k_ref, w2k_ref).astype(BF16)
        vc_scr[...] = _compress(lambda l: cv_ref[0, pl.ds(l, nb, stride=CMP_BLOCK), :],
                                posv_ref, w1v_ref, w2v_ref).astype(BF16)

    kc = kc_scr[...]
    vc = vc_scr[...]
    pos = i * tq + lax.broadcasted_iota(I32, (nb, tq), 1)
    n_idx = lax.broadcasted_iota(I32, (nb, tq), 0)
    valid = (n_idx + 1) * CMP_BLOCK - 1 <= pos
    cur = pos // CMP_BLOCK
    cand = n_idx < cur
    for g in range(NSA_KV_HEADS):
        imp = jnp.zeros((nb, tq), F32)
        for r in range(NSA_GROUP):
            hh = g * NSA_GROUP + r
            qT = _kv_head_rows(nqT_ref[0, hh * HEAD_DIM:(hh + 1) * HEAD_DIM, :], g)
            s = jnp.where(valid, _dot(kc, qT), NEG)
            e = jnp.exp(s - jnp.max(s, axis=0, keepdims=True))
            p = jnp.where(valid, e / jnp.sum(e, axis=0, keepdims=True), 0.0)
            imp = imp + p
            oT = _dot_tn(vc, p.astype(BF16))
            ocmpT_ref[0, hh * HEAD_DIM:(hh + 1) * HEAD_DIM, :] = (
                oT[g * HEAD_DIM:(g + 1) * HEAD_DIM] * gtT_ref[0, N_BRANCH * hh:N_BRANCH * hh + 1, :])
        imp = jnp.where(cand, imp, -1.0)
        rank = _rank_rows(imp)
        sel = (cand & (rank < N_PICK)) | (n_idx == cur)
        biasT_ref[0, g * MAX_BLOCKS:g * MAX_BLOCKS + nb, :] = jnp.where(sel, 0.0, NEG).astype(BF16)
        if nb < MAX_BLOCKS:
            biasT_ref[0, g * MAX_BLOCKS + nb:(g + 1) * MAX_BLOCKS, :] = jnp.zeros((MAX_BLOCKS - nb, tq), BF16)


def _cmp_weight_specs(c2, c3):
    return [pl.BlockSpec((CMP_BLOCK, KV_WIDTH), c2),
            pl.BlockSpec((CMP_BLOCK // 2, 2 * KV_WIDTH, KV_WIDTH), c3),
            pl.BlockSpec((KV_WIDTH, KV_WIDTH), c2)]


def _cmp_prompt(ck, cv, cwk, cwv, nqT, gtT, tq):
    B, T, _ = ck.shape
    nb = T // CMP_BLOCK
    full = pl.BlockSpec((1, T, KV_WIDTH), lambda b, i: (b, 0, 0))
    wspecs = _cmp_weight_specs(lambda b, i: (0, 0), lambda b, i: (0, 0, 0))
    blk = lambda w: pl.BlockSpec((1, w, tq), lambda b, i: (b, 0, i))
    return pl.pallas_call(
        _cmp_prompt_kernel,
        grid=(B, T // tq),
        in_specs=[full, full] + wspecs + wspecs + [blk(NSA_WIDTH), blk(GATE_PAD)],
        out_specs=[blk(NSA_WIDTH), blk(NSA_KV_HEADS * MAX_BLOCKS)],
        out_shape=[jax.ShapeDtypeStruct((B, NSA_WIDTH, T), F32),
                   jax.ShapeDtypeStruct((B, NSA_KV_HEADS * MAX_BLOCKS, T), BF16)],
        scratch_shapes=[pltpu.VMEM((nb, KV_WIDTH), BF16), pltpu.VMEM((nb, KV_WIDTH), BF16)],
        compiler_params=_params("parallel", "arbitrary"),
        name="cmp_prompt",
    )(ck, cv, *cwk, *cwv, nqT, gtT)


def _att_prompt_kernel(nqT_ref, biasT_ref, sk_ref, svT_ref, wk_ref, wvT_ref, gtT_ref, ocmpT_ref,
                       y_ref, ka_scr, va_scr, s_scr, m_scr, acc_scr):
    i = pl.program_id(1)
    tq = nqT_ref.shape[2]
    T = sk_ref.shape[1]
    cols = NSA_GROUP * tq

    @pl.when(i == 0)
    def _():
        sk = sk_ref[0].astype(F32)
        lane = lax.broadcasted_iota(I32, (T, KV_WIDTH), 1)
        blk = lax.broadcasted_iota(I32, (T, KV_WIDTH), 0) // CMP_BLOCK
        ka_scr[0] = jnp.where(lane < HEAD_DIM, sk, jnp.where(lane - HEAD_DIM == blk, 1.0, 0.0)).astype(BF16)
        ka_scr[1] = jnp.where(lane >= HEAD_DIM, sk, jnp.where(lane == blk, 1.0, 0.0)).astype(BF16)
        svT = svT_ref[0].astype(F32)
        row = lax.broadcasted_iota(I32, (KV_WIDTH, T), 0)
        va_scr[0] = jnp.where(row < HEAD_DIM, svT, 1.0).astype(BF16)
        va_scr[1] = jnp.where(row >= HEAD_DIM, svT, 1.0).astype(BF16)

    q_pos = i * tq + lax.broadcasted_iota(I32, (ATT_K, cols), 1) % tq
    k_off = lax.broadcasted_iota(I32, (ATT_K, cols), 0)
    n_win = WINDOW // tq
    win_keys = (n_win + 1) * tq
    win_start = pl.multiple_of(jnp.maximum(i - n_win, 0) * tq, tq)
    rel = i * tq - win_start + lax.broadcasted_iota(I32, (win_keys, tq), 1)
    j_win = lax.broadcasted_iota(I32, (win_keys, tq), 0)
    win_bias = jnp.where((j_win <= rel) & (j_win > rel - WINDOW), 0.0, NEG)
    win_bias = jnp.concatenate([win_bias] * NSA_GROUP, axis=1)
    wk = wk_ref[0, pl.ds(win_start, win_keys), :]
    wvT = wvT_ref[0, :, pl.ds(win_start, win_keys)]

    qaTs, qwTs = [], []
    for g in range(NSA_KV_HEADS):
        qTs = [nqT_ref[0, (g * NSA_GROUP + r) * HEAD_DIM:(g * NSA_GROUP + r + 1) * HEAD_DIM, :]
               for r in range(NSA_GROUP)]
        bT = biasT_ref[0, g * MAX_BLOCKS:(g + 1) * MAX_BLOCKS, :]
        qaTs.append(jnp.concatenate([jnp.concatenate([q, bT] if g == 0 else [bT, q], axis=0) for q in qTs],
                                    axis=1))
        qwTs.append(jnp.concatenate([_kv_head_rows(q, g) for q in qTs], axis=1))

    def scores_into(slot, c):
        start = pl.multiple_of(c * ATT_K, ATT_K)
        for g in range(NSA_KV_HEADS):
            s_scr[slot, g] = _dot(ka_scr[g, pl.ds(start, ATT_K), :], qaTs[g])

    def sel_step(c, slot, last):
        start = pl.multiple_of(c * ATT_K, ATT_K)
        if not last:
            scores_into(1 - slot, c + 1)
        for g in range(NSA_KV_HEADS):
            s = s_scr[slot, g]
            if last:
                s = jnp.where(start + k_off > q_pos, NEG, s)
            m = m_scr[g]
            m_new = jnp.maximum(m, jnp.max(s, axis=0, keepdims=True))
            m_scr[g] = m_new
            p = jnp.exp(s - m_new)
            pv = _dot(va_scr[g, :, pl.ds(start, ATT_K)], p.astype(BF16))
            acc_scr[g] = jnp.exp(m - m_new) * acc_scr[g] + pv

    m_scr[...] = jnp.full(m_scr.shape, NEG, F32)
    acc_scr[...] = jnp.zeros(acc_scr.shape, F32)
    scores_into(0, 0)
    n_before = (i * tq) // ATT_K

    def pair(pr, carry):
        sel_step(2 * pr, 0, False)
        sel_step(2 * pr + 1, 1, False)
        return carry

    lax.fori_loop(0, n_before // 2, pair, 0)

    @pl.when(n_before % 2 == 1)
    def _():
        sel_step(n_before - 1, 0, False)

    sel_step(n_before, n_before % 2, True)

    heads_out = []
    for g in range(NSA_KV_HEADS):
        gs = slice(g * HEAD_DIM, (g + 1) * HEAD_DIM)
        acc = acc_scr[g]
        ones_row = (1 - g) * HEAD_DIM
        o_sel = acc[gs] / acc[ones_row:ones_row + 1]

        s = _dot(wk, qwTs[g]) + win_bias
        p = jnp.exp(s - jnp.max(s, axis=0, keepdims=True))
        o_win = _dot(wvT, p.astype(BF16))[gs] / jnp.sum(p, axis=0, keepdims=True)

        for r in range(NSA_GROUP):
            hh = g * NSA_GROUP + r
            cs = slice(r * tq, (r + 1) * tq)
            heads_out.append(ocmpT_ref[0, hh * HEAD_DIM:(hh + 1) * HEAD_DIM, :]
                             + gtT_ref[0, N_BRANCH * hh + 1:N_BRANCH * hh + 2, :] * o_sel[:, cs]
                             + gtT_ref[0, N_BRANCH * hh + 2:N_BRANCH * hh + 3, :] * o_win[:, cs])
    y_ref[0] = jnp.concatenate(heads_out, axis=0).T.astype(BF16)


def _att_prompt(nqT, biasT, sk, svT, wk, wvT, gtT, ocmpT):
    B, _, T = nqT.shape
    tq = ATT_Q
    rows_full = pl.BlockSpec((1, T, KV_WIDTH), lambda b, i: (b, 0, 0))
    cols_full = pl.BlockSpec((1, KV_WIDTH, T), lambda b, i: (b, 0, 0))
    blk = lambda w: pl.BlockSpec((1, w, tq), lambda b, i: (b, 0, i))
    return pl.pallas_call(
        _att_prompt_kernel,
        grid=(B, T // tq),
        in_specs=[blk(NSA_WIDTH), blk(NSA_KV_HEADS * MAX_BLOCKS), rows_full, cols_full, rows_full, cols_full,
                  blk(GATE_PAD), blk(NSA_WIDTH)],
        out_specs=pl.BlockSpec((1, tq, NSA_WIDTH), lambda b, i: (b, i, 0)),
        out_shape=jax.ShapeDtypeStruct((B, T, NSA_WIDTH), BF16),
        scratch_shapes=[pltpu.VMEM((NSA_KV_HEADS, T, KV_WIDTH), BF16),
                        pltpu.VMEM((NSA_KV_HEADS, KV_WIDTH, T), BF16),
                        pltpu.VMEM((2, NSA_KV_HEADS, ATT_K, NSA_GROUP * tq), F32),
                        pltpu.VMEM((NSA_KV_HEADS, 1, NSA_GROUP * tq), F32),
                        pltpu.VMEM((NSA_KV_HEADS, KV_WIDTH, NSA_GROUP * tq), F32)],
        compiler_params=_params("parallel", "arbitrary"),
        name="att_prompt",
    )(nqT, biasT, sk, svT, wk, wvT, gtT, ocmpT)


def _finish_kernel(x_ref, ry_ref, ny_ref, wo_ref, l1_ref, l2_ref, l3_ref, wu_ref, wd_ref, o_ref, *, ret_t):
    if ret_t:
        mix = _dot_tn(ry_ref[...].astype(BF16), wo_ref[0:RET_WIDTH, :])
    else:
        mix = _dot(ry_ref[...], wo_ref[0:RET_WIDTH, :])
    mix = mix + _dot(ny_ref[...].astype(BF16), wo_ref[RET_WIDTH:, :])
    h = x_ref[...] + _rms(mix, l1_ref[...])
    u = _dot(_rms(h, l2_ref[...]).astype(BF16), wu_ref[...])
    a = jnp.square(jnp.maximum(u, 0.0)).astype(BF16)
    f = _dot(a, wd_ref[...])
    o_ref[...] = h + _rms(f, l3_ref[...])


def _finish(x2d, ret_y, nsa_y, w_o, l1, l2, l3, w_up, w_down, tile, ret_t=False):
    rows = x2d.shape[0]
    row = lambda w: pl.BlockSpec((tile, w), lambda i: (i, 0))
    const = lambda a, b: pl.BlockSpec((a, b), lambda i: (0, 0))
    ret_spec = pl.BlockSpec((RET_WIDTH, tile), lambda i: (0, i)) if ret_t else row(RET_WIDTH)
    return pl.pallas_call(
        functools.partial(_finish_kernel, ret_t=ret_t),
        grid=(rows // tile,),
        in_specs=[row(D_MODEL), ret_spec, row(NSA_WIDTH),
                  const(D_MODEL, D_MODEL), const(1, D_MODEL), const(1, D_MODEL), const(1, D_MODEL),
                  const(D_MODEL, D_FF), const(D_FF, D_MODEL)],
        out_specs=row(D_MODEL),
        out_shape=jax.ShapeDtypeStruct((rows, D_MODEL), F32),
        compiler_params=_params("parallel"),
        name="finish",
    )(x2d, ret_y, nsa_y, w_o, l1, l2, l3, w_up, w_down)


def _column(row):
    n = row.shape[1]
    eye = lax.broadcasted_iota(I32, (n, n), 0) == lax.broadcasted_iota(I32, (n, n), 1)
    return jnp.sum(jnp.where(eye, row, 0.0), axis=1, keepdims=True)


def _cmp_sample_kernel(pt_ref, nq_ref, poolk_ref, poolv_ref,
                       posk_ref, w1k_ref, w2k_ref, posv_ref, w1v_ref, w2v_ref,
                       ocmp_ref, idx_ref, kbuf, vbuf, ksem, vsem):
    step = pl.program_id(0)
    n_pages = pt_ref.shape[1]
    nbk = 2 * n_pages
    slot = step % 2

    def fetch(st, sl, start):
        for j in range(CMP_SAMPLES):
            for p in range(n_pages):
                tiles = pl.ds(((sl * CMP_SAMPLES + j) * n_pages + p) * PAGE_TILES, PAGE_TILES)
                page = pt_ref[st * CMP_SAMPLES + j, p]
                for pool, buf, sem in ((poolk_ref, kbuf, ksem), (poolv_ref, vbuf, vsem)):
                    cp = pltpu.make_async_copy(pool.at[page], buf.at[tiles], sem.at[sl])
                    cp.start() if start else cp.wait()

    @pl.when(step == 0)
    def _():
        fetch(0, 0, True)

    @pl.when(step + 1 < pl.num_programs(0))
    def _():
        fetch(step + 1, 1 - slot, True)

    fetch(step, slot, False)

    step_pages = CMP_SAMPLES * n_pages
    base = slot * step_pages * PAGE_TILES
    group_tiles = HEAD_DIM // SUBLANES

    def rows_of(buf):
        cache = {}

        def load(d):
            fg, s = divmod(d, SUBLANES)
            if fg not in cache:
                cache.clear()
                cache[fg] = [jnp.swapaxes(buf[pl.ds(base + g * group_tiles + fg, step_pages, stride=PAGE_TILES)],
                                          0, 1) for g in range(NSA_KV_HEADS)]
            return jnp.concatenate([t[s] for t in cache[fg]], axis=0)
        return load

    kc = _compress(rows_of(kbuf), posk_ref, w1k_ref, w2k_ref).astype(BF16)
    vc = _compress(rows_of(vbuf), posv_ref, w1v_ref, w2v_ref).astype(BF16)

    j_lane = lax.broadcasted_iota(I32, (nbk, nbk), 1)
    j_sub = lax.broadcasted_iota(I32, (nbk, nbk), 0)
    n_lane = 2 * (j_lane % n_pages) + j_lane // n_pages
    n_sub = 2 * (j_sub % n_pages) + j_sub // n_pages
    z = jnp.zeros((1, HEAD_DIM), F32)
    for j in range(CMP_SAMPLES):
        q = nq_ref[j]
        o_rows, idx_rows = [], []
        for g in range(NSA_KV_HEADS):
            blk_rows = slice((g * CMP_SAMPLES + j) * n_pages, (g * CMP_SAMPLES + j + 1) * n_pages)
            pieces = [q[:, (g * NSA_GROUP + r) * HEAD_DIM:(g * NSA_GROUP + r + 1) * HEAD_DIM]
                      for r in range(NSA_GROUP)]
            qh = jnp.concatenate([jnp.concatenate([pc, z], axis=1) for pc in pieces]
                                 + [jnp.concatenate([z, pc], axis=1) for pc in pieces], axis=0)
            s = _dot_nt(qh.astype(BF16), kc[blk_rows])
            m = jnp.max(s, axis=1, keepdims=True)
            m = jnp.maximum(m[:NSA_GROUP], m[NSA_GROUP:])
            e = jnp.exp(s - jnp.concatenate([m, m], axis=0))
            l = jnp.sum(e, axis=1, keepdims=True)
            l = l[:NSA_GROUP] + l[NSA_GROUP:]
            p = e / jnp.concatenate([l, l], axis=0)
            out = _dot(p.astype(BF16), vc[blk_rows])
            o_rows.append(out[:NSA_GROUP, :HEAD_DIM] + out[NSA_GROUP:, HEAD_DIM:])
            imp = jnp.concatenate([jnp.sum(p[:NSA_GROUP], axis=0, keepdims=True),
                                   jnp.sum(p[NSA_GROUP:], axis=0, keepdims=True)], axis=1)
            imp_col = _column(imp)
            ahead = (imp > imp_col) | ((imp == imp_col) & (n_lane < n_sub))
            rank_col = jnp.sum(jnp.where(ahead, 1, 0), axis=1, keepdims=True)
            idx_rows.append(jnp.sum(jnp.where(rank_col == j_lane, n_sub, 0), axis=0, keepdims=True))
        ocmp_ref[j] = jnp.concatenate([o[r:r + 1] for o in o_rows for r in range(NSA_GROUP)], axis=1)
        idx_ref[j] = jnp.concatenate(idx_rows, axis=0)


def _cmp_sample(page_table, nq, pool_k, pool_v, cwk, cwv):
    DB, n_pages = page_table.shape
    nbk = 2 * n_pages
    wspecs = _cmp_weight_specs(lambda s, pt: (0, 0), lambda s, pt: (0, 0, 0))
    any_spec = pl.BlockSpec(memory_space=pl.ANY)
    tiled = lambda t: t.reshape(t.shape[0], PAGE_TILES, SUBLANES, PAGE_SIZE)
    buf_shape = (2 * CMP_SAMPLES * n_pages * PAGE_TILES, SUBLANES, PAGE_SIZE)
    grid_spec = pltpu.PrefetchScalarGridSpec(
        num_scalar_prefetch=1,
        grid=(DB // CMP_SAMPLES,),
        in_specs=[pl.BlockSpec((CMP_SAMPLES, 1, NSA_WIDTH), lambda s, pt: (s, 0, 0)), any_spec, any_spec]
        + wspecs + wspecs,
        out_specs=[pl.BlockSpec((CMP_SAMPLES, 1, NSA_WIDTH), lambda s, pt: (s, 0, 0)),
                   pl.BlockSpec((CMP_SAMPLES, NSA_KV_HEADS, nbk), lambda s, pt: (s, 0, 0))],
        scratch_shapes=[pltpu.VMEM(buf_shape, F32),
                        pltpu.VMEM(buf_shape, F32),
                        pltpu.SemaphoreType.DMA((2,)),
                        pltpu.SemaphoreType.DMA((2,))],
    )
    pool_k, pool_v = tiled(pool_k), tiled(pool_v)
    return pl.pallas_call(
        _cmp_sample_kernel,
        grid_spec=grid_spec,
        out_shape=[jax.ShapeDtypeStruct((DB, 1, NSA_WIDTH), F32),
                   jax.ShapeDtypeStruct((DB, NSA_KV_HEADS, nbk), I32)],
        compiler_params=_params("arbitrary"),
        name="cmp_sample",
    )(page_table, nq.reshape(DB, 1, NSA_WIDTH), pool_k, pool_v, *cwk, *cwv)


def _att_sample_kernel(idx_ref, pt_ref, nq_ref, knew_ref, vnew_ref, wknew_ref, wvnew_ref,
                       gt_ref, ocmp_ref, poolk_ref, poolv_ref, wkT_ref, wvT_ref,
                       y_ref, wk_out_ref, wv_out_ref, kbuf, vbuf, ksem, vsem, *, past_len):
    b = pl.program_id(0)
    slot = b % 2

    def gather(bb, sl, start):
        for g in range(NSA_KV_HEADS):
            for s in range(N_PICK):
                blk = idx_ref[(bb * NSA_KV_HEADS + g) * N_SELECT + s]
                page = pt_ref[bb, blk // 2]
                for pool, buf, sem in ((poolk_ref, kbuf, ksem), (poolv_ref, vbuf, vsem)):
                    cp = pltpu.make_async_copy(pool.at[page, pl.ds(g * HEAD_DIM, HEAD_DIM), :],
                                               buf.at[sl, g, :, pl.ds(s * PAGE_SIZE, PAGE_SIZE)], sem.at[sl])
                    cp.start() if start else cp.wait()

    @pl.when(b == 0)
    def _():
        gather(0, 0, True)

    @pl.when(b + 1 < pl.num_programs(0))
    def _():
        gather(b + 1, 1 - slot, True)

    q = nq_ref[0]
    pieces = [q[:, hh * HEAD_DIM:(hh + 1) * HEAD_DIM] for hh in range(NSA_HEADS)]
    z = jnp.zeros((1, HEAD_DIM), F32)
    q_wide = jnp.concatenate([jnp.concatenate([pc, z] if hh < NSA_GROUP else [z, pc], axis=1)
                              for hh, pc in enumerate(pieces)], axis=0)

    def bf_round(x):
        return x.astype(BF16).astype(F32)

    def attend(qf, s_past, vT_past, k_new, v_new):
        s_self = jnp.sum(bf_round(qf) * bf_round(k_new), axis=1, keepdims=True)
        m = jnp.maximum(jnp.max(s_past, axis=-1, keepdims=True), s_self)
        p = jnp.exp(s_past - m)
        p_self = jnp.exp(s_self - m)
        l = jnp.sum(p, axis=-1, keepdims=True) + p_self
        o = _dot_nt(p.astype(BF16), vT_past) + bf_round(p_self) * bf_round(v_new)
        return o / l

    wkT = wkT_ref[0]
    wvT = wvT_ref[0]
    wb = wkT.shape[1]
    j = lax.broadcasted_iota(I32, (NSA_HEADS, wb), 1)
    pos_k = past_len - wb + j
    keep = (past_len - pos_k < WINDOW) & (pos_k >= 0)
    s_win = jnp.where(keep, _dot(q_wide.astype(BF16), wkT.astype(BF16)), NEG)
    o_win = attend(q_wide, s_win, wvT.astype(BF16), wknew_ref[0], wvnew_ref[0])

    lane_w = lax.broadcasted_iota(I32, (KV_WIDTH, wb), 1)
    wk_out_ref[0] = jnp.where(lane_w == wb - 1, _column(wknew_ref[0]), pltpu.roll(wkT, wb - 1, 1))
    wv_out_ref[0] = jnp.where(lane_w == wb - 1, _column(wvnew_ref[0]), pltpu.roll(wvT, wb - 1, 1))

    gather(b, slot, False)
    n_keys = N_PICK * PAGE_SIZE
    half_of_lane = (lax.broadcasted_iota(I32, (1, n_keys), 1) % PAGE_SIZE) // CMP_BLOCK
    zeros4 = jnp.zeros((NSA_GROUP, HEAD_DIM), F32)
    gt = gt_ref[0]
    ocmp = ocmp_ref[0]
    y_pieces = []
    for g in range(NSA_KV_HEADS):
        gs = slice(g * HEAD_DIM, (g + 1) * HEAD_DIM)
        want = jnp.concatenate(
            [jnp.full((1, PAGE_SIZE), idx_ref[(b * NSA_KV_HEADS + g) * N_SELECT + s] % 2, I32)
             for s in range(N_PICK)], axis=1)
        qg = jnp.concatenate(pieces[g * NSA_GROUP:(g + 1) * NSA_GROUP] + [zeros4], axis=0)
        s_sel = jnp.where(half_of_lane == want, _dot(qg.astype(BF16), kbuf[slot, g].astype(BF16)), NEG)
        o_sel = attend(qg, s_sel, vbuf[slot, g].astype(BF16), knew_ref[0][:, gs], vnew_ref[0][:, gs])
        for r in range(NSA_GROUP):
            hh = g * NSA_GROUP + r
            y_pieces.append(gt[:, N_BRANCH * hh:N_BRANCH * hh + 1] * ocmp[:, hh * HEAD_DIM:(hh + 1) * HEAD_DIM]
                            + gt[:, N_BRANCH * hh + 1:N_BRANCH * hh + 2] * o_sel[r:r + 1]
                            + gt[:, N_BRANCH * hh + 2:N_BRANCH * hh + 3] * o_win[hh:hh + 1, gs])
    y_ref[0] = jnp.concatenate(y_pieces, axis=1)


def _att_sample(idx, page_table, nq, sk, sv, wk_new, wv_new, gates, ocmp, pool_k, pool_v, buf_k, buf_v):
    DB, n_pages = page_table.shape
    past_len = n_pages * PAGE_SIZE
    wb = buf_k.shape[2]
    r3 = lambda t: t.reshape(DB, 1, t.shape[-1])
    row = lambda w: pl.BlockSpec((1, 1, w), lambda b, ix, pt: (b, 0, 0))
    any_spec = pl.BlockSpec(memory_space=pl.ANY)
    wspec = pl.BlockSpec((1, KV_WIDTH, wb), lambda b, ix, pt: (b, 0, 0))
    buf_shape = (2, NSA_KV_HEADS, HEAD_DIM, N_PICK * PAGE_SIZE)
    grid_spec = pltpu.PrefetchScalarGridSpec(
        num_scalar_prefetch=2,
        grid=(DB,),
        in_specs=[row(NSA_WIDTH), row(KV_WIDTH), row(KV_WIDTH), row(KV_WIDTH), row(KV_WIDTH),
                  row(GATE_PAD), row(NSA_WIDTH), any_spec, any_spec, wspec, wspec],
        out_specs=[row(NSA_WIDTH), wspec, wspec],
        scratch_shapes=[pltpu.VMEM(buf_shape, F32), pltpu.VMEM(buf_shape, F32),
                        pltpu.SemaphoreType.DMA((2,)), pltpu.SemaphoreType.DMA((2,))],
    )
    return pl.pallas_call(
        functools.partial(_att_sample_kernel, past_len=past_len),
        grid_spec=grid_spec,
        out_shape=[jax.ShapeDtypeStruct((DB, 1, NSA_WIDTH), F32),
                   jax.ShapeDtypeStruct(buf_k.shape, F32),
                   jax.ShapeDtypeStruct(buf_v.shape, F32)],
        compiler_params=_params("arbitrary"),
        name="att_sample",
    )(idx, page_table, r3(nq), r3(sk), r3(sv), r3(wk_new), r3(wv_new), r3(gates), ocmp,
      pool_k, pool_v, buf_k, buf_v)


def _row_tile(rows):
    return 256 if rows % 256 == 0 else 128


def kernel(x_prompt, x_sample, state_ret, cache_cmp_k, cache_cmp_v, cache_sel_k, cache_sel_v,
           cache_win_k, cache_win_v, page_table, ln_pre_mix, w_in, ret_gn_w,
           cmp_pos_k, cmp_w1_k, cmp_w2_k, cmp_pos_v, cmp_w1_v, cmp_w2_v,
           w_o, ln_post_mix, ln_pre_ffn, w_up, w_down, ln_post_ffn):
    B, T, _ = x_prompt.shape
    DB, Tn, _ = x_sample.shape
    n_pages = page_table.shape[1]
    past_len = n_pages * PAGE_SIZE
    wb = cache_win_k.shape[2]
    assert w_in.shape[0] == 1 and Tn == 1
    assert T % ATT_K == 0 and T >= WINDOW + ATT_Q and T // CMP_BLOCK <= MAX_BLOCKS
    assert 2 * n_pages >= N_PICK and DB % CMP_SAMPLES == 0

    w_main = w_in[0, :, :MAIN_WIDTH].astype(BF16)
    w_gate = jnp.pad(w_in[0, :, MAIN_WIDTH:], ((0, 0), (0, GATE_PAD - NSA_HEADS * N_BRANCH))).astype(BF16)
    tail = (w_o[0].astype(BF16), ln_post_mix, ln_pre_ffn, ln_post_ffn, w_up[0].astype(BF16), w_down[0].astype(BF16))

    tile_p = _row_tile(B * T)
    cos_p, sin_p = _rope_tables(jnp.arange(T, dtype=I32))
    xp = x_prompt.reshape(B * T, D_MODEL)
    (rq, rk, rv, rg, nqT, ck, cv, sk, wk, ckT, cvT, skT, svT, wkT, wvT, svTb, wvTb, gtT) = _inproj_prompt(
        xp, ln_pre_mix, w_main, w_gate, cos_p, sin_p, B, T, tile_p)
    b3 = lambda t: t.reshape(B, T, t.shape[-1])
    ret_y, rs_p = _ret_prompt(b3(rq), b3(rk), b3(rv), b3(rg), ret_gn_w)
    cwk = _compress_weights(cmp_pos_k[0], cmp_w1_k[0], cmp_w2_k[0], True)
    cwv = _compress_weights(cmp_pos_v[0], cmp_w1_v[0], cmp_w2_v[0], True)
    ocmpT, biasT = _cmp_prompt(b3(ck), b3(cv), cwk, cwv, nqT, gtT, _row_tile(T))
    nsa_y = _att_prompt(nqT, biasT, b3(sk), svTb, b3(wk), wvTb, gtT, ocmpT)
    y_p = _finish(xp, ret_y.reshape(B * T, RET_WIDTH), nsa_y.reshape(B * T, NSA_WIDTH), *tail, tile_p)
    win_p = min(WINDOW, T)
    kv5 = lambda t: t.reshape(t.shape[0], NSA_KV_HEADS, HEAD_DIM, t.shape[-1]).transpose(0, 3, 1, 2)[None]
    outs_p = (y_p.reshape(B, T, D_MODEL), rs_p[None],
              kv5(ckT), kv5(cvT), kv5(skT), kv5(svT),
              kv5(wkT[:, :, T - win_p:]), kv5(wvT[:, :, T - win_p:]))

    cos_s, sin_s = _rope_tables(jnp.full((DB,), past_len, I32))
    xs = x_sample.reshape(DB, D_MODEL)
    (rqT, rkT, rvT, rgT, nq, ckT, cvT, skT, svT, sk, sv, wk, wv, gt) = _inproj_sample(
        xs, ln_pre_mix, w_main, w_gate, cos_s, sin_s)
    ret_yT, rs_s = _ret_sample(rqT, rkT, rvT, rgT, ret_gn_w, state_ret[0].transpose(1, 2, 3, 0))
    feat_major = lambda t: t[0].transpose(0, 2, 3, 1).reshape(t.shape[1], KV_WIDTH, t.shape[2])
    cwk = _compress_weights(cmp_pos_k[0], cmp_w1_k[0], cmp_w2_k[0], False)
    cwv = _compress_weights(cmp_pos_v[0], cmp_w1_v[0], cmp_w2_v[0], False)
    ocmp, idx = _cmp_sample(page_table, nq, feat_major(cache_cmp_k), feat_major(cache_cmp_v), cwk, cwv)
    idx_flat = idx[:, :, :N_SELECT].reshape(-1)
    y_row, nwk, nwv = _att_sample(idx_flat, page_table, nq, sk, sv, wk, wv, gt, ocmp,
                                  feat_major(cache_sel_k), feat_major(cache_sel_v),
                                  feat_major(cache_win_k), feat_major(cache_win_v))
    y_s = _finish(xs, ret_yT, y_row.reshape(DB, NSA_WIDTH), *tail, DB, ret_t=True)
    new5 = lambda t: t.reshape(NSA_KV_HEADS, HEAD_DIM, DB).transpose(2, 0, 1).reshape(1, DB, 1, NSA_KV_HEADS, HEAD_DIM)
    outs_s = (y_s.reshape(DB, 1, D_MODEL), rs_s.transpose(3, 0, 1, 2)[None],
              new5(ckT), new5(cvT), new5(skT), new5(svT), kv5(nwk), kv5(nwv))

    return (outs_p[0], outs_s[0]) + outs_p[1:] + outs_s[1:]
```

```python
import functools

import numpy as np
import jax
import jax.numpy as jnp
from jax import lax
from jax.experimental import pallas as pl
from jax.experimental.pallas import tpu as pltpu

F32 = jnp.float32
BF16 = jnp.bfloat16
I32 = jnp.int32

D_MODEL = 1024
HEAD_DIM = 64
RET_WIDTH = 512
RET_HEADS = 8
NSA_WIDTH = 512
NSA_HEADS = 8
NSA_KV_HEADS = 2
NSA_GROUP = 4
KV_WIDTH = 128
N_BRANCH = 3
D_FF = 4 * D_MODEL
CMP_BLOCK = 64
N_SELECT = 16
N_PICK = N_SELECT - 1
WINDOW = 512
PAGE_SIZE = 128
RET_CHUNK = 128
ROPE_BASE = 10000.0
EPS = 1e-6
NEG = -1e30
SCALE = HEAD_DIM ** -0.5

MAIN_WIDTH = 4 * RET_WIDTH + NSA_WIDTH + 6 * KV_WIDTH
GATE_PAD = 128
ATT_Q = 128
ATT_K = 256
MAX_BLOCKS = 64
CMP_SAMPLES = 2
SUBLANES = 8
PAGE_TILES = KV_WIDTH // SUBLANES
VMEM_LIMIT = 56 * 1024 * 1024

_LOG_GAMMA = [float(np.log1p(-np.exp2(-5.0 - h))) for h in range(RET_HEADS)]


def _params(*sem):
    return pltpu.CompilerParams(dimension_semantics=sem, vmem_limit_bytes=VMEM_LIMIT)


def _dot(a, b):
    return jnp.dot(a, b, preferred_element_type=F32)


def _dot_nt(a, b):
    return lax.dot_general(a, b, (((1,), (1,)), ((), ())), preferred_element_type=F32)


def _dot_tn(a, b):
    return lax.dot_general(a, b, (((0,), (0,)), ((), ())), preferred_element_type=F32)


def _rms(x, w):
    return x * lax.rsqrt(jnp.mean(x * x, axis=-1, keepdims=True) + EPS) * w


def _projector(x_ref, lnw_ref, w_ref, cos_ref, sin_ref):
    h = _rms(x_ref[...], lnw_ref[...]).astype(BF16)
    rows = h.shape[0]

    def proj(a, b):
        return _dot(h, w_ref[:, a:b])

    cos = cos_ref[...]
    sin = sin_ref[...]
    lane = lax.broadcasted_iota(I32, (rows, RET_WIDTH), 1)
    first_half = (lane % HEAD_DIM) < (HEAD_DIM // 2)

    def rope(y):
        partner = jnp.where(first_half,
                            pltpu.roll(y, RET_WIDTH - HEAD_DIM // 2, 1),
                            pltpu.roll(y, HEAD_DIM // 2, 1))
        return y * cos + partner * sin

    return h, proj, rope


def _rope_tables(pos):
    half = HEAD_DIM // 2
    inv = ROPE_BASE ** (-jnp.arange(half, dtype=F32) / half)
    ang = pos.astype(F32)[:, None] * inv
    cos, sin = jnp.cos(ang), jnp.sin(ang)
    cos_h = jnp.concatenate([cos, cos], -1)
    sin_h = jnp.concatenate([-sin, sin], -1)
    return jnp.tile(cos_h, (1, RET_HEADS)), jnp.tile(sin_h, (1, RET_HEADS))


def _inproj_prompt_kernel(x_ref, lnw_ref, w_ref, wg_ref, cos_ref, sin_ref,
                          rq_ref, rk_ref, rv_ref, rg_ref, nqT_ref, ck_ref, cv_ref, sk_ref, wk_ref,
                          ckT_ref, cvT_ref, skT_ref, svT_ref, wkT_ref, wvT_ref, svTb_ref, wvTb_ref, gtT_ref):
    h, proj, rope = _projector(x_ref, lnw_ref, w_ref, cos_ref, sin_ref)
    rq_ref[...] = rope(proj(0, 512)).astype(BF16)
    rk_ref[...] = (rope(proj(512, 1024)) * SCALE).astype(BF16)
    rv_ref[...] = proj(1024, 1536).astype(BF16)
    rg_ref[...] = proj(1536, 2048)
    for n in range(NSA_WIDTH // KV_WIDTH):
        nq = proj(2048 + n * KV_WIDTH, 2048 + (n + 1) * KV_WIDTH) * SCALE
        nqT_ref[0, n * KV_WIDTH:(n + 1) * KV_WIDTH, :] = nq.T.astype(BF16)
    kv = [proj(2560 + n * KV_WIDTH, 2560 + (n + 1) * KV_WIDTH) for n in range(6)]
    ck_ref[...] = kv[0]
    cv_ref[...] = kv[1]
    sk_ref[...] = kv[2].astype(BF16)
    wk_ref[...] = kv[4].astype(BF16)
    for n, ref in enumerate((ckT_ref, cvT_ref, skT_ref, svT_ref, wkT_ref, wvT_ref)):
        ref[0] = kv[n].T
    svTb_ref[0] = kv[3].T.astype(BF16)
    wvTb_ref[0] = kv[5].T.astype(BF16)
    gtT_ref[0] = jax.nn.sigmoid(_dot(h, wg_ref[...])).T


def _inproj_prompt(x2d, lnw, w_main, w_gate, cos_t, sin_t, B, T, tile):
    rows = x2d.shape[0]
    nt = T // tile
    row = lambda i: (i, 0)
    const = lambda i: (0, 0)
    tab = lambda i: (i % nt, 0)
    colT = lambda i: (i // nt, 0, i % nt)
    rm = lambda w, dt: (pl.BlockSpec((tile, w), row), jax.ShapeDtypeStruct((rows, w), dt))
    tr = lambda w, dt: (pl.BlockSpec((1, w, tile), colT), jax.ShapeDtypeStruct((B, w, T), dt))
    outs = [rm(RET_WIDTH, BF16), rm(RET_WIDTH, BF16), rm(RET_WIDTH, BF16), rm(RET_WIDTH, F32),
            tr(NSA_WIDTH, BF16), rm(KV_WIDTH, F32), rm(KV_WIDTH, F32), rm(KV_WIDTH, BF16), rm(KV_WIDTH, BF16),
            *[tr(KV_WIDTH, F32)] * 6,
            tr(KV_WIDTH, BF16), tr(KV_WIDTH, BF16), tr(GATE_PAD, F32)]
    return pl.pallas_call(
        _inproj_prompt_kernel,
        grid=(rows // tile,),
        in_specs=[pl.BlockSpec((tile, D_MODEL), row),
                  pl.BlockSpec((1, D_MODEL), const),
                  pl.BlockSpec((D_MODEL, MAIN_WIDTH), const),
                  pl.BlockSpec((D_MODEL, GATE_PAD), const),
                  pl.BlockSpec((tile, RET_WIDTH), tab),
                  pl.BlockSpec((tile, RET_WIDTH), tab)],
        out_specs=[o[0] for o in outs],
        out_shape=[o[1] for o in outs],
        compiler_params=_params("parallel"),
        name="inproj_prompt",
    )(x2d, lnw, w_main, w_gate, cos_t, sin_t)


def _inproj_sample_kernel(x_ref, lnw_ref, w_ref, wg_ref, cos_ref, sin_ref,
                          rqT_ref, rkT_ref, rvT_ref, rgT_ref, nq_ref,
                          ckT_ref, cvT_ref, skT_ref, svT_ref, sk_ref, sv_ref, wk_ref, wv_ref, gt_ref):
    h, proj, rope = _projector(x_ref, lnw_ref, w_ref, cos_ref, sin_ref)
    rqT_ref[...] = rope(proj(0, 512)).T
    rkT_ref[...] = (rope(proj(512, 1024)) * SCALE).T
    rvT_ref[...] = proj(1024, 1536).T
    rgT_ref[...] = proj(1536, 2048).T
    nq_ref[...] = proj(2048, 2560) * SCALE
    kv = [proj(2560 + n * KV_WIDTH, 2560 + (n + 1) * KV_WIDTH) for n in range(6)]
    for n, ref in enumerate((ckT_ref, cvT_ref, skT_ref, svT_ref)):
        ref[...] = kv[n].T
    for n, ref in enumerate((sk_ref, sv_ref, wk_ref, wv_ref)):
        ref[...] = kv[2 + n]
    gt_ref[...] = jax.nn.sigmoid(_dot(h, wg_ref[...]))


def _inproj_sample(x2d, lnw, w_main, w_gate, cos_t, sin_t):
    rows = x2d.shape[0]
    full = lambda a, b: pl.BlockSpec((a, b), lambda i: (0, 0))
    rm = lambda w: (full(rows, w), jax.ShapeDtypeStruct((rows, w), F32))
    tr = lambda w: (full(w, rows), jax.ShapeDtypeStruct((w, rows), F32))
    outs = [tr(RET_WIDTH)] * 4 + [rm(NSA_WIDTH)] + [tr(KV_WIDTH)] * 4 + [rm(KV_WIDTH)] * 4 + [rm(GATE_PAD)]
    return pl.pallas_call(
        _inproj_sample_kernel,
        grid=(1,),
        in_specs=[full(rows, D_MODEL), full(1, D_MODEL), full(D_MODEL, MAIN_WIDTH), full(D_MODEL, GATE_PAD),
                  full(rows, RET_WIDTH), full(rows, RET_WIDTH)],
        out_specs=[o[0] for o in outs],
        out_shape=[o[1] for o in outs],
        compiler_params=_params("arbitrary"),
        name="inproj_sample",
    )(x2d, lnw, w_main, w_gate, cos_t, sin_t)


def _group_norm_gate(outs, g, gnw):
    ys = []
    for o in outs:
        mu = jnp.mean(o, axis=-1, keepdims=True)
        d = o - mu
        var = jnp.mean(d * d, axis=-1, keepdims=True)
        ys.append(d * lax.rsqrt(var + EPS))
    y = jnp.concatenate(ys, axis=1) * gnw
    return g * jax.nn.sigmoid(g) * y


def _ret_prompt_kernel(q_ref, k_ref, v_ref, g_ref, gnw_ref, inner_ref, qdec_ref, kdec_ref,
                       y_ref, s_out_ref, s_scr):
    c = pl.program_id(1)

    @pl.when(c == 0)
    def _():
        s_scr[...] = jnp.zeros_like(s_scr)

    q = q_ref[0]
    k = k_ref[0]
    v = v_ref[0]
    kd = (k.astype(F32) * kdec_ref[...]).astype(BF16)
    qdec = qdec_ref[...]
    sls = [slice(h * HEAD_DIM, (h + 1) * HEAD_DIM) for h in range(RET_HEADS)]
    att = [(_dot_nt(q[:, sl], k[:, sl]) * inner_ref[h]).astype(BF16) for h, sl in enumerate(sls)]
    cross = [_dot(q[:, sl], s_scr[h].astype(BF16)) * qdec[:, sl] for h, sl in enumerate(sls)]
    grow = [_dot_tn(kd[:, sl], v[:, sl]) for sl in sls]
    outs = [_dot(att[h], v[:, sl]) + cross[h] for h, sl in enumerate(sls)]
    for h in range(RET_HEADS):
        s_scr[h] = s_scr[h] * float(np.exp(_LOG_GAMMA[h] * RET_CHUNK)) + grow[h]
    y_ref[0] = _group_norm_gate(outs, g_ref[0], gnw_ref[...]).astype(BF16)

    @pl.when(c == pl.num_programs(1) - 1)
    def _():
        s_out_ref[0] = s_scr[...]


def _ret_prompt(rq, rk, rv, rg, gnw):
    B, T, _ = rq.shape
    C = RET_CHUNK
    lg = jnp.asarray(_LOG_GAMMA, F32)
    i = jnp.arange(C, dtype=F32)
    diff = i[:, None] - i[None, :]
    inner = jnp.where(diff >= 0, jnp.exp(lg[:, None, None] * jnp.maximum(diff, 0.0)), 0.0)
    qdec = jnp.repeat(jnp.exp(lg[None, :] * (i[:, None] + 1.0)), HEAD_DIM, axis=1)
    kdec = jnp.repeat(jnp.exp(lg[None, :] * (C - 1.0 - i[:, None])), HEAD_DIM, axis=1)
    blk = pl.BlockSpec((1, C, RET_WIDTH), lambda b, c: (b, c, 0))
    c2 = lambda b, c: (0, 0)
    return pl.pallas_call(
        _ret_prompt_kernel,
        grid=(B, T // C),
        in_specs=[blk, blk, blk, blk,
                  pl.BlockSpec((1, RET_WIDTH), c2),
                  pl.BlockSpec((RET_HEADS, C, C), lambda b, c: (0, 0, 0)),
                  pl.BlockSpec((C, RET_WIDTH), c2),
                  pl.BlockSpec((C, RET_WIDTH), c2)],
        out_specs=[blk, pl.BlockSpec((1, RET_HEADS, HEAD_DIM, HEAD_DIM), lambda b, c: (b, 0, 0, 0))],
        out_shape=[jax.ShapeDtypeStruct((B, T, RET_WIDTH), BF16),
                   jax.ShapeDtypeStruct((B, RET_HEADS, HEAD_DIM, HEAD_DIM), F32)],
        scratch_shapes=[pltpu.VMEM((RET_HEADS, HEAD_DIM, HEAD_DIM), F32)],
        compiler_params=_params("parallel", "arbitrary"),
        name="ret_prompt",
    )(rq, rk, rv, rg, gnw, inner, qdec, kdec)


def _ret_sample_kernel(q_ref, k_ref, v_ref, g_ref, gnw_ref, gam_ref, s_ref, y_ref, s_out_ref):
    q = q_ref[...]
    k = k_ref[...]
    v = v_ref[...]
    g = g_ref[...]
    gamma = gam_ref[0]
    acc = jnp.zeros(v.shape, F32)
    for d in range(HEAD_DIM):
        s_d = s_ref[0, d]
        acc = acc + s_d * q[d:d + 1, :]
        s_out_ref[0, d] = s_d * gamma + k[d:d + 1, :] * v
    o = acc * gamma + jnp.sum(q * k, axis=0, keepdims=True) * v
    mu = jnp.mean(o, axis=0, keepdims=True)
    dlt = o - mu
    var = jnp.mean(dlt * dlt, axis=0, keepdims=True)
    y_ref[...] = g * jax.nn.sigmoid(g) * (dlt * lax.rsqrt(var + EPS) * gnw_ref[...])


def _ret_sample(rqT, rkT, rvT, rgT, gnw, state_t):
    DB = rqT.shape[1]
    blk = pl.BlockSpec((HEAD_DIM, DB), lambda h: (h, 0))
    st = pl.BlockSpec((1, HEAD_DIM, HEAD_DIM, DB), lambda h: (h, 0, 0, 0))
    gam = jnp.broadcast_to(jnp.exp(jnp.asarray(_LOG_GAMMA, F32))[:, None, None], (RET_HEADS, 1, DB))
    return pl.pallas_call(
        _ret_sample_kernel,
        grid=(RET_HEADS,),
        in_specs=[blk, blk, blk, blk,
                  pl.BlockSpec((HEAD_DIM, 1), lambda h: (h, 0)),
                  pl.BlockSpec((1, 1, DB), lambda h: (h, 0, 0)),
                  st],
        out_specs=[blk, st],
        out_shape=[jax.ShapeDtypeStruct((RET_WIDTH, DB), F32),
                   jax.ShapeDtypeStruct(state_t.shape, F32)],
        compiler_params=_params("parallel"),
        name="ret_sample",
    )(rqT, rkT, rvT, rgT, gnw.reshape(RET_WIDTH, 1), gam, state_t)


def _compress(load_rows, pos_ref, w1_ref, w2_ref):
    acc = None
    for j2 in range(CMP_BLOCK // 2):
        x = jnp.concatenate([load_rows(2 * j2) + pos_ref[2 * j2:2 * j2 + 1, :],
                             load_rows(2 * j2 + 1) + pos_ref[2 * j2 + 1:2 * j2 + 2, :]], axis=1).astype(BF16)
        part = _dot(x, w1_ref[j2])
        acc = part if acc is None else acc + part
    h = jax.nn.gelu(acc)
    return _dot(h.astype(BF16), w2_ref[...])


def _block_diag2(w):
    z = jnp.zeros_like(w)
    return jnp.concatenate([jnp.concatenate([w, z], -1), jnp.concatenate([z, w], -1)], -2)


def _compress_weights(pos, w1, w2, token_major):
    if not token_major:
        pos, w1 = pos.T, w1.transpose(1, 0, 2)
    pos2 = jnp.concatenate([pos, pos], -1)
    w1p = _block_diag2(w1).reshape(CMP_BLOCK // 2, 2 * KV_WIDTH, KV_WIDTH)
    return pos2, w1p.astype(BF16), _block_diag2(w2).astype(BF16)


def _rank_rows(imp, rank_scr, last_live):
    nb, tq = imp.shape
    sub = SUBLANES
    n_groups = nb // sub
    groups = [imp[sub * j:sub * (j + 1)] for j in range(n_groups)]
    row_in_group = lax.broadcasted_iota(I32, (sub, tq), 0)
    rank_scr[...] = jnp.zeros((nb, tq), I32)
    for jm in range(n_groups):
        @pl.when(jm * sub <= last_live)
        def _():
            ranks = [rank_scr[sub * j:sub * (j + 1), :] for j in range(n_groups)]
            for sm in range(sub):
                row = imp[jm * sub + sm:jm * sub + sm + 1, :]
                for j, grp in enumerate(groups):
                    if j < jm:
                        inc = jnp.where(row > grp, 1, 0)
                    elif j > jm:
                        inc = jnp.where(row >= grp, 1, 0)
                    else:
                        inc = jnp.where(row_in_group > sm, jnp.where(row >= grp, 1, 0),
                                        jnp.where(row > grp, 1, 0))
                    ranks[j] = ranks[j] + inc
            for j in range(n_groups):
                rank_scr[sub * j:sub * (j + 1), :] = ranks[j]
    return rank_scr[...]


def _kv_head_rows(x, g):
    z = jnp.zeros_like(x)
    return jnp.concatenate([x, z] if g == 0 else [z, x], axis=0)


def _cmp_prompt_kernel(ck_ref, cv_ref, posk_ref, w1k_ref, w2k_ref, posv_ref, w1v_ref, w2v_ref,
                       nqT_ref, gtT_ref, ocmpT_ref, biasT_ref, kc_scr, vc_scr, rank_scr):
    i = pl.program_id(1)
    nb = kc_scr.shape[0]
    tq = nqT_ref.shape[2]

    @pl.when(i == 0)
    def _():
        kc_scr[...] = _compress(lambda l: ck_ref[0, pl.ds(l, nb, stride=CMP_BLOCK), :],
                                posk_ref, w1k_ref, w2k_ref).astype(BF16)
        vc_scr[...] = _compress(lambda l: cv_ref[0, pl.ds(l, nb, stride=CMP_BLOCK), :],
                                posv_ref, w1v_ref, w2v_ref).astype(BF16)

    kc = kc_scr[...]
    vc = vc_scr[...]
    pos = i * tq + lax.broadcasted_iota(I32, (nb, tq), 1)
    n_idx = lax.broadcasted_iota(I32, (nb, tq), 0)
    valid = (n_idx + 1) * CMP_BLOCK - 1 <= pos
    cur = pos // CMP_BLOCK
    cand = n_idx < cur
    for g in range(NSA_KV_HEADS):
        imp = jnp.zeros((nb, tq), F32)
        for r in range(NSA_GROUP):
            hh = g * NSA_GROUP + r
            qT = _kv_head_rows(nqT_ref[0, hh * HEAD_DIM:(hh + 1) * HEAD_DIM, :], g)
            s = jnp.where(valid, _dot(kc, qT), NEG)
            e = jnp.exp(s - jnp.max(s, axis=0, keepdims=True))
            p = jnp.where(valid, e / jnp.sum(e, axis=0, keepdims=True), 0.0)
            imp = imp + p
            oT = _dot_tn(vc, p.astype(BF16))
            ocmpT_ref[0, hh * HEAD_DIM:(hh + 1) * HEAD_DIM, :] = (
                oT[g * HEAD_DIM:(g + 1) * HEAD_DIM] * gtT_ref[0, N_BRANCH * hh:N_BRANCH * hh + 1, :])
        imp = jnp.where(cand, imp, -1.0)
        rank = _rank_rows(imp, rank_scr, (i * tq + tq - 1) // CMP_BLOCK)
        sel = (cand & (rank < N_PICK)) | (n_idx == cur)
        biasT_ref[0, g * MAX_BLOCKS:g * MAX_BLOCKS + nb, :] = jnp.where(sel, 0.0, NEG).astype(BF16)
        if nb < MAX_BLOCKS:
            biasT_ref[0, g * MAX_BLOCKS + nb:(g + 1) * MAX_BLOCKS, :] = jnp.zeros((MAX_BLOCKS - nb, tq), BF16)


def _cmp_weight_specs(c2, c3):
    return [pl.BlockSpec((CMP_BLOCK, KV_WIDTH), c2),
            pl.BlockSpec((CMP_BLOCK // 2, 2 * KV_WIDTH, KV_WIDTH), c3),
            pl.BlockSpec((KV_WIDTH, KV_WIDTH), c2)]


def _cmp_prompt(ck, cv, cwk, cwv, nqT, gtT, tq):
    B, T, _ = ck.shape
    nb = T // CMP_BLOCK
    full = pl.BlockSpec((1, T, KV_WIDTH), lambda b, i: (b, 0, 0))
    wspecs = _cmp_weight_specs(lambda b, i: (0, 0), lambda b, i: (0, 0, 0))
    blk = lambda w: pl.BlockSpec((1, w, tq), lambda b, i: (b, 0, i))
    return pl.pallas_call(
        _cmp_prompt_kernel,
        grid=(B, T // tq),
        in_specs=[full, full] + wspecs + wspecs + [blk(NSA_WIDTH), blk(GATE_PAD)],
        out_specs=[blk(NSA_WIDTH), blk(NSA_KV_HEADS * MAX_BLOCKS)],
        out_shape=[jax.ShapeDtypeStruct((B, NSA_WIDTH, T), F32),
                   jax.ShapeDtypeStruct((B, NSA_KV_HEADS * MAX_BLOCKS, T), BF16)],
        scratch_shapes=[pltpu.VMEM((nb, KV_WIDTH), BF16), pltpu.VMEM((nb, KV_WIDTH), BF16),
                        pltpu.VMEM((nb, tq), I32)],
        compiler_params=_params("parallel", "arbitrary"),
        name="cmp_prompt",
    )(ck, cv, *cwk, *cwv, nqT, gtT)


def _att_prompt_kernel(nqT_ref, biasT_ref, sk_ref, svT_ref, wk_ref, wvT_ref, gtT_ref, ocmpT_ref,
                       y_ref, ka_scr, va_scr, s_scr, m_scr, acc_scr):
    i = pl.program_id(1)
    tq = nqT_ref.shape[2]
    T = sk_ref.shape[1]
    cols = NSA_GROUP * tq

    @pl.when(i == 0)
    def _():
        sk = sk_ref[0].astype(F32)
        lane = lax.broadcasted_iota(I32, (T, KV_WIDTH), 1)
        blk = lax.broadcasted_iota(I32, (T, KV_WIDTH), 0) // CMP_BLOCK
        ka_scr[0] = jnp.where(lane < HEAD_DIM, sk, jnp.where(lane - HEAD_DIM == blk, 1.0, 0.0)).astype(BF16)
        ka_scr[1] = jnp.where(lane >= HEAD_DIM, sk, jnp.where(lane == blk, 1.0, 0.0)).astype(BF16)
        svT = svT_ref[0].astype(F32)
        row = lax.broadcasted_iota(I32, (KV_WIDTH, T), 0)
        va_scr[0] = jnp.where(row < HEAD_DIM, svT, 1.0).astype(BF16)
        va_scr[1] = jnp.where(row >= HEAD_DIM, svT, 1.0).astype(BF16)

    q_pos = i * tq + lax.broadcasted_iota(I32, (ATT_K, cols), 1) % tq
    k_off = lax.broadcasted_iota(I32, (ATT_K, cols), 0)
    n_win = WINDOW // tq
    win_keys = (n_win + 1) * tq
    win_start = pl.multiple_of(jnp.maximum(i - n_win, 0) * tq, tq)
    rel = i * tq - win_start + lax.broadcasted_iota(I32, (win_keys, tq), 1)
    j_win = lax.broadcasted_iota(I32, (win_keys, tq), 0)
    win_bias = jnp.where((j_win <= rel) & (j_win > rel - WINDOW), 0.0, NEG)
    win_bias = jnp.concatenate([win_bias] * NSA_GROUP, axis=1)
    wk = wk_ref[0, pl.ds(win_start, win_keys), :]
    wvT = wvT_ref[0, :, pl.ds(win_start, win_keys)]

    qaTs, qwTs = [], []
    for g in range(NSA_KV_HEADS):
        qTs = [nqT_ref[0, (g * NSA_GROUP + r) * HEAD_DIM:(g * NSA_GROUP + r + 1) * HEAD_DIM, :]
               for r in range(NSA_GROUP)]
        bT = biasT_ref[0, g * MAX_BLOCKS:(g + 1) * MAX_BLOCKS, :]
        qaTs.append(jnp.concatenate([jnp.concatenate([q, bT] if g == 0 else [bT, q], axis=0) for q in qTs],
                                    axis=1))
        qwTs.append(jnp.concatenate([_kv_head_rows(q, g) for q in qTs], axis=1))

    def scores_into(slot, c):
        start = pl.multiple_of(c * ATT_K, ATT_K)
        for g in range(NSA_KV_HEADS):
            s_scr[slot, g] = _dot(ka_scr[g, pl.ds(start, ATT_K), :], qaTs[g])

    def sel_step(c, slot, last):
        start = pl.multiple_of(c * ATT_K, ATT_K)
        if not last:
            scores_into(1 - slot, c + 1)
        for g in range(NSA_KV_HEADS):
            s = s_scr[slot, g]
            if last:
                s = jnp.where(start + k_off > q_pos, NEG, s)
            m = m_scr[g]
            m_new = jnp.maximum(m, jnp.max(s, axis=0, keepdims=True))
            m_scr[g] = m_new
            p = jnp.exp(s - m_new)
            pv = _dot(va_scr[g, :, pl.ds(start, ATT_K)], p.astype(BF16))
            acc_scr[g] = jnp.exp(m - m_new) * acc_scr[g] + pv

    m_scr[...] = jnp.full(m_scr.shape, NEG, F32)
    acc_scr[...] = jnp.zeros(acc_scr.shape, F32)
    scores_into(0, 0)
    n_before = (i * tq) // ATT_K

    def pair(pr, carry):
        sel_step(2 * pr, 0, False)
        sel_step(2 * pr + 1, 1, False)
        return carry

    lax.fori_loop(0, n_before // 2, pair, 0)

    @pl.when(n_before % 2 == 1)
    def _():
        sel_step(n_before - 1, 0, False)

    sel_step(n_before, n_before % 2, True)

    heads_out = []
    for g in range(NSA_KV_HEADS):
        gs = slice(g * HEAD_DIM, (g + 1) * HEAD_DIM)
        acc = acc_scr[g]
        ones_row = (1 - g) * HEAD_DIM
        o_sel = acc[gs] / acc[ones_row:ones_row + 1]

        s = _dot(wk, qwTs[g]) + win_bias
        p = jnp.exp(s - jnp.max(s, axis=0, keepdims=True))
        o_win = _dot(wvT, p.astype(BF16))[gs] / jnp.sum(p, axis=0, keepdims=True)

        for r in range(NSA_GROUP):
            hh = g * NSA_GROUP + r
            cs = slice(r * tq, (r + 1) * tq)
            heads_out.append(ocmpT_ref[0, hh * HEAD_DIM:(hh + 1) * HEAD_DIM, :]
                             + gtT_ref[0, N_BRANCH * hh + 1:N_BRANCH * hh + 2, :] * o_sel[:, cs]
                             + gtT_ref[0, N_BRANCH * hh + 2:N_BRANCH * hh + 3, :] * o_win[:, cs])
    y_ref[0] = jnp.concatenate(heads_out, axis=0).T.astype(BF16)


def _att_prompt(nqT, biasT, sk, svT, wk, wvT, gtT, ocmpT):
    B, _, T = nqT.shape
    tq = ATT_Q
    rows_full = pl.BlockSpec((1, T, KV_WIDTH), lambda b, i: (b, 0, 0))
    cols_full = pl.BlockSpec((1, KV_WIDTH, T), lambda b, i: (b, 0, 0))
    blk = lambda w: pl.BlockSpec((1, w, tq), lambda b, i: (b, 0, i))
    return pl.pallas_call(
        _att_prompt_kernel,
        grid=(B, T // tq),
        in_specs=[blk(NSA_WIDTH), blk(NSA_KV_HEADS * MAX_BLOCKS), rows_full, cols_full, rows_full, cols_full,
                  blk(GATE_PAD), blk(NSA_WIDTH)],
        out_specs=pl.BlockSpec((1, tq, NSA_WIDTH), lambda b, i: (b, i, 0)),
        out_shape=jax.ShapeDtypeStruct((B, T, NSA_WIDTH), BF16),
        scratch_shapes=[pltpu.VMEM((NSA_KV_HEADS, T, KV_WIDTH), BF16),
                        pltpu.VMEM((NSA_KV_HEADS, KV_WIDTH, T), BF16),
                        pltpu.VMEM((2, NSA_KV_HEADS, ATT_K, NSA_GROUP * tq), F32),
                        pltpu.VMEM((NSA_KV_HEADS, 1, NSA_GROUP * tq), F32),
                        pltpu.VMEM((NSA_KV_HEADS, KV_WIDTH, NSA_GROUP * tq), F32)],
        compiler_params=_params("parallel", "arbitrary"),
        name="att_prompt",
    )(nqT, biasT, sk, svT, wk, wvT, gtT, ocmpT)


def _finish_kernel(x_ref, ry_ref, ny_ref, wo_ref, l1_ref, l2_ref, l3_ref, wu_ref, wd_ref, o_ref, *, ret_t):
    if ret_t:
        mix = _dot_tn(ry_ref[...].astype(BF16), wo_ref[0:RET_WIDTH, :])
    else:
        mix = _dot(ry_ref[...], wo_ref[0:RET_WIDTH, :])
    mix = mix + _dot(ny_ref[...].astype(BF16), wo_ref[RET_WIDTH:, :])
    h = x_ref[...] + _rms(mix, l1_ref[...])
    u = _dot(_rms(h, l2_ref[...]).astype(BF16), wu_ref[...])
    a = jnp.square(jnp.maximum(u, 0.0)).astype(BF16)
    f = _dot(a, wd_ref[...])
    o_ref[...] = h + _rms(f, l3_ref[...])


def _finish(x2d, ret_y, nsa_y, w_o, l1, l2, l3, w_up, w_down, tile, ret_t=False):
    rows = x2d.shape[0]
    row = lambda w: pl.BlockSpec((tile, w), lambda i: (i, 0))
    const = lambda a, b: pl.BlockSpec((a, b), lambda i: (0, 0), pipeline_mode=pl.Buffered(1))
    ret_spec = pl.BlockSpec((RET_WIDTH, tile), lambda i: (0, i)) if ret_t else row(RET_WIDTH)
    return pl.pallas_call(
        functools.partial(_finish_kernel, ret_t=ret_t),
        grid=(rows // tile,),
        in_specs=[row(D_MODEL), ret_spec, row(NSA_WIDTH),
                  const(D_MODEL, D_MODEL), const(1, D_MODEL), const(1, D_MODEL), const(1, D_MODEL),
                  const(D_MODEL, D_FF), const(D_FF, D_MODEL)],
        out_specs=row(D_MODEL),
        out_shape=jax.ShapeDtypeStruct((rows, D_MODEL), F32),
        compiler_params=_params("parallel"),
        name="finish",
    )(x2d, ret_y, nsa_y, w_o, l1, l2, l3, w_up, w_down)


def _column(row):
    n = row.shape[1]
    eye = lax.broadcasted_iota(I32, (n, n), 0) == lax.broadcasted_iota(I32, (n, n), 1)
    return jnp.sum(jnp.where(eye, row, 0.0), axis=1, keepdims=True)


def _cmp_sample_kernel(pt_ref, nq_ref, poolk_ref, poolv_ref,
                       posk_ref, w1k_ref, w2k_ref, posv_ref, w1v_ref, w2v_ref,
                       ocmp_ref, idx_ref, kbuf, vbuf, ksem, vsem):
    step = pl.program_id(0)
    n_pages = pt_ref.shape[1]
    nbk = 2 * n_pages
    slot = step % 2

    def fetch(st, sl, start):
        for j in range(CMP_SAMPLES):
            for p in range(n_pages):
                tiles = pl.ds(((sl * CMP_SAMPLES + j) * n_pages + p) * PAGE_TILES, PAGE_TILES)
                page = pt_ref[st * CMP_SAMPLES + j, p]
                for pool, buf, sem in ((poolk_ref, kbuf, ksem), (poolv_ref, vbuf, vsem)):
                    cp = pltpu.make_async_copy(pool.at[page], buf.at[tiles], sem.at[sl])
                    cp.start() if start else cp.wait()

    @pl.when(step == 0)
    def _():
        fetch(0, 0, True)

    @pl.when(step + 1 < pl.num_programs(0))
    def _():
        fetch(step + 1, 1 - slot, True)

    fetch(step, slot, False)

    step_pages = CMP_SAMPLES * n_pages
    base = slot * step_pages * PAGE_TILES
    group_tiles = HEAD_DIM // SUBLANES

    def rows_of(buf):
        cache = {}

        def load(d):
            fg, s = divmod(d, SUBLANES)
            if fg not in cache:
                cache.clear()
                cache[fg] = [jnp.swapaxes(buf[pl.ds(base + g * group_tiles + fg, step_pages, stride=PAGE_TILES)],
                                          0, 1) for g in range(NSA_KV_HEADS)]
            return jnp.concatenate([t[s] for t in cache[fg]], axis=0)
        return load

    kc = _compress(rows_of(kbuf), posk_ref, w1k_ref, w2k_ref).astype(BF16)
    vc = _compress(rows_of(vbuf), posv_ref, w1v_ref, w2v_ref).astype(BF16)

    j_lane = lax.broadcasted_iota(I32, (nbk, nbk), 1)
    j_sub = lax.broadcasted_iota(I32, (nbk, nbk), 0)
    n_lane = 2 * (j_lane % n_pages) + j_lane // n_pages
    n_sub = 2 * (j_sub % n_pages) + j_sub // n_pages
    z = jnp.zeros((1, HEAD_DIM), F32)
    for j in range(CMP_SAMPLES):
        q = nq_ref[j]
        o_rows, idx_rows = [], []
        for g in range(NSA_KV_HEADS):
            blk_rows = slice((g * CMP_SAMPLES + j) * n_pages, (g * CMP_SAMPLES + j + 1) * n_pages)
            pieces = [q[:, (g * NSA_GROUP + r) * HEAD_DIM:(g * NSA_GROUP + r + 1) * HEAD_DIM]
                      for r in range(NSA_GROUP)]
            qh = jnp.concatenate([jnp.concatenate([pc, z], axis=1) for pc in pieces]
                                 + [jnp.concatenate([z, pc], axis=1) for pc in pieces], axis=0)
            s = _dot_nt(qh.astype(BF16), kc[blk_rows])
            m = jnp.max(s, axis=1, keepdims=True)
            m = jnp.maximum(m[:NSA_GROUP], m[NSA_GROUP:])
            e = jnp.exp(s - jnp.concatenate([m, m], axis=0))
            l = jnp.sum(e, axis=1, keepdims=True)
            l = l[:NSA_GROUP] + l[NSA_GROUP:]
            p = e / jnp.concatenate([l, l], axis=0)
            out = _dot(p.astype(BF16), vc[blk_rows])
            o_rows.append(out[:NSA_GROUP, :HEAD_DIM] + out[NSA_GROUP:, HEAD_DIM:])
            imp = jnp.concatenate([jnp.sum(p[:NSA_GROUP], axis=0, keepdims=True),
                                   jnp.sum(p[NSA_GROUP:], axis=0, keepdims=True)], axis=1)
            imp_col = _column(imp)
            ahead = (imp > imp_col) | ((imp == imp_col) & (n_lane < n_sub))
            rank_col = jnp.sum(jnp.where(ahead, 1, 0), axis=1, keepdims=True)
            idx_rows.append(jnp.sum(jnp.where(rank_col == j_lane, n_sub, 0), axis=0, keepdims=True))
        ocmp_ref[j] = jnp.concatenate([o[r:r + 1] for o in o_rows for r in range(NSA_GROUP)], axis=1)
        idx_ref[j] = jnp.concatenate(idx_rows, axis=0)


def _cmp_sample(page_table, nq, pool_k, pool_v, cwk, cwv):
    DB, n_pages = page_table.shape
    nbk = 2 * n_pages
    wspecs = _cmp_weight_specs(lambda s, pt: (0, 0), lambda s, pt: (0, 0, 0))
    any_spec = pl.BlockSpec(memory_space=pl.ANY)
    tiled = lambda t: t.reshape(t.shape[0], PAGE_TILES, SUBLANES, PAGE_SIZE)
    buf_shape = (2 * CMP_SAMPLES * n_pages * PAGE_TILES, SUBLANES, PAGE_SIZE)
    grid_spec = pltpu.PrefetchScalarGridSpec(
        num_scalar_prefetch=1,
        grid=(DB // CMP_SAMPLES,),
        in_specs=[pl.BlockSpec((CMP_SAMPLES, 1, NSA_WIDTH), lambda s, pt: (s, 0, 0)), any_spec, any_spec]
        + wspecs + wspecs,
        out_specs=[pl.BlockSpec((CMP_SAMPLES, 1, NSA_WIDTH), lambda s, pt: (s, 0, 0)),
                   pl.BlockSpec((CMP_SAMPLES, NSA_KV_HEADS, nbk), lambda s, pt: (s, 0, 0))],
        scratch_shapes=[pltpu.VMEM(buf_shape, F32),
                        pltpu.VMEM(buf_shape, F32),
                        pltpu.SemaphoreType.DMA((2,)),
                        pltpu.SemaphoreType.DMA((2,))],
    )
    pool_k, pool_v = tiled(pool_k), tiled(pool_v)
    return pl.pallas_call(
        _cmp_sample_kernel,
        grid_spec=grid_spec,
        out_shape=[jax.ShapeDtypeStruct((DB, 1, NSA_WIDTH), F32),
                   jax.ShapeDtypeStruct((DB, NSA_KV_HEADS, nbk), I32)],
        compiler_params=_params("arbitrary"),
        name="cmp_sample",
    )(page_table, nq.reshape(DB, 1, NSA_WIDTH), pool_k, pool_v, *cwk, *cwv)


def _att_sample_kernel(idx_ref, pt_ref, nq_ref, knew_ref, vnew_ref, wknew_ref, wvnew_ref,
                       gt_ref, ocmp_ref, poolk_ref, poolv_ref, wkT_ref, wvT_ref,
                       y_ref, wk_out_ref, wv_out_ref, kbuf, vbuf, ksem, vsem, *, past_len):
    b = pl.program_id(0)
    slot = b % 2

    def gather(bb, sl, start):
        for g in range(NSA_KV_HEADS):
            for s in range(N_PICK):
                blk = idx_ref[(bb * NSA_KV_HEADS + g) * N_SELECT + s]
                page = pt_ref[bb, blk // 2]
                for pool, buf, sem in ((poolk_ref, kbuf, ksem), (poolv_ref, vbuf, vsem)):
                    cp = pltpu.make_async_copy(pool.at[page, pl.ds(g * HEAD_DIM, HEAD_DIM), :],
                                               buf.at[sl, g, :, pl.ds(s * PAGE_SIZE, PAGE_SIZE)], sem.at[sl])
                    cp.start() if start else cp.wait()

    @pl.when(b == 0)
    def _():
        gather(0, 0, True)

    @pl.when(b + 1 < pl.num_programs(0))
    def _():
        gather(b + 1, 1 - slot, True)

    q = nq_ref[0]
    pieces = [q[:, hh * HEAD_DIM:(hh + 1) * HEAD_DIM] for hh in range(NSA_HEADS)]
    z = jnp.zeros((1, HEAD_DIM), F32)
    q_wide = jnp.concatenate([jnp.concatenate([pc, z] if hh < NSA_GROUP else [z, pc], axis=1)
                              for hh, pc in enumerate(pieces)], axis=0)

    def bf_round(x):
        return x.astype(BF16).astype(F32)

    def attend(qf, s_past, vT_past, k_new, v_new):
        s_self = jnp.sum(bf_round(qf) * bf_round(k_new), axis=1, keepdims=True)
        m = jnp.maximum(jnp.max(s_past, axis=-1, keepdims=True), s_self)
        p = jnp.exp(s_past - m)
        p_self = jnp.exp(s_self - m)
        l = jnp.sum(p, axis=-1, keepdims=True) + p_self
        o = _dot_nt(p.astype(BF16), vT_past) + bf_round(p_self) * bf_round(v_new)
        return o / l

    wkT = wkT_ref[0]
    wvT = wvT_ref[0]
    wb = wkT.shape[1]
    j = lax.broadcasted_iota(I32, (NSA_HEADS, wb), 1)
    pos_k = past_len - wb + j
    keep = (past_len - pos_k < WINDOW) & (pos_k >= 0)
    s_win = jnp.where(keep, _dot(q_wide.astype(BF16), wkT.astype(BF16)), NEG)
    o_win = attend(q_wide, s_win, wvT.astype(BF16), wknew_ref[0], wvnew_ref[0])

    lane_w = lax.broadcasted_iota(I32, (KV_WIDTH, wb), 1)
    wk_out_ref[0] = jnp.where(lane_w == wb - 1, _column(wknew_ref[0]), pltpu.roll(wkT, wb - 1, 1))
    wv_out_ref[0] = jnp.where(lane_w == wb - 1, _column(wvnew_ref[0]), pltpu.roll(wvT, wb - 1, 1))

    gather(b, slot, False)
    n_keys = N_PICK * PAGE_SIZE
    half_of_lane = (lax.broadcasted_iota(I32, (1, n_keys), 1) % PAGE_SIZE) // CMP_BLOCK
    zeros4 = jnp.zeros((NSA_GROUP, HEAD_DIM), F32)
    gt = gt_ref[0]
    ocmp = ocmp_ref[0]
    y_pieces = []
    for g in range(NSA_KV_HEADS):
        gs = slice(g * HEAD_DIM, (g + 1) * HEAD_DIM)
        want = jnp.concatenate(
            [jnp.full((1, PAGE_SIZE), idx_ref[(b * NSA_KV_HEADS + g) * N_SELECT + s] % 2, I32)
             for s in range(N_PICK)], axis=1)
        qg = jnp.concatenate(pieces[g * NSA_GROUP:(g + 1) * NSA_GROUP] + [zeros4], axis=0)
        s_sel = jnp.where(half_of_lane == want, _dot(qg.astype(BF16), kbuf[slot, g].astype(BF16)), NEG)
        o_sel = attend(qg, s_sel, vbuf[slot, g].astype(BF16), knew_ref[0][:, gs], vnew_ref[0][:, gs])
        for r in range(NSA_GROUP):
            hh = g * NSA_GROUP + r
            y_pieces.append(gt[:, N_BRANCH * hh:N_BRANCH * hh + 1] * ocmp[:, hh * HEAD_DIM:(hh + 1) * HEAD_DIM]
                            + gt[:, N_BRANCH * hh + 1:N_BRANCH * hh + 2] * o_sel[r:r + 1]
                            + gt[:, N_BRANCH * hh + 2:N_BRANCH * hh + 3] * o_win[hh:hh + 1, gs])
    y_ref[0] = jnp.concatenate(y_pieces, axis=1)


def _att_sample(idx, page_table, nq, sk, sv, wk_new, wv_new, gates, ocmp, pool_k, pool_v, buf_k, buf_v):
    DB, n_pages = page_table.shape
    past_len = n_pages * PAGE_SIZE
    wb = buf_k.shape[2]
    r3 = lambda t: t.reshape(DB, 1, t.shape[-1])
    row = lambda w: pl.BlockSpec((1, 1, w), lambda b, ix, pt: (b, 0, 0))
    any_spec = pl.BlockSpec(memory_space=pl.ANY)
    wspec = pl.BlockSpec((1, KV_WIDTH, wb), lambda b, ix, pt: (b, 0, 0))
    buf_shape = (2, NSA_KV_HEADS, HEAD_DIM, N_PICK * PAGE_SIZE)
    grid_spec = pltpu.PrefetchScalarGridSpec(
        num_scalar_prefetch=2,
        grid=(DB,),
        in_specs=[row(NSA_WIDTH), row(KV_WIDTH), row(KV_WIDTH), row(KV_WIDTH), row(KV_WIDTH),
                  row(GATE_PAD), row(NSA_WIDTH), any_spec, any_spec, wspec, wspec],
        out_specs=[row(NSA_WIDTH), wspec, wspec],
        scratch_shapes=[pltpu.VMEM(buf_shape, F32), pltpu.VMEM(buf_shape, F32),
                        pltpu.SemaphoreType.DMA((2,)), pltpu.SemaphoreType.DMA((2,))],
    )
    return pl.pallas_call(
        functools.partial(_att_sample_kernel, past_len=past_len),
        grid_spec=grid_spec,
        out_shape=[jax.ShapeDtypeStruct((DB, 1, NSA_WIDTH), F32),
                   jax.ShapeDtypeStruct(buf_k.shape, F32),
                   jax.ShapeDtypeStruct(buf_v.shape, F32)],
        compiler_params=_params("arbitrary"),
        name="att_sample",
    )(idx, page_table, r3(nq), r3(sk), r3(sv), r3(wk_new), r3(wv_new), r3(gates), ocmp,
      pool_k, pool_v, buf_k, buf_v)


def _row_tile(rows, want=256):
    return next(t for t in (want, 256, 128) if rows % t == 0)


def kernel(x_prompt, x_sample, state_ret, cache_cmp_k, cache_cmp_v, cache_sel_k, cache_sel_v,
           cache_win_k, cache_win_v, page_table, ln_pre_mix, w_in, ret_gn_w,
           cmp_pos_k, cmp_w1_k, cmp_w2_k, cmp_pos_v, cmp_w1_v, cmp_w2_v,
           w_o, ln_post_mix, ln_pre_ffn, w_up, w_down, ln_post_ffn):
    B, T, _ = x_prompt.shape
    DB, Tn, _ = x_sample.shape
    n_pages = page_table.shape[1]
    past_len = n_pages * PAGE_SIZE
    wb = cache_win_k.shape[2]
    assert w_in.shape[0] == 1 and Tn == 1
    assert T % ATT_K == 0 and T >= WINDOW + ATT_Q and T // CMP_BLOCK <= MAX_BLOCKS
    assert 2 * n_pages >= N_PICK and DB % CMP_SAMPLES == 0

    w_main = w_in[0, :, :MAIN_WIDTH].astype(BF16)
    w_gate = jnp.pad(w_in[0, :, MAIN_WIDTH:], ((0, 0), (0, GATE_PAD - NSA_HEADS * N_BRANCH))).astype(BF16)
    tail = (w_o[0].astype(BF16), ln_post_mix, ln_pre_ffn, ln_post_ffn, w_up[0].astype(BF16), w_down[0].astype(BF16))

    tile_p = _row_tile(T, 512)
    cos_p, sin_p = _rope_tables(jnp.arange(T, dtype=I32))
    xp = x_prompt.reshape(B * T, D_MODEL)
    (rq, rk, rv, rg, nqT, ck, cv, sk, wk, ckT, cvT, skT, svT, wkT, wvT, svTb, wvTb, gtT) = _inproj_prompt(
        xp, ln_pre_mix, w_main, w_gate, cos_p, sin_p, B, T, tile_p)
    b3 = lambda t: t.reshape(B, T, t.shape[-1])
    ret_y, rs_p = _ret_prompt(b3(rq), b3(rk), b3(rv), b3(rg), ret_gn_w)
    cwk = _compress_weights(cmp_pos_k[0], cmp_w1_k[0], cmp_w2_k[0], True)
    cwv = _compress_weights(cmp_pos_v[0], cmp_w1_v[0], cmp_w2_v[0], True)
    ocmpT, biasT = _cmp_prompt(b3(ck), b3(cv), cwk, cwv, nqT, gtT, _row_tile(T))
    nsa_y = _att_prompt(nqT, biasT, b3(sk), svTb, b3(wk), wvTb, gtT, ocmpT)
    y_p = _finish(xp, ret_y.reshape(B * T, RET_WIDTH), nsa_y.reshape(B * T, NSA_WIDTH), *tail,
                  _row_tile(B * T, 512))
    win_p = min(WINDOW, T)
    kv5 = lambda t: t.reshape(t.shape[0], NSA_KV_HEADS, HEAD_DIM, t.shape[-1]).transpose(0, 3, 1, 2)[None]
    outs_p = (y_p.reshape(B, T, D_MODEL), rs_p[None],
              kv5(ckT), kv5(cvT), kv5(skT), kv5(svT),
              kv5(wkT[:, :, T - win_p:]), kv5(wvT[:, :, T - win_p:]))

    cos_s, sin_s = _rope_tables(jnp.full((DB,), past_len, I32))
    xs = x_sample.reshape(DB, D_MODEL)
    (rqT, rkT, rvT, rgT, nq, ckT, cvT, skT, svT, sk, sv, wk, wv, gt) = _inproj_sample(
        xs, ln_pre_mix, w_main, w_gate, cos_s, sin_s)
    ret_yT, rs_s = _ret_sample(rqT, rkT, rvT, rgT, ret_gn_w, state_ret[0].transpose(1, 2, 3, 0))
    feat_major = lambda t: t[0].transpose(0, 2, 3, 1).reshape(t.shape[1], KV_WIDTH, t.shape[2])
    cwk = _compress_weights(cmp_pos_k[0], cmp_w1_k[0], cmp_w2_k[0], False)
    cwv = _compress_weights(cmp_pos_v[0], cmp_w1_v[0], cmp_w2_v[0], False)
    ocmp, idx = _cmp_sample(page_table, nq, feat_major(cache_cmp_k), feat_major(cache_cmp_v), cwk, cwv)
    idx_flat = idx[:, :, :N_SELECT].reshape(-1)
    y_row, nwk, nwv = _att_sample(idx_flat, page_table, nq, sk, sv, wk, wv, gt, ocmp,
                                  feat_major(cache_sel_k), feat_major(cache_sel_v),
                                  feat_major(cache_win_k), feat_major(cache_win_v))
    y_s = _finish(xs, ret_yT, y_row.reshape(DB, NSA_WIDTH), *tail, DB, ret_t=True)
    new5 = lambda t: t.reshape(NSA_KV_HEADS, HEAD_DIM, DB).transpose(2, 0, 1).reshape(1, DB, 1, NSA_KV_HEADS, HEAD_DIM)
    outs_s = (y_s.reshape(DB, 1, D_MODEL), rs_s.transpose(3, 0, 1, 2)[None],
              new5(ckT), new5(cvT), new5(skT), new5(svT), kv5(nwk), kv5(nwv))

    return (outs_p[0], outs_s[0]) + outs_p[1:] + outs_s[1:]
```

```python
import functools

import numpy as np
import jax
import jax.numpy as jnp
from jax import lax
from jax.experimental import pallas as pl
from jax.experimental.pallas import tpu as pltpu

F32 = jnp.float32
BF16 = jnp.bfloat16
I32 = jnp.int32

D_MODEL = 1024
HEAD_DIM = 64
RET_WIDTH = 512
RET_HEADS = 8
NSA_WIDTH = 512
NSA_HEADS = 8
NSA_KV_HEADS = 2
NSA_GROUP = 4
KV_WIDTH = 128
N_BRANCH = 3
D_FF = 4 * D_MODEL
CMP_BLOCK = 64
N_SELECT = 16
N_PICK = N_SELECT - 1
WINDOW = 512
PAGE_SIZE = 128
RET_CHUNK = 128
ROPE_BASE = 10000.0
EPS = 1e-6
NEG = -1e30
SCALE = HEAD_DIM ** -0.5

MAIN_WIDTH = 4 * RET_WIDTH + NSA_WIDTH + 6 * KV_WIDTH
GATE_PAD = 128
ATT_Q = 128
ATT_K = 256
MAX_BLOCKS = 64
CMP_SAMPLES = 2
SUBLANES = 8
PAGE_TILES = KV_WIDTH // SUBLANES
VMEM_LIMIT = 56 * 1024 * 1024

_LOG_GAMMA = [float(np.log1p(-np.exp2(-5.0 - h))) for h in range(RET_HEADS)]


def _params(*sem):
    return pltpu.CompilerParams(dimension_semantics=sem, vmem_limit_bytes=VMEM_LIMIT)


def _dot(a, b):
    return jnp.dot(a, b, preferred_element_type=F32)


def _dot_nt(a, b):
    return lax.dot_general(a, b, (((1,), (1,)), ((), ())), preferred_element_type=F32)


def _dot_tn(a, b):
    return lax.dot_general(a, b, (((0,), (0,)), ((), ())), preferred_element_type=F32)


def _rms(x, w):
    return x * lax.rsqrt(jnp.mean(x * x, axis=-1, keepdims=True) + EPS) * w


def _projector(x_ref, lnw_ref, w_ref, cos_ref, sin_ref):
    h = _rms(x_ref[...], lnw_ref[...]).astype(BF16)
    rows = h.shape[0]

    acc = _dot(h, w_ref[...])

    def proj(a, b):
        return acc[:, a:b]

    cos = cos_ref[...]
    sin = sin_ref[...]
    lane = lax.broadcasted_iota(I32, (rows, RET_WIDTH), 1)
    first_half = (lane % HEAD_DIM) < (HEAD_DIM // 2)

    def rope(y):
        partner = jnp.where(first_half,
                            pltpu.roll(y, RET_WIDTH - HEAD_DIM // 2, 1),
                            pltpu.roll(y, HEAD_DIM // 2, 1))
        return y * cos + partner * sin

    return h, proj, rope


def _rope_tables(pos):
    half = HEAD_DIM // 2
    inv = ROPE_BASE ** (-jnp.arange(half, dtype=F32) / half)
    ang = pos.astype(F32)[:, None] * inv
    cos, sin = jnp.cos(ang), jnp.sin(ang)
    cos_h = jnp.concatenate([cos, cos], -1)
    sin_h = jnp.concatenate([-sin, sin], -1)
    return jnp.tile(cos_h, (1, RET_HEADS)), jnp.tile(sin_h, (1, RET_HEADS))


def _inproj_prompt_kernel(x_ref, lnw_ref, w_ref, wg_ref, cos_ref, sin_ref,
                          rq_ref, rk_ref, rv_ref, rg_ref, nqT_ref, ck_ref, cv_ref, sk_ref, wk_ref,
                          ckT_ref, cvT_ref, skT_ref, svT_ref, wkT_ref, wvT_ref, svTb_ref, wvTb_ref, gtT_ref):
    h, proj, rope = _projector(x_ref, lnw_ref, w_ref, cos_ref, sin_ref)
    rq_ref[...] = rope(proj(0, 512)).astype(BF16)
    rk_ref[...] = (rope(proj(512, 1024)) * SCALE).astype(BF16)
    rv_ref[...] = proj(1024, 1536).astype(BF16)
    rg_ref[...] = proj(1536, 2048)
    for n in range(NSA_WIDTH // KV_WIDTH):
        nq = proj(2048 + n * KV_WIDTH, 2048 + (n + 1) * KV_WIDTH) * SCALE
        nqT_ref[0, n * KV_WIDTH:(n + 1) * KV_WIDTH, :] = nq.T.astype(BF16)
    kv = [proj(2560 + n * KV_WIDTH, 2560 + (n + 1) * KV_WIDTH) for n in range(6)]
    ck_ref[...] = kv[0]
    cv_ref[...] = kv[1]
    sk_ref[...] = kv[2].astype(BF16)
    wk_ref[...] = kv[4].astype(BF16)
    for n, ref in enumerate((ckT_ref, cvT_ref, skT_ref, svT_ref, wkT_ref, wvT_ref)):
        ref[0] = kv[n].T
    svTb_ref[0] = kv[3].T.astype(BF16)
    wvTb_ref[0] = kv[5].T.astype(BF16)
    gtT_ref[0] = jax.nn.sigmoid(_dot(h, wg_ref[...])).T


def _inproj_prompt(x2d, lnw, w_main, w_gate, cos_t, sin_t, B, T, tile):
    rows = x2d.shape[0]
    nt = T // tile
    row = lambda i: (i, 0)
    const = lambda i: (0, 0)
    tab = lambda i: (i % nt, 0)
    colT = lambda i: (i // nt, 0, i % nt)
    rm = lambda w, dt: (pl.BlockSpec((tile, w), row), jax.ShapeDtypeStruct((rows, w), dt))
    tr = lambda w, dt: (pl.BlockSpec((1, w, tile), colT), jax.ShapeDtypeStruct((B, w, T), dt))
    outs = [rm(RET_WIDTH, BF16), rm(RET_WIDTH, BF16), rm(RET_WIDTH, BF16), rm(RET_WIDTH, F32),
            tr(NSA_WIDTH, BF16), rm(KV_WIDTH, F32), rm(KV_WIDTH, F32), rm(KV_WIDTH, BF16), rm(KV_WIDTH, BF16),
            *[tr(KV_WIDTH, F32)] * 6,
            tr(KV_WIDTH, BF16), tr(KV_WIDTH, BF16), tr(GATE_PAD, F32)]
    return pl.pallas_call(
        _inproj_prompt_kernel,
        grid=(rows // tile,),
        in_specs=[pl.BlockSpec((tile, D_MODEL), row),
                  pl.BlockSpec((1, D_MODEL), const),
                  pl.BlockSpec((D_MODEL, MAIN_WIDTH), const),
                  pl.BlockSpec((D_MODEL, GATE_PAD), const),
                  pl.BlockSpec((tile, RET_WIDTH), tab),
                  pl.BlockSpec((tile, RET_WIDTH), tab)],
        out_specs=[o[0] for o in outs],
        out_shape=[o[1] for o in outs],
        compiler_params=_params("parallel"),
        name="inproj_prompt",
    )(x2d, lnw, w_main, w_gate, cos_t, sin_t)


def _inproj_sample_kernel(x_ref, lnw_ref, w_ref, wg_ref, cos_ref, sin_ref,
                          rqT_ref, rkT_ref, rvT_ref, rgT_ref, nq_ref,
                          ckT_ref, cvT_ref, skT_ref, svT_ref, sk_ref, sv_ref, wk_ref, wv_ref, gt_ref):
    h, proj, rope = _projector(x_ref, lnw_ref, w_ref, cos_ref, sin_ref)
    rqT_ref[...] = rope(proj(0, 512)).T
    rkT_ref[...] = (rope(proj(512, 1024)) * SCALE).T
    rvT_ref[...] = proj(1024, 1536).T
    rgT_ref[...] = proj(1536, 2048).T
    nq_ref[...] = proj(2048, 2560) * SCALE
    kv = [proj(2560 + n * KV_WIDTH, 2560 + (n + 1) * KV_WIDTH) for n in range(6)]
    for n, ref in enumerate((ckT_ref, cvT_ref, skT_ref, svT_ref)):
        ref[...] = kv[n].T
    for n, ref in enumerate((sk_ref, sv_ref, wk_ref, wv_ref)):
        ref[...] = kv[2 + n]
    gt_ref[...] = jax.nn.sigmoid(_dot(h, wg_ref[...]))


def _inproj_sample(x2d, lnw, w_main, w_gate, cos_t, sin_t):
    rows = x2d.shape[0]
    full = lambda a, b: pl.BlockSpec((a, b), lambda i: (0, 0))
    rm = lambda w: (full(rows, w), jax.ShapeDtypeStruct((rows, w), F32))
    tr = lambda w: (full(w, rows), jax.ShapeDtypeStruct((w, rows), F32))
    outs = [tr(RET_WIDTH)] * 4 + [rm(NSA_WIDTH)] + [tr(KV_WIDTH)] * 4 + [rm(KV_WIDTH)] * 4 + [rm(GATE_PAD)]
    return pl.pallas_call(
        _inproj_sample_kernel,
        grid=(1,),
        in_specs=[full(rows, D_MODEL), full(1, D_MODEL), full(D_MODEL, MAIN_WIDTH), full(D_MODEL, GATE_PAD),
                  full(rows, RET_WIDTH), full(rows, RET_WIDTH)],
        out_specs=[o[0] for o in outs],
        out_shape=[o[1] for o in outs],
        compiler_params=_params("arbitrary"),
        name="inproj_sample",
    )(x2d, lnw, w_main, w_gate, cos_t, sin_t)


def _group_norm_gate(outs, g, gnw):
    ys = []
    for o in outs:
        mu = jnp.mean(o, axis=-1, keepdims=True)
        d = o - mu
        var = jnp.mean(d * d, axis=-1, keepdims=True)
        ys.append(d * lax.rsqrt(var + EPS))
    y = jnp.concatenate(ys, axis=1) * gnw
    return g * jax.nn.sigmoid(g) * y


def _ret_prompt_kernel(q_ref, k_ref, v_ref, g_ref, gnw_ref, inner_ref, qdec_ref, kdec_ref,
                       y_ref, s_out_ref, s_scr):
    c = pl.program_id(1)

    @pl.when(c == 0)
    def _():
        s_scr[...] = jnp.zeros_like(s_scr)

    q = q_ref[0]
    k = k_ref[0]
    v = v_ref[0]
    kd = (k.astype(F32) * kdec_ref[...]).astype(BF16)
    qdec = qdec_ref[...]
    sls = [slice(h * HEAD_DIM, (h + 1) * HEAD_DIM) for h in range(RET_HEADS)]
    att = [(_dot_nt(q[:, sl], k[:, sl]) * inner_ref[h]).astype(BF16) for h, sl in enumerate(sls)]
    cross = [_dot(q[:, sl], s_scr[h].astype(BF16)) * qdec[:, sl] for h, sl in enumerate(sls)]
    grow = [_dot_tn(kd[:, sl], v[:, sl]) for sl in sls]
    outs = [_dot(att[h], v[:, sl]) + cross[h] for h, sl in enumerate(sls)]
    for h in range(RET_HEADS):
        s_scr[h] = s_scr[h] * float(np.exp(_LOG_GAMMA[h] * RET_CHUNK)) + grow[h]
    y_ref[0] = _group_norm_gate(outs, g_ref[0], gnw_ref[...]).astype(BF16)

    @pl.when(c == pl.num_programs(1) - 1)
    def _():
        s_out_ref[0] = s_scr[...]


def _ret_prompt(rq, rk, rv, rg, gnw):
    B, T, _ = rq.shape
    C = RET_CHUNK
    lg = jnp.asarray(_LOG_GAMMA, F32)
    i = jnp.arange(C, dtype=F32)
    diff = i[:, None] - i[None, :]
    inner = jnp.where(diff >= 0, jnp.exp(lg[:, None, None] * jnp.maximum(diff, 0.0)), 0.0)
    qdec = jnp.repeat(jnp.exp(lg[None, :] * (i[:, None] + 1.0)), HEAD_DIM, axis=1)
    kdec = jnp.repeat(jnp.exp(lg[None, :] * (C - 1.0 - i[:, None])), HEAD_DIM, axis=1)
    blk = pl.BlockSpec((1, C, RET_WIDTH), lambda b, c: (b, c, 0))
    c2 = lambda b, c: (0, 0)
    return pl.pallas_call(
        _ret_prompt_kernel,
        grid=(B, T // C),
        in_specs=[blk, blk, blk, blk,
                  pl.BlockSpec((1, RET_WIDTH), c2),
                  pl.BlockSpec((RET_HEADS, C, C), lambda b, c: (0, 0, 0)),
                  pl.BlockSpec((C, RET_WIDTH), c2),
                  pl.BlockSpec((C, RET_WIDTH), c2)],
        out_specs=[blk, pl.BlockSpec((1, RET_HEADS, HEAD_DIM, HEAD_DIM), lambda b, c: (b, 0, 0, 0))],
        out_shape=[jax.ShapeDtypeStruct((B, T, RET_WIDTH), BF16),
                   jax.ShapeDtypeStruct((B, RET_HEADS, HEAD_DIM, HEAD_DIM), F32)],
        scratch_shapes=[pltpu.VMEM((RET_HEADS, HEAD_DIM, HEAD_DIM), F32)],
        compiler_params=_params("parallel", "arbitrary"),
        name="ret_prompt",
    )(rq, rk, rv, rg, gnw, inner, qdec, kdec)


def _ret_sample_kernel(q_ref, k_ref, v_ref, g_ref, gnw_ref, gam_ref, s_ref, y_ref, s_out_ref):
    q = q_ref[...]
    k = k_ref[...]
    v = v_ref[...]
    g = g_ref[...]
    gamma = gam_ref[0]
    acc = jnp.zeros(v.shape, F32)
    for d in range(HEAD_DIM):
        s_d = s_ref[0, d]
        acc = acc + s_d * q[d:d + 1, :]
        s_out_ref[0, d] = s_d * gamma + k[d:d + 1, :] * v
    o = acc * gamma + jnp.sum(q * k, axis=0, keepdims=True) * v
    mu = jnp.mean(o, axis=0, keepdims=True)
    dlt = o - mu
    var = jnp.mean(dlt * dlt, axis=0, keepdims=True)
    y_ref[...] = g * jax.nn.sigmoid(g) * (dlt * lax.rsqrt(var + EPS) * gnw_ref[...])


def _ret_sample(rqT, rkT, rvT, rgT, gnw, state_t):
    DB = rqT.shape[1]
    blk = pl.BlockSpec((HEAD_DIM, DB), lambda h: (h, 0))
    st = pl.BlockSpec((1, HEAD_DIM, HEAD_DIM, DB), lambda h: (h, 0, 0, 0))
    gam = jnp.broadcast_to(jnp.exp(jnp.asarray(_LOG_GAMMA, F32))[:, None, None], (RET_HEADS, 1, DB))
    return pl.pallas_call(
        _ret_sample_kernel,
        grid=(RET_HEADS,),
        in_specs=[blk, blk, blk, blk,
                  pl.BlockSpec((HEAD_DIM, 1), lambda h: (h, 0)),
                  pl.BlockSpec((1, 1, DB), lambda h: (h, 0, 0)),
                  st],
        out_specs=[blk, st],
        out_shape=[jax.ShapeDtypeStruct((RET_WIDTH, DB), F32),
                   jax.ShapeDtypeStruct(state_t.shape, F32)],
        compiler_params=_params("parallel"),
        name="ret_sample",
    )(rqT, rkT, rvT, rgT, gnw.reshape(RET_WIDTH, 1), gam, state_t)


def _compress(load_rows, pos_ref, w1_ref, w2_ref):
    acc = None
    for j2 in range(CMP_BLOCK // 2):
        x = jnp.concatenate([load_rows(2 * j2) + pos_ref[2 * j2:2 * j2 + 1, :],
                             load_rows(2 * j2 + 1) + pos_ref[2 * j2 + 1:2 * j2 + 2, :]], axis=1).astype(BF16)
        part = _dot(x, w1_ref[j2])
        acc = part if acc is None else acc + part
    h = jax.nn.gelu(acc)
    return _dot(h.astype(BF16), w2_ref[...])


def _block_diag2(w):
    z = jnp.zeros_like(w)
    return jnp.concatenate([jnp.concatenate([w, z], -1), jnp.concatenate([z, w], -1)], -2)


def _compress_weights(pos, w1, w2, token_major):
    if not token_major:
        pos, w1 = pos.T, w1.transpose(1, 0, 2)
    pos2 = jnp.concatenate([pos, pos], -1)
    w1p = _block_diag2(w1).reshape(CMP_BLOCK // 2, 2 * KV_WIDTH, KV_WIDTH)
    return pos2, w1p.astype(BF16), _block_diag2(w2).astype(BF16)


def _rank_rows(imp, rank_scr, last_live):
    nb, tq = imp.shape
    sub = SUBLANES
    n_groups = nb // sub
    groups = [imp[sub * j:sub * (j + 1)] for j in range(n_groups)]
    row_in_group = lax.broadcasted_iota(I32, (sub, tq), 0)
    rank_scr[...] = jnp.zeros((nb, tq), I32)
    for jm in range(n_groups):
        @pl.when(jm * sub <= last_live)
        def _():
            ranks = [rank_scr[sub * j:sub * (j + 1), :] for j in range(n_groups)]
            for sm in range(sub):
                row = imp[jm * sub + sm:jm * sub + sm + 1, :]
                for j, grp in enumerate(groups):
                    if j < jm:
                        inc = jnp.where(row > grp, 1, 0)
                    elif j > jm:
                        inc = jnp.where(row >= grp, 1, 0)
                    else:
                        inc = jnp.where(row_in_group > sm, jnp.where(row >= grp, 1, 0),
                                        jnp.where(row > grp, 1, 0))
                    ranks[j] = ranks[j] + inc
            for j in range(n_groups):
                rank_scr[sub * j:sub * (j + 1), :] = ranks[j]
    return rank_scr[...]


def _kv_head_rows(x, g):
    z = jnp.zeros_like(x)
    return jnp.concatenate([x, z] if g == 0 else [z, x], axis=0)


def _cmp_prompt_kernel(ck_ref, cv_ref, posk_ref, w1k_ref, w2k_ref, posv_ref, w1v_ref, w2v_ref,
                       nqT_ref, gtT_ref, ocmpT_ref, biasT_ref, kc_scr, vc_scr, rank_scr):
    i = pl.program_id(1)
    nb = kc_scr.shape[0]
    tq = nqT_ref.shape[2]

    @pl.when(i == 0)
    def _():
        kc_scr[...] = _compress(lambda l: ck_ref[0, pl.ds(l, nb, stride=CMP_BLOCK), :],
                                posk_ref, w1k_ref, w2k_ref).astype(BF16)
        vc_scr[...] = _compress(lambda l: cv_ref[0, pl.ds(l, nb, stride=CMP_BLOCK), :],
                                posv_ref, w1v_ref, w2v_ref).astype(BF16)

    kc = kc_scr[...]
    vc = vc_scr[...]
    pos = i * tq + lax.broadcasted_iota(I32, (nb, tq), 1)
    n_idx = lax.broadcasted_iota(I32, (nb, tq), 0)
    valid = (n_idx + 1) * CMP_BLOCK - 1 <= pos
    cur = pos // CMP_BLOCK
    cand = n_idx < cur
    for g in range(NSA_KV_HEADS):
        qT = jnp.concatenate(
            [_kv_head_rows(nqT_ref[0, (g * NSA_GROUP + r) * HEAD_DIM:(g * NSA_GROUP + r + 1) * HEAD_DIM, :], g)
             for r in range(NSA_GROUP)], axis=1)
        valid4 = jnp.concatenate([valid] * NSA_GROUP, axis=1)
        s = jnp.where(valid4, _dot(kc, qT), NEG)
        e = jnp.exp(s - jnp.max(s, axis=0, keepdims=True))
        p = jnp.where(valid4, e / jnp.sum(e, axis=0, keepdims=True), 0.0)
        oT = _dot_tn(vc, p.astype(BF16))[g * HEAD_DIM:(g + 1) * HEAD_DIM]
        imp = jnp.zeros((nb, tq), F32)
        for r in range(NSA_GROUP):
            hh = g * NSA_GROUP + r
            imp = imp + p[:, r * tq:(r + 1) * tq]
            ocmpT_ref[0, hh * HEAD_DIM:(hh + 1) * HEAD_DIM, :] = (
                oT[:, r * tq:(r + 1) * tq] * gtT_ref[0, N_BRANCH * hh:N_BRANCH * hh + 1, :])
        imp = jnp.where(cand, imp, -1.0)
        rank = _rank_rows(imp, rank_scr, (i * tq + tq - 1) // CMP_BLOCK)
        sel = (cand & (rank < N_PICK)) | (n_idx == cur)
        biasT_ref[0, g * MAX_BLOCKS:g * MAX_BLOCKS + nb, :] = jnp.where(sel, 0.0, NEG).astype(BF16)
        if nb < MAX_BLOCKS:
            biasT_ref[0, g * MAX_BLOCKS + nb:(g + 1) * MAX_BLOCKS, :] = jnp.zeros((MAX_BLOCKS - nb, tq), BF16)


def _cmp_weight_specs(c2, c3):
    return [pl.BlockSpec((CMP_BLOCK, KV_WIDTH), c2),
            pl.BlockSpec((CMP_BLOCK // 2, 2 * KV_WIDTH, KV_WIDTH), c3),
            pl.BlockSpec((KV_WIDTH, KV_WIDTH), c2)]


def _cmp_prompt(ck, cv, cwk, cwv, nqT, gtT, tq):
    B, T, _ = ck.shape
    nb = T // CMP_BLOCK
    full = pl.BlockSpec((1, T, KV_WIDTH), lambda b, i: (b, 0, 0))
    wspecs = _cmp_weight_specs(lambda b, i: (0, 0), lambda b, i: (0, 0, 0))
    blk = lambda w: pl.BlockSpec((1, w, tq), lambda b, i: (b, 0, i))
    return pl.pallas_call(
        _cmp_prompt_kernel,
        grid=(B, T // tq),
        in_specs=[full, full] + wspecs + wspecs + [blk(NSA_WIDTH), blk(GATE_PAD)],
        out_specs=[blk(NSA_WIDTH), blk(NSA_KV_HEADS * MAX_BLOCKS)],
        out_shape=[jax.ShapeDtypeStruct((B, NSA_WIDTH, T), F32),
                   jax.ShapeDtypeStruct((B, NSA_KV_HEADS * MAX_BLOCKS, T), BF16)],
        scratch_shapes=[pltpu.VMEM((nb, KV_WIDTH), BF16), pltpu.VMEM((nb, KV_WIDTH), BF16),
                        pltpu.VMEM((nb, tq), I32)],
        compiler_params=_params("parallel", "arbitrary"),
        name="cmp_prompt",
    )(ck, cv, *cwk, *cwv, nqT, gtT)


def _att_prompt_kernel(nqT_ref, biasT_ref, sk_ref, svT_ref, wk_ref, wvT_ref, gtT_ref, ocmpT_ref,
                       y_ref, ka_scr, va_scr, s_scr, m_scr, acc_scr):
    i = pl.program_id(1)
    tq = nqT_ref.shape[2]
    T = sk_ref.shape[1]
    cols = NSA_GROUP * tq

    @pl.when(i == 0)
    def _():
        sk = sk_ref[0].astype(F32)
        lane = lax.broadcasted_iota(I32, (T, KV_WIDTH), 1)
        blk = lax.broadcasted_iota(I32, (T, KV_WIDTH), 0) // CMP_BLOCK
        ka_scr[0] = jnp.where(lane < HEAD_DIM, sk, jnp.where(lane - HEAD_DIM == blk, 1.0, 0.0)).astype(BF16)
        ka_scr[1] = jnp.where(lane >= HEAD_DIM, sk, jnp.where(lane == blk, 1.0, 0.0)).astype(BF16)
        svT = svT_ref[0].astype(F32)
        row = lax.broadcasted_iota(I32, (KV_WIDTH, T), 0)
        va_scr[0] = jnp.where(row < HEAD_DIM, svT, 1.0).astype(BF16)
        va_scr[1] = jnp.where(row >= HEAD_DIM, svT, 1.0).astype(BF16)

    q_pos = i * tq + lax.broadcasted_iota(I32, (ATT_K, cols), 1) % tq
    k_off = lax.broadcasted_iota(I32, (ATT_K, cols), 0)
    n_win = WINDOW // tq
    win_keys = (n_win + 1) * tq
    win_start = pl.multiple_of(jnp.maximum(i - n_win, 0) * tq, tq)
    rel = i * tq - win_start + lax.broadcasted_iota(I32, (win_keys, tq), 1)
    j_win = lax.broadcasted_iota(I32, (win_keys, tq), 0)
    win_bias = jnp.where((j_win <= rel) & (j_win > rel - WINDOW), 0.0, NEG)
    win_bias = jnp.concatenate([win_bias] * NSA_GROUP, axis=1)
    wk = wk_ref[0, pl.ds(win_start, win_keys), :]
    wvT = wvT_ref[0, :, pl.ds(win_start, win_keys)]

    qaTs, qwTs = [], []
    for g in range(NSA_KV_HEADS):
        qTs = [nqT_ref[0, (g * NSA_GROUP + r) * HEAD_DIM:(g * NSA_GROUP + r + 1) * HEAD_DIM, :]
               for r in range(NSA_GROUP)]
        bT = biasT_ref[0, g * MAX_BLOCKS:(g + 1) * MAX_BLOCKS, :]
        qaTs.append(jnp.concatenate([jnp.concatenate([q, bT] if g == 0 else [bT, q], axis=0) for q in qTs],
                                    axis=1))
        qwTs.append(jnp.concatenate([_kv_head_rows(q, g) for q in qTs], axis=1))

    def scores_into(slot, c):
        start = pl.multiple_of(c * ATT_K, ATT_K)
        for g in range(NSA_KV_HEADS):
            s_scr[slot, g] = _dot(ka_scr[g, pl.ds(start, ATT_K), :], qaTs[g])

    def sel_step(c, slot, last):
        start = pl.multiple_of(c * ATT_K, ATT_K)
        if not last:
            scores_into(1 - slot, c + 1)
        for g in range(NSA_KV_HEADS):
            s = s_scr[slot, g]
            if last:
                s = jnp.where(start + k_off > q_pos, NEG, s)
            m = m_scr[g]
            m_new = jnp.maximum(m, jnp.max(s, axis=0, keepdims=True))
            m_scr[g] = m_new
            p = jnp.exp(s - m_new)
            pv = _dot(va_scr[g, :, pl.ds(start, ATT_K)], p.astype(BF16))
            acc_scr[g] = jnp.exp(m - m_new) * acc_scr[g] + pv

    m_scr[...] = jnp.full(m_scr.shape, NEG, F32)
    acc_scr[...] = jnp.zeros(acc_scr.shape, F32)
    scores_into(0, 0)
    n_before = (i * tq) // ATT_K

    def pair(pr, carry):
        sel_step(2 * pr, 0, False)
        sel_step(2 * pr + 1, 1, False)
        return carry

    lax.fori_loop(0, n_before // 2, pair, 0)

    @pl.when(n_before % 2 == 1)
    def _():
        sel_step(n_before - 1, 0, False)

    sel_step(n_before, n_before % 2, True)

    heads_out = []
    for g in range(NSA_KV_HEADS):
        gs = slice(g * HEAD_DIM, (g + 1) * HEAD_DIM)
        acc = acc_scr[g]
        ones_row = (1 - g) * HEAD_DIM
        o_sel = acc[gs] / acc[ones_row:ones_row + 1]

        s = _dot(wk, qwTs[g]) + win_bias
        p = jnp.exp(s - jnp.max(s, axis=0, keepdims=True))
        o_win = _dot(wvT, p.astype(BF16))[gs] / jnp.sum(p, axis=0, keepdims=True)

        for r in range(NSA_GROUP):
            hh = g * NSA_GROUP + r
            cs = slice(r * tq, (r + 1) * tq)
            heads_out.append(ocmpT_ref[0, hh * HEAD_DIM:(hh + 1) * HEAD_DIM, :]
                             + gtT_ref[0, N_BRANCH * hh + 1:N_BRANCH * hh + 2, :] * o_sel[:, cs]
                             + gtT_ref[0, N_BRANCH * hh + 2:N_BRANCH * hh + 3, :] * o_win[:, cs])
    y_ref[0] = jnp.concatenate(heads_out, axis=0).T.astype(BF16)


def _att_prompt(nqT, biasT, sk, svT, wk, wvT, gtT, ocmpT):
    B, _, T = nqT.shape
    tq = ATT_Q
    rows_full = pl.BlockSpec((1, T, KV_WIDTH), lambda b, i: (b, 0, 0))
    cols_full = pl.BlockSpec((1, KV_WIDTH, T), lambda b, i: (b, 0, 0))
    blk = lambda w: pl.BlockSpec((1, w, tq), lambda b, i: (b, 0, i))
    return pl.pallas_call(
        _att_prompt_kernel,
        grid=(B, T // tq),
        in_specs=[blk(NSA_WIDTH), blk(NSA_KV_HEADS * MAX_BLOCKS), rows_full, cols_full, rows_full, cols_full,
                  blk(GATE_PAD), blk(NSA_WIDTH)],
        out_specs=pl.BlockSpec((1, tq, NSA_WIDTH), lambda b, i: (b, i, 0)),
        out_shape=jax.ShapeDtypeStruct((B, T, NSA_WIDTH), BF16),
        scratch_shapes=[pltpu.VMEM((NSA_KV_HEADS, T, KV_WIDTH), BF16),
                        pltpu.VMEM((NSA_KV_HEADS, KV_WIDTH, T), BF16),
                        pltpu.VMEM((2, NSA_KV_HEADS, ATT_K, NSA_GROUP * tq), F32),
                        pltpu.VMEM((NSA_KV_HEADS, 1, NSA_GROUP * tq), F32),
                        pltpu.VMEM((NSA_KV_HEADS, KV_WIDTH, NSA_GROUP * tq), F32)],
        compiler_params=_params("parallel", "arbitrary"),
        name="att_prompt",
    )(nqT, biasT, sk, svT, wk, wvT, gtT, ocmpT)


def _finish_kernel(x_ref, ry_ref, ny_ref, wo_ref, l1_ref, l2_ref, l3_ref, wu_ref, wd_ref, o_ref, *, ret_t):
    if ret_t:
        mix = _dot_tn(ry_ref[...].astype(BF16), wo_ref[0:RET_WIDTH, :])
    else:
        mix = _dot(ry_ref[...], wo_ref[0:RET_WIDTH, :])
    mix = mix + _dot(ny_ref[...].astype(BF16), wo_ref[RET_WIDTH:, :])
    h = x_ref[...] + _rms(mix, l1_ref[...])
    u = _dot(_rms(h, l2_ref[...]).astype(BF16), wu_ref[...])
    a = jnp.square(jnp.maximum(u, 0.0)).astype(BF16)
    f = _dot(a, wd_ref[...])
    o_ref[...] = h + _rms(f, l3_ref[...])


def _finish(x2d, ret_y, nsa_y, w_o, l1, l2, l3, w_up, w_down, tile, ret_t=False):
    rows = x2d.shape[0]
    row = lambda w: pl.BlockSpec((tile, w), lambda i: (i, 0))
    const = lambda a, b: pl.BlockSpec((a, b), lambda i: (0, 0), pipeline_mode=pl.Buffered(1))
    ret_spec = pl.BlockSpec((RET_WIDTH, tile), lambda i: (0, i)) if ret_t else row(RET_WIDTH)
    return pl.pallas_call(
        functools.partial(_finish_kernel, ret_t=ret_t),
        grid=(rows // tile,),
        in_specs=[row(D_MODEL), ret_spec, row(NSA_WIDTH),
                  const(D_MODEL, D_MODEL), const(1, D_MODEL), const(1, D_MODEL), const(1, D_MODEL),
                  const(D_MODEL, D_FF), const(D_FF, D_MODEL)],
        out_specs=row(D_MODEL),
        out_shape=jax.ShapeDtypeStruct((rows, D_MODEL), F32),
        compiler_params=_params("parallel"),
        name="finish",
    )(x2d, ret_y, nsa_y, w_o, l1, l2, l3, w_up, w_down)


def _column(row):
    n = row.shape[1]
    eye = lax.broadcasted_iota(I32, (n, n), 0) == lax.broadcasted_iota(I32, (n, n), 1)
    return jnp.sum(jnp.where(eye, row, 0.0), axis=1, keepdims=True)


def _cmp_sample_kernel(pt_ref, nq_ref, poolk_ref, poolv_ref,
                       posk_ref, w1k_ref, w2k_ref, posv_ref, w1v_ref, w2v_ref,
                       ocmp_ref, idx_ref, kbuf, vbuf, ksem, vsem):
    step = pl.program_id(0)
    n_pages = pt_ref.shape[1]
    nbk = 2 * n_pages
    slot = step % 2

    def fetch(st, sl, start):
        for j in range(CMP_SAMPLES):
            for p in range(n_pages):
                tiles = pl.ds(((sl * CMP_SAMPLES + j) * n_pages + p) * PAGE_TILES, PAGE_TILES)
                page = pt_ref[st * CMP_SAMPLES + j, p]
                for pool, buf, sem in ((poolk_ref, kbuf, ksem), (poolv_ref, vbuf, vsem)):
                    cp = pltpu.make_async_copy(pool.at[page], buf.at[tiles], sem.at[sl])
                    cp.start() if start else cp.wait()

    @pl.when(step == 0)
    def _():
        fetch(0, 0, True)

    @pl.when(step + 1 < pl.num_programs(0))
    def _():
        fetch(step + 1, 1 - slot, True)

    fetch(step, slot, False)

    step_pages = CMP_SAMPLES * n_pages
    base = slot * step_pages * PAGE_TILES
    group_tiles = HEAD_DIM // SUBLANES

    def rows_of(buf):
        cache = {}

        def load(d):
            fg, s = divmod(d, SUBLANES)
            if fg not in cache:
                cache.clear()
                cache[fg] = [jnp.swapaxes(buf[pl.ds(base + g * group_tiles + fg, step_pages, stride=PAGE_TILES)],
                                          0, 1) for g in range(NSA_KV_HEADS)]
            return jnp.concatenate([t[s] for t in cache[fg]], axis=0)
        return load

    kc = _compress(rows_of(kbuf), posk_ref, w1k_ref, w2k_ref).astype(BF16)
    vc = _compress(rows_of(vbuf), posv_ref, w1v_ref, w2v_ref).astype(BF16)

    j_lane = lax.broadcasted_iota(I32, (nbk, nbk), 1)
    j_sub = lax.broadcasted_iota(I32, (nbk, nbk), 0)
    n_lane = 2 * (j_lane % n_pages) + j_lane // n_pages
    n_sub = 2 * (j_sub % n_pages) + j_sub // n_pages
    z = jnp.zeros((1, HEAD_DIM), F32)
    for j in range(CMP_SAMPLES):
        q = nq_ref[j]
        o_rows, idx_rows = [], []
        for g in range(NSA_KV_HEADS):
            blk_rows = slice((g * CMP_SAMPLES + j) * n_pages, (g * CMP_SAMPLES + j + 1) * n_pages)
            pieces = [q[:, (g * NSA_GROUP + r) * HEAD_DIM:(g * NSA_GROUP + r + 1) * HEAD_DIM]
                      for r in range(NSA_GROUP)]
            qh = jnp.concatenate([jnp.concatenate([pc, z], axis=1) for pc in pieces]
                                 + [jnp.concatenate([z, pc], axis=1) for pc in pieces], axis=0)
            s = _dot_nt(qh.astype(BF16), kc[blk_rows])
            m = jnp.max(s, axis=1, keepdims=True)
            m = jnp.maximum(m[:NSA_GROUP], m[NSA_GROUP:])
            e = jnp.exp(s - jnp.concatenate([m, m], axis=0))
            l = jnp.sum(e, axis=1, keepdims=True)
            l = l[:NSA_GROUP] + l[NSA_GROUP:]
            p = e / jnp.concatenate([l, l], axis=0)
            out = _dot(p.astype(BF16), vc[blk_rows])
            o_rows.append(out[:NSA_GROUP, :HEAD_DIM] + out[NSA_GROUP:, HEAD_DIM:])
            imp = jnp.concatenate([jnp.sum(p[:NSA_GROUP], axis=0, keepdims=True),
                                   jnp.sum(p[NSA_GROUP:], axis=0, keepdims=True)], axis=1)
            imp_col = _column(imp)
            ahead = (imp > imp_col) | ((imp == imp_col) & (n_lane < n_sub))
            rank_col = jnp.sum(jnp.where(ahead, 1, 0), axis=1, keepdims=True)
            idx_rows.append(jnp.sum(jnp.where(rank_col == j_lane, n_sub, 0), axis=0, keepdims=True))
        ocmp_ref[j] = jnp.concatenate([o[r:r + 1] for o in o_rows for r in range(NSA_GROUP)], axis=1)
        idx_ref[j] = jnp.concatenate(idx_rows, axis=0)


def _cmp_sample(page_table, nq, pool_k, pool_v, cwk, cwv):
    DB, n_pages = page_table.shape
    nbk = 2 * n_pages
    wspecs = _cmp_weight_specs(lambda s, pt: (0, 0), lambda s, pt: (0, 0, 0))
    any_spec = pl.BlockSpec(memory_space=pl.ANY)
    tiled = lambda t: t.reshape(t.shape[0], PAGE_TILES, SUBLANES, PAGE_SIZE)
    buf_shape = (2 * CMP_SAMPLES * n_pages * PAGE_TILES, SUBLANES, PAGE_SIZE)
    grid_spec = pltpu.PrefetchScalarGridSpec(
        num_scalar_prefetch=1,
        grid=(DB // CMP_SAMPLES,),
        in_specs=[pl.BlockSpec((CMP_SAMPLES, 1, NSA_WIDTH), lambda s, pt: (s, 0, 0)), any_spec, any_spec]
        + wspecs + wspecs,
        out_specs=[pl.BlockSpec((CMP_SAMPLES, 1, NSA_WIDTH), lambda s, pt: (s, 0, 0)),
                   pl.BlockSpec((CMP_SAMPLES, NSA_KV_HEADS, nbk), lambda s, pt: (s, 0, 0))],
        scratch_shapes=[pltpu.VMEM(buf_shape, F32),
                        pltpu.VMEM(buf_shape, F32),
                        pltpu.SemaphoreType.DMA((2,)),
                        pltpu.SemaphoreType.DMA((2,))],
    )
    pool_k, pool_v = tiled(pool_k), tiled(pool_v)
    return pl.pallas_call(
        _cmp_sample_kernel,
        grid_spec=grid_spec,
        out_shape=[jax.ShapeDtypeStruct((DB, 1, NSA_WIDTH), F32),
                   jax.ShapeDtypeStruct((DB, NSA_KV_HEADS, nbk), I32)],
        compiler_params=_params("arbitrary"),
        name="cmp_sample",
    )(page_table, nq.reshape(DB, 1, NSA_WIDTH), pool_k, pool_v, *cwk, *cwv)


def _att_sample_kernel(idx_ref, pt_ref, nq_ref, knew_ref, vnew_ref, wknew_ref, wvnew_ref,
                       gt_ref, ocmp_ref, poolk_ref, poolv_ref, wkT_ref, wvT_ref,
                       y_ref, wk_out_ref, wv_out_ref, kbuf, vbuf, ksem, vsem, *, past_len):
    b = pl.program_id(0)
    slot = b % 2

    def gather(bb, sl, start):
        for g in range(NSA_KV_HEADS):
            for s in range(N_PICK):
                blk = idx_ref[(bb * NSA_KV_HEADS + g) * N_SELECT + s]
                page = pt_ref[bb, blk // 2]
                for pool, buf, sem in ((poolk_ref, kbuf, ksem), (poolv_ref, vbuf, vsem)):
                    cp = pltpu.make_async_copy(pool.at[page, pl.ds(g * HEAD_DIM, HEAD_DIM), :],
                                               buf.at[sl, g, :, pl.ds(s * PAGE_SIZE, PAGE_SIZE)], sem.at[sl])
                    cp.start() if start else cp.wait()

    @pl.when(b == 0)
    def _():
        gather(0, 0, True)

    @pl.when(b + 1 < pl.num_programs(0))
    def _():
        gather(b + 1, 1 - slot, True)

    q = nq_ref[0]
    pieces = [q[:, hh * HEAD_DIM:(hh + 1) * HEAD_DIM] for hh in range(NSA_HEADS)]
    z = jnp.zeros((1, HEAD_DIM), F32)
    q_wide = jnp.concatenate([jnp.concatenate([pc, z] if hh < NSA_GROUP else [z, pc], axis=1)
                              for hh, pc in enumerate(pieces)], axis=0)

    def bf_round(x):
        return x.astype(BF16).astype(F32)

    def attend(qf, s_past, vT_past, k_new, v_new):
        s_self = jnp.sum(bf_round(qf) * bf_round(k_new), axis=1, keepdims=True)
        m = jnp.maximum(jnp.max(s_past, axis=-1, keepdims=True), s_self)
        p = jnp.exp(s_past - m)
        p_self = jnp.exp(s_self - m)
        l = jnp.sum(p, axis=-1, keepdims=True) + p_self
        o = _dot_nt(p.astype(BF16), vT_past) + bf_round(p_self) * bf_round(v_new)
        return o / l

    wkT = wkT_ref[0]
    wvT = wvT_ref[0]
    wb = wkT.shape[1]
    j = lax.broadcasted_iota(I32, (NSA_HEADS, wb), 1)
    pos_k = past_len - wb + j
    keep = (past_len - pos_k < WINDOW) & (pos_k >= 0)
    s_win = jnp.where(keep, _dot(q_wide.astype(BF16), wkT.astype(BF16)), NEG)
    o_win = attend(q_wide, s_win, wvT.astype(BF16), wknew_ref[0], wvnew_ref[0])

    lane_w = lax.broadcasted_iota(I32, (KV_WIDTH, wb), 1)
    wk_out_ref[0] = jnp.where(lane_w == wb - 1, _column(wknew_ref[0]), pltpu.roll(wkT, wb - 1, 1))
    wv_out_ref[0] = jnp.where(lane_w == wb - 1, _column(wvnew_ref[0]), pltpu.roll(wvT, wb - 1, 1))

    gather(b, slot, False)
    n_keys = N_PICK * PAGE_SIZE
    half_of_lane = (lax.broadcasted_iota(I32, (1, n_keys), 1) % PAGE_SIZE) // CMP_BLOCK
    zeros4 = jnp.zeros((NSA_GROUP, HEAD_DIM), F32)
    gt = gt_ref[0]
    ocmp = ocmp_ref[0]
    y_pieces = []
    for g in range(NSA_KV_HEADS):
        gs = slice(g * HEAD_DIM, (g + 1) * HEAD_DIM)
        want = jnp.concatenate(
            [jnp.full((1, PAGE_SIZE), idx_ref[(b * NSA_KV_HEADS + g) * N_SELECT + s] % 2, I32)
             for s in range(N_PICK)], axis=1)
        qg = jnp.concatenate(pieces[g * NSA_GROUP:(g + 1) * NSA_GROUP] + [zeros4], axis=0)
        s_sel = jnp.where(half_of_lane == want, _dot(qg.astype(BF16), kbuf[slot, g].astype(BF16)), NEG)
        o_sel = attend(qg, s_sel, vbuf[slot, g].astype(BF16), knew_ref[0][:, gs], vnew_ref[0][:, gs])
        for r in range(NSA_GROUP):
            hh = g * NSA_GROUP + r
            y_pieces.append(gt[:, N_BRANCH * hh:N_BRANCH * hh + 1] * ocmp[:, hh * HEAD_DIM:(hh + 1) * HEAD_DIM]
                            + gt[:, N_BRANCH * hh + 1:N_BRANCH * hh + 2] * o_sel[r:r + 1]
                            + gt[:, N_BRANCH * hh + 2:N_BRANCH * hh + 3] * o_win[hh:hh + 1, gs])
    y_ref[0] = jnp.concatenate(y_pieces, axis=1)


def _att_sample(idx, page_table, nq, sk, sv, wk_new, wv_new, gates, ocmp, pool_k, pool_v, buf_k, buf_v):
    DB, n_pages = page_table.shape
    past_len = n_pages * PAGE_SIZE
    wb = buf_k.shape[2]
    r3 = lambda t: t.reshape(DB, 1, t.shape[-1])
    row = lambda w: pl.BlockSpec((1, 1, w), lambda b, ix, pt: (b, 0, 0))
    any_spec = pl.BlockSpec(memory_space=pl.ANY)
    wspec = pl.BlockSpec((1, KV_WIDTH, wb), lambda b, ix, pt: (b, 0, 0))
    buf_shape = (2, NSA_KV_HEADS, HEAD_DIM, N_PICK * PAGE_SIZE)
    grid_spec = pltpu.PrefetchScalarGridSpec(
        num_scalar_prefetch=2,
        grid=(DB,),
        in_specs=[row(NSA_WIDTH), row(KV_WIDTH), row(KV_WIDTH), row(KV_WIDTH), row(KV_WIDTH),
                  row(GATE_PAD), row(NSA_WIDTH), any_spec, any_spec, wspec, wspec],
        out_specs=[row(NSA_WIDTH), wspec, wspec],
        scratch_shapes=[pltpu.VMEM(buf_shape, F32), pltpu.VMEM(buf_shape, F32),
                        pltpu.SemaphoreType.DMA((2,)), pltpu.SemaphoreType.DMA((2,))],
    )
    return pl.pallas_call(
        functools.partial(_att_sample_kernel, past_len=past_len),
        grid_spec=grid_spec,
        out_shape=[jax.ShapeDtypeStruct((DB, 1, NSA_WIDTH), F32),
                   jax.ShapeDtypeStruct(buf_k.shape, F32),
                   jax.ShapeDtypeStruct(buf_v.shape, F32)],
        compiler_params=_params("arbitrary"),
        name="att_sample",
    )(idx, page_table, r3(nq), r3(sk), r3(sv), r3(wk_new), r3(wv_new), r3(gates), ocmp,
      pool_k, pool_v, buf_k, buf_v)


def _row_tile(rows, want=256):
    return next(t for t in (want, 256, 128) if rows % t == 0)


def kernel(x_prompt, x_sample, state_ret, cache_cmp_k, cache_cmp_v, cache_sel_k, cache_sel_v,
           cache_win_k, cache_win_v, page_table, ln_pre_mix, w_in, ret_gn_w,
           cmp_pos_k, cmp_w1_k, cmp_w2_k, cmp_pos_v, cmp_w1_v, cmp_w2_v,
           w_o, ln_post_mix, ln_pre_ffn, w_up, w_down, ln_post_ffn):
    B, T, _ = x_prompt.shape
    DB, Tn, _ = x_sample.shape
    n_pages = page_table.shape[1]
    past_len = n_pages * PAGE_SIZE
    wb = cache_win_k.shape[2]
    assert w_in.shape[0] == 1 and Tn == 1
    assert T % ATT_K == 0 and T >= WINDOW + ATT_Q and T // CMP_BLOCK <= MAX_BLOCKS
    assert 2 * n_pages >= N_PICK and DB % CMP_SAMPLES == 0

    w_main = w_in[0, :, :MAIN_WIDTH].astype(BF16)
    w_gate = jnp.pad(w_in[0, :, MAIN_WIDTH:], ((0, 0), (0, GATE_PAD - NSA_HEADS * N_BRANCH))).astype(BF16)
    tail = (w_o[0].astype(BF16), ln_post_mix, ln_pre_ffn, ln_post_ffn, w_up[0].astype(BF16), w_down[0].astype(BF16))

    tile_p = _row_tile(T, 512)
    cos_p, sin_p = _rope_tables(jnp.arange(T, dtype=I32))
    xp = x_prompt.reshape(B * T, D_MODEL)
    (rq, rk, rv, rg, nqT, ck, cv, sk, wk, ckT, cvT, skT, svT, wkT, wvT, svTb, wvTb, gtT) = _inproj_prompt(
        xp, ln_pre_mix, w_main, w_gate, cos_p, sin_p, B, T, tile_p)
    b3 = lambda t: t.reshape(B, T, t.shape[-1])
    ret_y, rs_p = _ret_prompt(b3(rq), b3(rk), b3(rv), b3(rg), ret_gn_w)
    cwk = _compress_weights(cmp_pos_k[0], cmp_w1_k[0], cmp_w2_k[0], True)
    cwv = _compress_weights(cmp_pos_v[0], cmp_w1_v[0], cmp_w2_v[0], True)
    ocmpT, biasT = _cmp_prompt(b3(ck), b3(cv), cwk, cwv, nqT, gtT, _row_tile(T))
    nsa_y = _att_prompt(nqT, biasT, b3(sk), svTb, b3(wk), wvTb, gtT, ocmpT)
    y_p = _finish(xp, ret_y.reshape(B * T, RET_WIDTH), nsa_y.reshape(B * T, NSA_WIDTH), *tail,
                  _row_tile(B * T, 512))
    win_p = min(WINDOW, T)
    kv5 = lambda t: t.reshape(t.shape[0], NSA_KV_HEADS, HEAD_DIM, t.shape[-1]).transpose(0, 3, 1, 2)[None]
    outs_p = (y_p.reshape(B, T, D_MODEL), rs_p[None],
              kv5(ckT), kv5(cvT), kv5(skT), kv5(svT),
              kv5(wkT[:, :, T - win_p:]), kv5(wvT[:, :, T - win_p:]))

    cos_s, sin_s = _rope_tables(jnp.full((DB,), past_len, I32))
    xs = x_sample.reshape(DB, D_MODEL)
    (rqT, rkT, rvT, rgT, nq, ckT, cvT, skT, svT, sk, sv, wk, wv, gt) = _inproj_sample(
        xs, ln_pre_mix, w_main, w_gate, cos_s, sin_s)
    ret_yT, rs_s = _ret_sample(rqT, rkT, rvT, rgT, ret_gn_w, state_ret[0].transpose(1, 2, 3, 0))
    feat_major = lambda t: t[0].transpose(0, 2, 3, 1).reshape(t.shape[1], KV_WIDTH, t.shape[2])
    cwk = _compress_weights(cmp_pos_k[0], cmp_w1_k[0], cmp_w2_k[0], False)
    cwv = _compress_weights(cmp_pos_v[0], cmp_w1_v[0], cmp_w2_v[0], False)
    ocmp, idx = _cmp_sample(page_table, nq, feat_major(cache_cmp_k), feat_major(cache_cmp_v), cwk, cwv)
    idx_flat = idx[:, :, :N_SELECT].reshape(-1)
    y_row, nwk, nwv = _att_sample(idx_flat, page_table, nq, sk, sv, wk, wv, gt, ocmp,
                                  feat_major(cache_sel_k), feat_major(cache_sel_v),
                                  feat_major(cache_win_k), feat_major(cache_win_v))
    y_s = _finish(xs, ret_yT, y_row.reshape(DB, NSA_WIDTH), *tail, DB, ret_t=True)
    new5 = lambda t: t.reshape(NSA_KV_HEADS, HEAD_DIM, DB).transpose(2, 0, 1).reshape(1, DB, 1, NSA_KV_HEADS, HEAD_DIM)
    outs_s = (y_s.reshape(DB, 1, D_MODEL), rs_s.transpose(3, 0, 1, 2)[None],
              new5(ckT), new5(cvT), new5(skT), new5(svT), kv5(nwk), kv5(nwv))

    return (outs_p[0], outs_s[0]) + outs_p[1:] + outs_s[1:]
```

```python
import functools

import numpy as np
import jax
import jax.numpy as jnp
from jax import lax
from jax.experimental import pallas as pl
from jax.experimental.pallas import tpu as pltpu

F32 = jnp.float32
BF16 = jnp.bfloat16
I32 = jnp.int32

D_MODEL = 1024
HEAD_DIM = 64
RET_WIDTH = 512
RET_HEADS = 8
NSA_WIDTH = 512
NSA_HEADS = 8
NSA_KV_HEADS = 2
NSA_GROUP = 4
KV_WIDTH = 128
N_BRANCH = 3
D_FF = 4 * D_MODEL
CMP_BLOCK = 64
N_SELECT = 16
N_PICK = N_SELECT - 1
WINDOW = 512
PAGE_SIZE = 128
RET_CHUNK = 128
ROPE_BASE = 10000.0
EPS = 1e-6
NEG = -1e30
SCALE = HEAD_DIM ** -0.5

MAIN_WIDTH = 4 * RET_WIDTH + NSA_WIDTH + 6 * KV_WIDTH
GATE_PAD = 128
ATT_Q = 128
ATT_K = 256
MAX_BLOCKS = 64
CMP_SAMPLES = 2
SUBLANES = 8
PAGE_TILES = KV_WIDTH // SUBLANES
VMEM_LIMIT = 56 * 1024 * 1024

_LOG_GAMMA = [float(np.log1p(-np.exp2(-5.0 - h))) for h in range(RET_HEADS)]

_KV_NAMES = ("ck", "cv", "sk", "sv", "wk", "wv")
_COLUMNS = {}
_off = 0
for _name, _width in ((("rq", RET_WIDTH), ("rk", RET_WIDTH), ("rv", RET_WIDTH), ("rg", RET_WIDTH),
                       ("nq", NSA_WIDTH)) + tuple((n, KV_WIDTH) for n in _KV_NAMES)):
    _COLUMNS[_name] = (_off, _off + _width)
    _off += _width
assert _off == MAIN_WIDTH


def _params(*sem):
    return pltpu.CompilerParams(dimension_semantics=sem, vmem_limit_bytes=VMEM_LIMIT)


def _dot(a, b):
    return jnp.dot(a, b, preferred_element_type=F32)


def _dot_nt(a, b):
    return lax.dot_general(a, b, (((1,), (1,)), ((), ())), preferred_element_type=F32)


def _dot_tn(a, b):
    return lax.dot_general(a, b, (((0,), (0,)), ((), ())), preferred_element_type=F32)


def _rms(x, w):
    return x * lax.rsqrt(jnp.mean(x * x, axis=-1, keepdims=True) + EPS) * w


def _projector(x_ref, lnw_ref, w_ref, cos_ref, sin_ref):
    h = _rms(x_ref[...], lnw_ref[...]).astype(BF16)
    rows = h.shape[0]

    acc = _dot(h, w_ref[...])

    def proj(name):
        a, b = _COLUMNS[name]
        return acc[:, a:b]

    cos = cos_ref[...]
    sin = sin_ref[...]
    lane = lax.broadcasted_iota(I32, (rows, RET_WIDTH), 1)
    first_half = (lane % HEAD_DIM) < (HEAD_DIM // 2)

    def rope(y):
        partner = jnp.where(first_half,
                            pltpu.roll(y, RET_WIDTH - HEAD_DIM // 2, 1),
                            pltpu.roll(y, HEAD_DIM // 2, 1))
        return y * cos + partner * sin

    return h, proj, rope


def _rope_tables(pos):
    half = HEAD_DIM // 2
    inv = ROPE_BASE ** (-jnp.arange(half, dtype=F32) / half)
    ang = pos.astype(F32)[:, None] * inv
    cos, sin = jnp.cos(ang), jnp.sin(ang)
    cos_h = jnp.concatenate([cos, cos], -1)
    sin_h = jnp.concatenate([-sin, sin], -1)
    return jnp.tile(cos_h, (1, RET_HEADS)), jnp.tile(sin_h, (1, RET_HEADS))


def _inproj_prompt_kernel(x_ref, lnw_ref, w_ref, wg_ref, cos_ref, sin_ref,
                          rq_ref, rk_ref, rv_ref, rg_ref, nqT_ref, ck_ref, cv_ref, sk_ref, wk_ref,
                          ckT_ref, cvT_ref, skT_ref, svT_ref, wkT_ref, wvT_ref, svTb_ref, wvTb_ref, gtT_ref):
    h, proj, rope = _projector(x_ref, lnw_ref, w_ref, cos_ref, sin_ref)
    rq_ref[...] = rope(proj("rq")).astype(BF16)
    rk_ref[...] = (rope(proj("rk")) * SCALE).astype(BF16)
    rv_ref[...] = proj("rv").astype(BF16)
    rg_ref[...] = proj("rg")
    nq = proj("nq") * SCALE
    for n in range(NSA_WIDTH // KV_WIDTH):
        nqT_ref[0, n * KV_WIDTH:(n + 1) * KV_WIDTH, :] = nq[:, n * KV_WIDTH:(n + 1) * KV_WIDTH].T.astype(BF16)
    kv = [proj(name) for name in _KV_NAMES]
    ck_ref[...] = kv[0]
    cv_ref[...] = kv[1]
    sk_ref[...] = kv[2].astype(BF16)
    wk_ref[...] = kv[4].astype(BF16)
    for n, ref in enumerate((ckT_ref, cvT_ref, skT_ref, svT_ref, wkT_ref, wvT_ref)):
        ref[0] = kv[n].T
    svTb_ref[0] = kv[3].T.astype(BF16)
    wvTb_ref[0] = kv[5].T.astype(BF16)
    gtT_ref[0] = jax.nn.sigmoid(_dot(h, wg_ref[...])).T


def _inproj_prompt(x2d, lnw, w_main, w_gate, cos_t, sin_t, B, T, tile):
    rows = x2d.shape[0]
    nt = T // tile
    row = lambda i: (i, 0)
    const = lambda i: (0, 0)
    tab = lambda i: (i % nt, 0)
    colT = lambda i: (i // nt, 0, i % nt)
    rm = lambda w, dt: (pl.BlockSpec((tile, w), row), jax.ShapeDtypeStruct((rows, w), dt))
    tr = lambda w, dt: (pl.BlockSpec((1, w, tile), colT), jax.ShapeDtypeStruct((B, w, T), dt))
    outs = [rm(RET_WIDTH, BF16), rm(RET_WIDTH, BF16), rm(RET_WIDTH, BF16), rm(RET_WIDTH, F32),
            tr(NSA_WIDTH, BF16), rm(KV_WIDTH, F32), rm(KV_WIDTH, F32), rm(KV_WIDTH, BF16), rm(KV_WIDTH, BF16),
            *[tr(KV_WIDTH, F32)] * 6,
            tr(KV_WIDTH, BF16), tr(KV_WIDTH, BF16), tr(GATE_PAD, F32)]
    return pl.pallas_call(
        _inproj_prompt_kernel,
        grid=(rows // tile,),
        in_specs=[pl.BlockSpec((tile, D_MODEL), row),
                  pl.BlockSpec((1, D_MODEL), const),
                  pl.BlockSpec((D_MODEL, MAIN_WIDTH), const),
                  pl.BlockSpec((D_MODEL, GATE_PAD), const),
                  pl.BlockSpec((tile, RET_WIDTH), tab),
                  pl.BlockSpec((tile, RET_WIDTH), tab)],
        out_specs=[o[0] for o in outs],
        out_shape=[o[1] for o in outs],
        compiler_params=_params("parallel"),
        name="inproj_prompt",
    )(x2d, lnw, w_main, w_gate, cos_t, sin_t)


def _inproj_sample_kernel(x_ref, lnw_ref, w_ref, wg_ref, cos_ref, sin_ref,
                          rqT_ref, rkT_ref, rvT_ref, rgT_ref, nq_ref,
                          ckT_ref, cvT_ref, skT_ref, svT_ref, sk_ref, sv_ref, wk_ref, wv_ref, gt_ref):
    h, proj, rope = _projector(x_ref, lnw_ref, w_ref, cos_ref, sin_ref)
    rqT_ref[...] = rope(proj("rq")).T
    rkT_ref[...] = (rope(proj("rk")) * SCALE).T
    rvT_ref[...] = proj("rv").T
    rgT_ref[...] = proj("rg").T
    nq_ref[...] = proj("nq") * SCALE
    kv = [proj(name) for name in _KV_NAMES]
    for n, ref in enumerate((ckT_ref, cvT_ref, skT_ref, svT_ref)):
        ref[...] = kv[n].T
    for n, ref in enumerate((sk_ref, sv_ref, wk_ref, wv_ref)):
        ref[...] = kv[2 + n]
    gt_ref[...] = jax.nn.sigmoid(_dot(h, wg_ref[...]))


def _inproj_sample(x2d, lnw, w_main, w_gate, cos_t, sin_t):
    rows = x2d.shape[0]
    full = lambda a, b: pl.BlockSpec((a, b), lambda i: (0, 0))
    rm = lambda w: (full(rows, w), jax.ShapeDtypeStruct((rows, w), F32))
    tr = lambda w: (full(w, rows), jax.ShapeDtypeStruct((w, rows), F32))
    outs = [tr(RET_WIDTH)] * 4 + [rm(NSA_WIDTH)] + [tr(KV_WIDTH)] * 4 + [rm(KV_WIDTH)] * 4 + [rm(GATE_PAD)]
    return pl.pallas_call(
        _inproj_sample_kernel,
        grid=(1,),
        in_specs=[full(rows, D_MODEL), full(1, D_MODEL), full(D_MODEL, MAIN_WIDTH), full(D_MODEL, GATE_PAD),
                  full(rows, RET_WIDTH), full(rows, RET_WIDTH)],
        out_specs=[o[0] for o in outs],
        out_shape=[o[1] for o in outs],
        compiler_params=_params("arbitrary"),
        name="inproj_sample",
    )(x2d, lnw, w_main, w_gate, cos_t, sin_t)


def _group_norm_gate(outs, g, gnw):
    ys = []
    for o in outs:
        mu = jnp.mean(o, axis=-1, keepdims=True)
        d = o - mu
        var = jnp.mean(d * d, axis=-1, keepdims=True)
        ys.append(d * lax.rsqrt(var + EPS))
    y = jnp.concatenate(ys, axis=1) * gnw
    return g * jax.nn.sigmoid(g) * y


def _ret_prompt_kernel(q_ref, k_ref, v_ref, g_ref, gnw_ref, inner_ref, qdec_ref, kdec_ref,
                       y_ref, s_out_ref, s_scr):
    c = pl.program_id(1)

    @pl.when(c == 0)
    def _():
        s_scr[...] = jnp.zeros_like(s_scr)

    qdec = qdec_ref[...]
    sls = [slice(h * HEAD_DIM, (h + 1) * HEAD_DIM) for h in range(RET_HEADS)]
    q = q_ref[0]
    k = k_ref[0]
    v = v_ref[0]
    kd = (k.astype(F32) * kdec_ref[...]).astype(BF16)
    att = [(_dot_nt(q[:, sl], k[:, sl]) * inner_ref[h]).astype(BF16) for h, sl in enumerate(sls)]
    cross = [_dot(q[:, sl], s_scr[h].astype(BF16)) * qdec[:, sl] for h, sl in enumerate(sls)]
    grow = [_dot_tn(kd[:, sl], v[:, sl]) for sl in sls]
    outs = [_dot(att[h], v[:, sl]) + cross[h] for h, sl in enumerate(sls)]
    for h in range(RET_HEADS):
        s_scr[h] = s_scr[h] * float(np.exp(_LOG_GAMMA[h] * RET_CHUNK)) + grow[h]
    y_ref[0] = _group_norm_gate(outs, g_ref[0], gnw_ref[...]).astype(BF16)

    @pl.when(c == pl.num_programs(1) - 1)
    def _():
        s_out_ref[0] = s_scr[...]


def _ret_prompt(rq, rk, rv, rg, gnw):
    B, T, _ = rq.shape
    C = RET_CHUNK
    lg = jnp.asarray(_LOG_GAMMA, F32)
    i = jnp.arange(C, dtype=F32)
    diff = i[:, None] - i[None, :]
    inner = jnp.where(diff >= 0, jnp.exp(lg[:, None, None] * jnp.maximum(diff, 0.0)), 0.0)
    qdec = jnp.repeat(jnp.exp(lg[None, :] * (i[:, None] + 1.0)), HEAD_DIM, axis=1)
    kdec = jnp.repeat(jnp.exp(lg[None, :] * (C - 1.0 - i[:, None])), HEAD_DIM, axis=1)
    blk = pl.BlockSpec((1, C, RET_WIDTH), lambda b, c: (b, c, 0))
    c2 = lambda b, c: (0, 0)
    return pl.pallas_call(
        _ret_prompt_kernel,
        grid=(B, T // C),
        in_specs=[blk, blk, blk, blk,
                  pl.BlockSpec((1, RET_WIDTH), c2),
                  pl.BlockSpec((RET_HEADS, C, C), lambda b, c: (0, 0, 0)),
                  pl.BlockSpec((C, RET_WIDTH), c2),
                  pl.BlockSpec((C, RET_WIDTH), c2)],
        out_specs=[blk, pl.BlockSpec((1, RET_HEADS, HEAD_DIM, HEAD_DIM), lambda b, c: (b, 0, 0, 0))],
        out_shape=[jax.ShapeDtypeStruct((B, T, RET_WIDTH), BF16),
                   jax.ShapeDtypeStruct((B, RET_HEADS, HEAD_DIM, HEAD_DIM), F32)],
        scratch_shapes=[pltpu.VMEM((RET_HEADS, HEAD_DIM, HEAD_DIM), F32)],
        compiler_params=_params("parallel", "arbitrary"),
        name="ret_prompt",
    )(rq, rk, rv, rg, gnw, inner, qdec, kdec)


def _ret_sample_kernel(q_ref, k_ref, v_ref, g_ref, gnw_ref, gam_ref, s_ref, y_ref, s_out_ref):
    q = q_ref[...]
    k = k_ref[...]
    v = v_ref[...]
    g = g_ref[...]
    gamma = gam_ref[0]
    acc = jnp.zeros(v.shape, F32)
    for d in range(HEAD_DIM):
        s_d = s_ref[0, d]
        acc = acc + s_d * q[d:d + 1, :]
        s_out_ref[0, d] = s_d * gamma + k[d:d + 1, :] * v
    o = acc * gamma + jnp.sum(q * k, axis=0, keepdims=True) * v
    mu = jnp.mean(o, axis=0, keepdims=True)
    dlt = o - mu
    var = jnp.mean(dlt * dlt, axis=0, keepdims=True)
    y_ref[...] = g * jax.nn.sigmoid(g) * (dlt * lax.rsqrt(var + EPS) * gnw_ref[...])


def _ret_sample(rqT, rkT, rvT, rgT, gnw, state_t):
    DB = rqT.shape[1]
    blk = pl.BlockSpec((HEAD_DIM, DB), lambda h: (h, 0))
    st = pl.BlockSpec((1, HEAD_DIM, HEAD_DIM, DB), lambda h: (h, 0, 0, 0))
    gam = jnp.broadcast_to(jnp.exp(jnp.asarray(_LOG_GAMMA, F32))[:, None, None], (RET_HEADS, 1, DB))
    return pl.pallas_call(
        _ret_sample_kernel,
        grid=(RET_HEADS,),
        in_specs=[blk, blk, blk, blk,
                  pl.BlockSpec((HEAD_DIM, 1), lambda h: (h, 0)),
                  pl.BlockSpec((1, 1, DB), lambda h: (h, 0, 0)),
                  st],
        out_specs=[blk, st],
        out_shape=[jax.ShapeDtypeStruct((RET_WIDTH, DB), F32),
                   jax.ShapeDtypeStruct(state_t.shape, F32)],
        compiler_params=_params("parallel"),
        name="ret_sample",
    )(rqT, rkT, rvT, rgT, gnw.reshape(RET_WIDTH, 1), gam, state_t)


def _compress(load_rows, pos_ref, w1_ref, w2_ref):
    acc = None
    for j2 in range(CMP_BLOCK // 2):
        x = jnp.concatenate([load_rows(2 * j2) + pos_ref[2 * j2:2 * j2 + 1, :],
                             load_rows(2 * j2 + 1) + pos_ref[2 * j2 + 1:2 * j2 + 2, :]], axis=1).astype(BF16)
        part = _dot(x, w1_ref[j2])
        acc = part if acc is None else acc + part
    h = jax.nn.gelu(acc)
    return _dot(h.astype(BF16), w2_ref[...])


def _block_diag2(w):
    z = jnp.zeros_like(w)
    return jnp.concatenate([jnp.concatenate([w, z], -1), jnp.concatenate([z, w], -1)], -2)


def _compress_weights(pos, w1, w2, token_major):
    if not token_major:
        pos, w1 = pos.T, w1.transpose(1, 0, 2)
    pos2 = jnp.concatenate([pos, pos], -1)
    w1p = _block_diag2(w1).reshape(CMP_BLOCK // 2, 2 * KV_WIDTH, KV_WIDTH)
    return pos2, w1p.astype(BF16), _block_diag2(w2).astype(BF16)


def _rank_rows(imp, rank_scr, last_live):
    nb, tq = imp.shape
    sub = SUBLANES
    n_groups = nb // sub
    groups = [imp[sub * j:sub * (j + 1)] for j in range(n_groups)]
    row_in_group = lax.broadcasted_iota(I32, (sub, tq), 0)
    rank_scr[...] = jnp.zeros((nb, tq), I32)
    for jm in range(n_groups):
        @pl.when(jm * sub <= last_live)
        def _():
            ranks = [rank_scr[sub * j:sub * (j + 1), :] for j in range(n_groups)]
            for sm in range(sub):
                row = imp[jm * sub + sm:jm * sub + sm + 1, :]
                for j, grp in enumerate(groups):
                    if j < jm:
                        inc = jnp.where(row > grp, 1, 0)
                    elif j > jm:
                        inc = jnp.where(row >= grp, 1, 0)
                    else:
                        inc = jnp.where(row_in_group > sm, jnp.where(row >= grp, 1, 0),
                                        jnp.where(row > grp, 1, 0))
                    ranks[j] = ranks[j] + inc
            for j in range(n_groups):
                rank_scr[sub * j:sub * (j + 1), :] = ranks[j]
    return rank_scr[...]


def _kv_head_rows(x, g):
    z = jnp.zeros_like(x)
    return jnp.concatenate([x, z] if g == 0 else [z, x], axis=0)


def _cmp_prompt_kernel(ck_ref, cv_ref, posk_ref, w1k_ref, w2k_ref, posv_ref, w1v_ref, w2v_ref,
                       nqT_ref, gtT_ref, ocmpT_ref, biasT_ref, kc_scr, vc_scr, rank_scr):
    i = pl.program_id(1)
    nb = kc_scr.shape[0]
    tq = nqT_ref.shape[2]

    @pl.when(i == 0)
    def _():
        kc_scr[...] = _compress(lambda l: ck_ref[0, pl.ds(l, nb, stride=CMP_BLOCK), :],
                                posk_ref, w1k_ref, w2k_ref).astype(BF16)
        vc_scr[...] = _compress(lambda l: cv_ref[0, pl.ds(l, nb, stride=CMP_BLOCK), :],
                                posv_ref, w1v_ref, w2v_ref).astype(BF16)

    kc = kc_scr[...]
    vc = vc_scr[...]
    pos = i * tq + lax.broadcasted_iota(I32, (nb, tq), 1)
    n_idx = lax.broadcasted_iota(I32, (nb, tq), 0)
    valid = (n_idx + 1) * CMP_BLOCK - 1 <= pos
    cur = pos // CMP_BLOCK
    cand = n_idx < cur
    for g in range(NSA_KV_HEADS):
        qT = jnp.concatenate(
            [_kv_head_rows(nqT_ref[0, (g * NSA_GROUP + r) * HEAD_DIM:(g * NSA_GROUP + r + 1) * HEAD_DIM, :], g)
             for r in range(NSA_GROUP)], axis=1)
        valid4 = jnp.concatenate([valid] * NSA_GROUP, axis=1)
        s = jnp.where(valid4, _dot(kc, qT), NEG)
        e = jnp.exp(s - jnp.max(s, axis=0, keepdims=True))
        p = jnp.where(valid4, e / jnp.sum(e, axis=0, keepdims=True), 0.0)
        oT = _dot_tn(vc, p.astype(BF16))[g * HEAD_DIM:(g + 1) * HEAD_DIM]
        imp = jnp.zeros((nb, tq), F32)
        for r in range(NSA_GROUP):
            hh = g * NSA_GROUP + r
            imp = imp + p[:, r * tq:(r + 1) * tq]
            ocmpT_ref[0, hh * HEAD_DIM:(hh + 1) * HEAD_DIM, :] = (
                oT[:, r * tq:(r + 1) * tq] * gtT_ref[0, N_BRANCH * hh:N_BRANCH * hh + 1, :])
        imp = jnp.where(cand, imp, -1.0)
        rank = _rank_rows(imp, rank_scr, (i * tq + tq - 1) // CMP_BLOCK)
        sel = (cand & (rank < N_PICK)) | (n_idx == cur)
        biasT_ref[0, g * MAX_BLOCKS:g * MAX_BLOCKS + nb, :] = jnp.where(sel, 0.0, NEG).astype(BF16)
        if nb < MAX_BLOCKS:
            biasT_ref[0, g * MAX_BLOCKS + nb:(g + 1) * MAX_BLOCKS, :] = jnp.zeros((MAX_BLOCKS - nb, tq), BF16)


def _cmp_weight_specs(c2, c3):
    return [pl.BlockSpec((CMP_BLOCK, KV_WIDTH), c2),
            pl.BlockSpec((CMP_BLOCK // 2, 2 * KV_WIDTH, KV_WIDTH), c3),
            pl.BlockSpec((KV_WIDTH, KV_WIDTH), c2)]


def _cmp_prompt(ck, cv, cwk, cwv, nqT, gtT, tq):
    B, T, _ = ck.shape
    nb = T // CMP_BLOCK
    full = pl.BlockSpec((1, T, KV_WIDTH), lambda b, i: (b, 0, 0))
    wspecs = _cmp_weight_specs(lambda b, i: (0, 0), lambda b, i: (0, 0, 0))
    blk = lambda w: pl.BlockSpec((1, w, tq), lambda b, i: (b, 0, i))
    return pl.pallas_call(
        _cmp_prompt_kernel,
        grid=(B, T // tq),
        in_specs=[full, full] + wspecs + wspecs + [blk(NSA_WIDTH), blk(GATE_PAD)],
        out_specs=[blk(NSA_WIDTH), blk(NSA_KV_HEADS * MAX_BLOCKS)],
        out_shape=[jax.ShapeDtypeStruct((B, NSA_WIDTH, T), F32),
                   jax.ShapeDtypeStruct((B, NSA_KV_HEADS * MAX_BLOCKS, T), BF16)],
        scratch_shapes=[pltpu.VMEM((nb, KV_WIDTH), BF16), pltpu.VMEM((nb, KV_WIDTH), BF16),
                        pltpu.VMEM((nb, tq), I32)],
        compiler_params=_params("parallel", "arbitrary"),
        name="cmp_prompt",
    )(ck, cv, *cwk, *cwv, nqT, gtT)


def _att_prompt_kernel(nqT_ref, biasT_ref, sk_ref, svT_ref, wk_ref, wvT_ref, gtT_ref, ocmpT_ref,
                       y_ref, ka_scr, va_scr, wva_scr, s_scr, m_scr, acc_scr):
    i = pl.program_id(1)
    tq = nqT_ref.shape[2]
    T = sk_ref.shape[1]
    cols = NSA_GROUP * tq

    @pl.when(i == 0)
    def _():
        sk = sk_ref[0].astype(F32)
        lane = lax.broadcasted_iota(I32, (T, KV_WIDTH), 1)
        blk = lax.broadcasted_iota(I32, (T, KV_WIDTH), 0) // CMP_BLOCK
        ka_scr[0] = jnp.where(lane < HEAD_DIM, sk, jnp.where(lane - HEAD_DIM == blk, 1.0, 0.0)).astype(BF16)
        ka_scr[1] = jnp.where(lane >= HEAD_DIM, sk, jnp.where(lane == blk, 1.0, 0.0)).astype(BF16)
        svT = svT_ref[0].astype(F32)
        row = lax.broadcasted_iota(I32, (KV_WIDTH, T), 0)
        va_scr[0] = jnp.where(row < HEAD_DIM, svT, 1.0).astype(BF16)
        va_scr[1] = jnp.where(row >= HEAD_DIM, svT, 1.0).astype(BF16)
        wvT_all = wvT_ref[0].astype(F32)
        wva_scr[0] = jnp.where(row < HEAD_DIM, wvT_all, 1.0).astype(BF16)
        wva_scr[1] = jnp.where(row >= HEAD_DIM, wvT_all, 1.0).astype(BF16)

    q_pos = i * tq + lax.broadcasted_iota(I32, (ATT_K, cols), 1) % tq
    k_off = lax.broadcasted_iota(I32, (ATT_K, cols), 0)
    n_win = WINDOW // tq
    win_keys = (n_win + 1) * tq
    win_start = pl.multiple_of(jnp.maximum(i - n_win, 0) * tq, tq)
    rel = i * tq - win_start + lax.broadcasted_iota(I32, (win_keys, tq), 1)
    j_win = lax.broadcasted_iota(I32, (win_keys, tq), 0)
    win_bias = jnp.where((j_win <= rel) & (j_win > rel - WINDOW), 0.0, NEG)
    win_bias = jnp.concatenate([win_bias] * NSA_GROUP, axis=1)
    wk = wk_ref[0, pl.ds(win_start, win_keys), :]

    qaTs, qwTs = [], []
    for g in range(NSA_KV_HEADS):
        qTs = [nqT_ref[0, (g * NSA_GROUP + r) * HEAD_DIM:(g * NSA_GROUP + r + 1) * HEAD_DIM, :]
               for r in range(NSA_GROUP)]
        bT = biasT_ref[0, g * MAX_BLOCKS:(g + 1) * MAX_BLOCKS, :]
        qaTs.append(jnp.concatenate([jnp.concatenate([q, bT] if g == 0 else [bT, q], axis=0) for q in qTs],
                                    axis=1))
        qwTs.append(jnp.concatenate([_kv_head_rows(q, g) for q in qTs], axis=1))

    def scores_into(slot, c):
        start = pl.multiple_of(c * ATT_K, ATT_K)
        for g in range(NSA_KV_HEADS):
            s_scr[slot, g] = _dot(ka_scr[g, pl.ds(start, ATT_K), :], qaTs[g])

    def sel_step(c, slot, last):
        start = pl.multiple_of(c * ATT_K, ATT_K)
        if not last:
            scores_into(1 - slot, c + 1)
        for g in range(NSA_KV_HEADS):
            s = s_scr[slot, g]
            if last:
                s = jnp.where(start + k_off > q_pos, NEG, s)
            m = m_scr[g]
            m_new = jnp.maximum(m, jnp.max(s, axis=0, keepdims=True))
            m_scr[g] = m_new
            p = jnp.exp(s - m_new)
            pv = _dot(va_scr[g, :, pl.ds(start, ATT_K)], p.astype(BF16))
            acc_scr[g] = jnp.exp(m - m_new) * acc_scr[g] + pv

    m_scr[...] = jnp.full(m_scr.shape, NEG, F32)
    acc_scr[...] = jnp.zeros(acc_scr.shape, F32)
    scores_into(0, 0)
    n_before = (i * tq) // ATT_K

    def pair(pr, carry):
        sel_step(2 * pr, 0, False)
        sel_step(2 * pr + 1, 1, False)
        return carry

    lax.fori_loop(0, n_before // 2, pair, 0)

    @pl.when(n_before % 2 == 1)
    def _():
        sel_step(n_before - 1, 0, False)

    sel_step(n_before, n_before % 2, True)

    heads_out = []
    for g in range(NSA_KV_HEADS):
        gs = slice(g * HEAD_DIM, (g + 1) * HEAD_DIM)
        acc = acc_scr[g]
        ones_row = (1 - g) * HEAD_DIM
        o_sel = acc[gs] / acc[ones_row:ones_row + 1]

        s = _dot(wk, qwTs[g]) + win_bias
        p = jnp.exp(s - jnp.max(s, axis=0, keepdims=True))
        acc_w = _dot(wva_scr[g, :, pl.ds(win_start, win_keys)], p.astype(BF16))
        o_win = acc_w[gs] / acc_w[ones_row:ones_row + 1]

        for r in range(NSA_GROUP):
            hh = g * NSA_GROUP + r
            cs = slice(r * tq, (r + 1) * tq)
            heads_out.append(ocmpT_ref[0, hh * HEAD_DIM:(hh + 1) * HEAD_DIM, :]
                             + gtT_ref[0, N_BRANCH * hh + 1:N_BRANCH * hh + 2, :] * o_sel[:, cs]
                             + gtT_ref[0, N_BRANCH * hh + 2:N_BRANCH * hh + 3, :] * o_win[:, cs])
    y_ref[0] = jnp.concatenate(heads_out, axis=0).T.astype(BF16)


def _att_prompt(nqT, biasT, sk, svT, wk, wvT, gtT, ocmpT):
    B, _, T = nqT.shape
    tq = ATT_Q
    rows_full = pl.BlockSpec((1, T, KV_WIDTH), lambda b, i: (b, 0, 0))
    cols_full = pl.BlockSpec((1, KV_WIDTH, T), lambda b, i: (b, 0, 0))
    blk = lambda w: pl.BlockSpec((1, w, tq), lambda b, i: (b, 0, i))
    return pl.pallas_call(
        _att_prompt_kernel,
        grid=(B, T // tq),
        in_specs=[blk(NSA_WIDTH), blk(NSA_KV_HEADS * MAX_BLOCKS), rows_full, cols_full, rows_full, cols_full,
                  blk(GATE_PAD), blk(NSA_WIDTH)],
        out_specs=pl.BlockSpec((1, tq, NSA_WIDTH), lambda b, i: (b, i, 0)),
        out_shape=jax.ShapeDtypeStruct((B, T, NSA_WIDTH), BF16),
        scratch_shapes=[pltpu.VMEM((NSA_KV_HEADS, T, KV_WIDTH), BF16),
                        pltpu.VMEM((NSA_KV_HEADS, KV_WIDTH, T), BF16),
                        pltpu.VMEM((NSA_KV_HEADS, KV_WIDTH, T), BF16),
                        pltpu.VMEM((2, NSA_KV_HEADS, ATT_K, NSA_GROUP * tq), F32),
                        pltpu.VMEM((NSA_KV_HEADS, 1, NSA_GROUP * tq), F32),
                        pltpu.VMEM((NSA_KV_HEADS, KV_WIDTH, NSA_GROUP * tq), F32)],
        compiler_params=_params("parallel", "arbitrary"),
        name="att_prompt",
    )(nqT, biasT, sk, svT, wk, wvT, gtT, ocmpT)


def _finish_kernel(x_ref, ry_ref, ny_ref, wo_ref, l1_ref, l2_ref, l3_ref, wu_ref, wd_ref, o_ref, *, ret_t):
    if ret_t:
        mix = _dot_tn(ry_ref[...].astype(BF16), wo_ref[0:RET_WIDTH, :])
    else:
        mix = _dot(ry_ref[...], wo_ref[0:RET_WIDTH, :])
    mix = mix + _dot(ny_ref[...].astype(BF16), wo_ref[RET_WIDTH:, :])
    h = x_ref[...] + _rms(mix, l1_ref[...])
    u = _dot(_rms(h, l2_ref[...]).astype(BF16), wu_ref[...])
    a = jnp.square(jnp.maximum(u, 0.0)).astype(BF16)
    f = _dot(a, wd_ref[...])
    o_ref[...] = h + _rms(f, l3_ref[...])


def _finish(x2d, ret_y, nsa_y, w_o, l1, l2, l3, w_up, w_down, tile, ret_t=False):
    rows = x2d.shape[0]
    row = lambda w: pl.BlockSpec((tile, w), lambda i: (i, 0))
    const = lambda a, b: pl.BlockSpec((a, b), lambda i: (0, 0), pipeline_mode=pl.Buffered(1))
    ret_spec = pl.BlockSpec((RET_WIDTH, tile), lambda i: (0, i)) if ret_t else row(RET_WIDTH)
    return pl.pallas_call(
        functools.partial(_finish_kernel, ret_t=ret_t),
        grid=(rows // tile,),
        in_specs=[row(D_MODEL), ret_spec, row(NSA_WIDTH),
                  const(D_MODEL, D_MODEL), const(1, D_MODEL), const(1, D_MODEL), const(1, D_MODEL),
                  const(D_MODEL, D_FF), const(D_FF, D_MODEL)],
        out_specs=row(D_MODEL),
        out_shape=jax.ShapeDtypeStruct((rows, D_MODEL), F32),
        compiler_params=_params("parallel"),
        name="finish",
    )(x2d, ret_y, nsa_y, w_o, l1, l2, l3, w_up, w_down)


def _column(row):
    n = row.shape[1]
    eye = lax.broadcasted_iota(I32, (n, n), 0) == lax.broadcasted_iota(I32, (n, n), 1)
    return jnp.sum(jnp.where(eye, row, 0.0), axis=1, keepdims=True)


def _cmp_sample_kernel(pt_ref, nq_ref, poolk_ref, poolv_ref,
                       posk_ref, w1k_ref, w2k_ref, posv_ref, w1v_ref, w2v_ref,
                       ocmp_ref, idx_ref, kbuf, vbuf, ksem, vsem):
    step = pl.program_id(0)
    n_pages = pt_ref.shape[1]
    nbk = 2 * n_pages
    slot = step % 2

    def fetch(st, sl, start):
        for j in range(CMP_SAMPLES):
            for p in range(n_pages):
                tiles = pl.ds(((sl * CMP_SAMPLES + j) * n_pages + p) * PAGE_TILES, PAGE_TILES)
                page = pt_ref[st * CMP_SAMPLES + j, p]
                for pool, buf, sem in ((poolk_ref, kbuf, ksem), (poolv_ref, vbuf, vsem)):
                    cp = pltpu.make_async_copy(pool.at[page], buf.at[tiles], sem.at[sl])
                    cp.start() if start else cp.wait()

    @pl.when(step == 0)
    def _():
        fetch(0, 0, True)

    @pl.when(step + 1 < pl.num_programs(0))
    def _():
        fetch(step + 1, 1 - slot, True)

    fetch(step, slot, False)

    step_pages = CMP_SAMPLES * n_pages
    base = slot * step_pages * PAGE_TILES
    group_tiles = HEAD_DIM // SUBLANES

    def rows_of(buf):
        cache = {}

        def load(d):
            fg, s = divmod(d, SUBLANES)
            if fg not in cache:
                cache.clear()
                cache[fg] = [jnp.swapaxes(buf[pl.ds(base + g * group_tiles + fg, step_pages, stride=PAGE_TILES)],
                                          0, 1) for g in range(NSA_KV_HEADS)]
            return jnp.concatenate([t[s] for t in cache[fg]], axis=0)
        return load

    kc = _compress(rows_of(kbuf), posk_ref, w1k_ref, w2k_ref).astype(BF16)
    vc = _compress(rows_of(vbuf), posv_ref, w1v_ref, w2v_ref).astype(BF16)

    j_lane = lax.broadcasted_iota(I32, (nbk, nbk), 1)
    j_sub = lax.broadcasted_iota(I32, (nbk, nbk), 0)
    n_lane = 2 * (j_lane % n_pages) + j_lane // n_pages
    n_sub = 2 * (j_sub % n_pages) + j_sub // n_pages
    z = jnp.zeros((1, HEAD_DIM), F32)
    for j in range(CMP_SAMPLES):
        q = nq_ref[j]
        o_rows, idx_rows = [], []
        for g in range(NSA_KV_HEADS):
            blk_rows = slice((g * CMP_SAMPLES + j) * n_pages, (g * CMP_SAMPLES + j + 1) * n_pages)
            pieces = [q[:, (g * NSA_GROUP + r) * HEAD_DIM:(g * NSA_GROUP + r + 1) * HEAD_DIM]
                      for r in range(NSA_GROUP)]
            qh = jnp.concatenate([jnp.concatenate([pc, z], axis=1) for pc in pieces]
                                 + [jnp.concatenate([z, pc], axis=1) for pc in pieces], axis=0)
            s = _dot_nt(qh.astype(BF16), kc[blk_rows])
            m = jnp.max(s, axis=1, keepdims=True)
            m = jnp.maximum(m[:NSA_GROUP], m[NSA_GROUP:])
            e = jnp.exp(s - jnp.concatenate([m, m], axis=0))
            l = jnp.sum(e, axis=1, keepdims=True)
            l = l[:NSA_GROUP] + l[NSA_GROUP:]
            p = e / jnp.concatenate([l, l], axis=0)
            out = _dot(p.astype(BF16), vc[blk_rows])
            o_rows.append(out[:NSA_GROUP, :HEAD_DIM] + out[NSA_GROUP:, HEAD_DIM:])
            imp = jnp.concatenate([jnp.sum(p[:NSA_GROUP], axis=0, keepdims=True),
                                   jnp.sum(p[NSA_GROUP:], axis=0, keepdims=True)], axis=1)
            imp_col = _column(imp)
            ahead = (imp > imp_col) | ((imp == imp_col) & (n_lane < n_sub))
            rank_col = jnp.sum(jnp.where(ahead, 1, 0), axis=1, keepdims=True)
            idx_rows.append(jnp.sum(jnp.where(rank_col == j_lane, n_sub, 0), axis=0, keepdims=True))
        ocmp_ref[j] = jnp.concatenate([o[r:r + 1] for o in o_rows for r in range(NSA_GROUP)], axis=1)
        idx_ref[j] = jnp.concatenate(idx_rows, axis=0)


def _cmp_sample(page_table, nq, pool_k, pool_v, cwk, cwv):
    DB, n_pages = page_table.shape
    nbk = 2 * n_pages
    wspecs = _cmp_weight_specs(lambda s, pt: (0, 0), lambda s, pt: (0, 0, 0))
    any_spec = pl.BlockSpec(memory_space=pl.ANY)
    tiled = lambda t: t.reshape(t.shape[0], PAGE_TILES, SUBLANES, PAGE_SIZE)
    buf_shape = (2 * CMP_SAMPLES * n_pages * PAGE_TILES, SUBLANES, PAGE_SIZE)
    grid_spec = pltpu.PrefetchScalarGridSpec(
        num_scalar_prefetch=1,
        grid=(DB // CMP_SAMPLES,),
        in_specs=[pl.BlockSpec((CMP_SAMPLES, 1, NSA_WIDTH), lambda s, pt: (s, 0, 0)), any_spec, any_spec]
        + wspecs + wspecs,
        out_specs=[pl.BlockSpec((CMP_SAMPLES, 1, NSA_WIDTH), lambda s, pt: (s, 0, 0)),
                   pl.BlockSpec((CMP_SAMPLES, NSA_KV_HEADS, nbk), lambda s, pt: (s, 0, 0))],
        scratch_shapes=[pltpu.VMEM(buf_shape, F32),
                        pltpu.VMEM(buf_shape, F32),
                        pltpu.SemaphoreType.DMA((2,)),
                        pltpu.SemaphoreType.DMA((2,))],
    )
    pool_k, pool_v = tiled(pool_k), tiled(pool_v)
    return pl.pallas_call(
        _cmp_sample_kernel,
        grid_spec=grid_spec,
        out_shape=[jax.ShapeDtypeStruct((DB, 1, NSA_WIDTH), F32),
                   jax.ShapeDtypeStruct((DB, NSA_KV_HEADS, nbk), I32)],
        compiler_params=_params("arbitrary"),
        name="cmp_sample",
    )(page_table, nq.reshape(DB, 1, NSA_WIDTH), pool_k, pool_v, *cwk, *cwv)


def _att_sample_kernel(idx_ref, pt_ref, nq_ref, knew_ref, vnew_ref, wknew_ref, wvnew_ref,
                       gt_ref, ocmp_ref, poolk_ref, poolv_ref, wkT_ref, wvT_ref,
                       y_ref, wk_out_ref, wv_out_ref, kbuf, vbuf, ksem, vsem, *, past_len):
    b = pl.program_id(0)
    slot = b % 2

    def gather(bb, sl, start):
        for g in range(NSA_KV_HEADS):
            for s in range(N_PICK):
                blk = idx_ref[(bb * NSA_KV_HEADS + g) * N_SELECT + s]
                page = pt_ref[bb, blk // 2]
                for pool, buf, sem in ((poolk_ref, kbuf, ksem), (poolv_ref, vbuf, vsem)):
                    cp = pltpu.make_async_copy(pool.at[page, pl.ds(g * HEAD_DIM, HEAD_DIM), :],
                                               buf.at[sl, g, :, pl.ds(s * PAGE_SIZE, PAGE_SIZE)], sem.at[sl])
                    cp.start() if start else cp.wait()

    @pl.when(b == 0)
    def _():
        gather(0, 0, True)

    @pl.when(b + 1 < pl.num_programs(0))
    def _():
        gather(b + 1, 1 - slot, True)

    q = nq_ref[0]
    pieces = [q[:, hh * HEAD_DIM:(hh + 1) * HEAD_DIM] for hh in range(NSA_HEADS)]
    z = jnp.zeros((1, HEAD_DIM), F32)
    q_wide = jnp.concatenate([jnp.concatenate([pc, z] if hh < NSA_GROUP else [z, pc], axis=1)
                              for hh, pc in enumerate(pieces)], axis=0)

    def bf_round(x):
        return x.astype(BF16).astype(F32)

    def attend(qf, s_past, vT_past, k_new, v_new):
        s_self = jnp.sum(bf_round(qf) * bf_round(k_new), axis=1, keepdims=True)
        m = jnp.maximum(jnp.max(s_past, axis=-1, keepdims=True), s_self)
        p = jnp.exp(s_past - m)
        p_self = jnp.exp(s_self - m)
        l = jnp.sum(p, axis=-1, keepdims=True) + p_self
        o = _dot_nt(p.astype(BF16), vT_past) + bf_round(p_self) * bf_round(v_new)
        return o / l

    wkT = wkT_ref[0]
    wvT = wvT_ref[0]
    wb = wkT.shape[1]
    j = lax.broadcasted_iota(I32, (NSA_HEADS, wb), 1)
    pos_k = past_len - wb + j
    keep = (past_len - pos_k < WINDOW) & (pos_k >= 0)
    s_win = jnp.where(keep, _dot(q_wide.astype(BF16), wkT.astype(BF16)), NEG)
    o_win = attend(q_wide, s_win, wvT.astype(BF16), wknew_ref[0], wvnew_ref[0])

    lane_w = lax.broadcasted_iota(I32, (KV_WIDTH, wb), 1)
    wk_out_ref[0] = jnp.where(lane_w == wb - 1, _column(wknew_ref[0]), pltpu.roll(wkT, wb - 1, 1))
    wv_out_ref[0] = jnp.where(lane_w == wb - 1, _column(wvnew_ref[0]), pltpu.roll(wvT, wb - 1, 1))

    gather(b, slot, False)
    n_keys = N_PICK * PAGE_SIZE
    half_of_lane = (lax.broadcasted_iota(I32, (1, n_keys), 1) % PAGE_SIZE) // CMP_BLOCK
    zeros4 = jnp.zeros((NSA_GROUP, HEAD_DIM), F32)
    gt = gt_ref[0]
    ocmp = ocmp_ref[0]
    y_pieces = []
    for g in range(NSA_KV_HEADS):
        gs = slice(g * HEAD_DIM, (g + 1) * HEAD_DIM)
        want = jnp.concatenate(
            [jnp.full((1, PAGE_SIZE), idx_ref[(b * NSA_KV_HEADS + g) * N_SELECT + s] % 2, I32)
             for s in range(N_PICK)], axis=1)
        qg = jnp.concatenate(pieces[g * NSA_GROUP:(g + 1) * NSA_GROUP] + [zeros4], axis=0)
        s_sel = jnp.where(half_of_lane == want, _dot(qg.astype(BF16), kbuf[slot, g].astype(BF16)), NEG)
        o_sel = attend(qg, s_sel, vbuf[slot, g].astype(BF16), knew_ref[0][:, gs], vnew_ref[0][:, gs])
        for r in range(NSA_GROUP):
            hh = g * NSA_GROUP + r
            y_pieces.append(gt[:, N_BRANCH * hh:N_BRANCH * hh + 1] * ocmp[:, hh * HEAD_DIM:(hh + 1) * HEAD_DIM]
                            + gt[:, N_BRANCH * hh + 1:N_BRANCH * hh + 2] * o_sel[r:r + 1]
                            + gt[:, N_BRANCH * hh + 2:N_BRANCH * hh + 3] * o_win[hh:hh + 1, gs])
    y_ref[0] = jnp.concatenate(y_pieces, axis=1)


def _att_sample(idx, page_table, nq, sk, sv, wk_new, wv_new, gates, ocmp, pool_k, pool_v, buf_k, buf_v):
    DB, n_pages = page_table.shape
    past_len = n_pages * PAGE_SIZE
    wb = buf_k.shape[2]
    r3 = lambda t: t.reshape(DB, 1, t.shape[-1])
    row = lambda w: pl.BlockSpec((1, 1, w), lambda b, ix, pt: (b, 0, 0))
    any_spec = pl.BlockSpec(memory_space=pl.ANY)
    wspec = pl.BlockSpec((1, KV_WIDTH, wb), lambda b, ix, pt: (b, 0, 0))
    buf_shape = (2, NSA_KV_HEADS, HEAD_DIM, N_PICK * PAGE_SIZE)
    grid_spec = pltpu.PrefetchScalarGridSpec(
        num_scalar_prefetch=2,
        grid=(DB,),
        in_specs=[row(NSA_WIDTH), row(KV_WIDTH), row(KV_WIDTH), row(KV_WIDTH), row(KV_WIDTH),
                  row(GATE_PAD), row(NSA_WIDTH), any_spec, any_spec, wspec, wspec],
        out_specs=[row(NSA_WIDTH), wspec, wspec],
        scratch_shapes=[pltpu.VMEM(buf_shape, F32), pltpu.VMEM(buf_shape, F32),
                        pltpu.SemaphoreType.DMA((2,)), pltpu.SemaphoreType.DMA((2,))],
    )
    return pl.pallas_call(
        functools.partial(_att_sample_kernel, past_len=past_len),
        grid_spec=grid_spec,
        out_shape=[jax.ShapeDtypeStruct((DB, 1, NSA_WIDTH), F32),
                   jax.ShapeDtypeStruct(buf_k.shape, F32),
                   jax.ShapeDtypeStruct(buf_v.shape, F32)],
        compiler_params=_params("arbitrary"),
        name="att_sample",
    )(idx, page_table, r3(nq), r3(sk), r3(sv), r3(wk_new), r3(wv_new), r3(gates), ocmp,
      pool_k, pool_v, buf_k, buf_v)


def _row_tile(rows, want=256):
    return next(t for t in (want, 256, 128) if rows % t == 0)


def kernel(x_prompt, x_sample, state_ret, cache_cmp_k, cache_cmp_v, cache_sel_k, cache_sel_v,
           cache_win_k, cache_win_v, page_table, ln_pre_mix, w_in, ret_gn_w,
           cmp_pos_k, cmp_w1_k, cmp_w2_k, cmp_pos_v, cmp_w1_v, cmp_w2_v,
           w_o, ln_post_mix, ln_pre_ffn, w_up, w_down, ln_post_ffn):
    B, T, _ = x_prompt.shape
    DB, Tn, _ = x_sample.shape
    n_pages = page_table.shape[1]
    past_len = n_pages * PAGE_SIZE
    wb = cache_win_k.shape[2]
    assert w_in.shape[0] == 1 and Tn == 1
    assert T % ATT_K == 0 and T >= WINDOW + ATT_Q and T // CMP_BLOCK <= MAX_BLOCKS
    assert 2 * n_pages >= N_PICK and DB % CMP_SAMPLES == 0

    w_main = w_in[0, :, :MAIN_WIDTH].astype(BF16)
    w_gate = jnp.pad(w_in[0, :, MAIN_WIDTH:], ((0, 0), (0, GATE_PAD - NSA_HEADS * N_BRANCH))).astype(BF16)
    tail = (w_o[0].astype(BF16), ln_post_mix, ln_pre_ffn, ln_post_ffn, w_up[0].astype(BF16), w_down[0].astype(BF16))

    tile_p = _row_tile(T, 512)
    cos_p, sin_p = _rope_tables(jnp.arange(T, dtype=I32))
    xp = x_prompt.reshape(B * T, D_MODEL)
    (rq, rk, rv, rg, nqT, ck, cv, sk, wk, ckT, cvT, skT, svT, wkT, wvT, svTb, wvTb, gtT) = _inproj_prompt(
        xp, ln_pre_mix, w_main, w_gate, cos_p, sin_p, B, T, tile_p)
    b3 = lambda t: t.reshape(B, T, t.shape[-1])
    ret_y, rs_p = _ret_prompt(b3(rq), b3(rk), b3(rv), b3(rg), ret_gn_w)
    cwk = _compress_weights(cmp_pos_k[0], cmp_w1_k[0], cmp_w2_k[0], True)
    cwv = _compress_weights(cmp_pos_v[0], cmp_w1_v[0], cmp_w2_v[0], True)
    ocmpT, biasT = _cmp_prompt(b3(ck), b3(cv), cwk, cwv, nqT, gtT, _row_tile(T, 512))
    nsa_y = _att_prompt(nqT, biasT, b3(sk), svTb, b3(wk), wvTb, gtT, ocmpT)
    y_p = _finish(xp, ret_y.reshape(B * T, RET_WIDTH), nsa_y.reshape(B * T, NSA_WIDTH), *tail,
                  _row_tile(B * T, 512))
    win_p = min(WINDOW, T)
    kv5 = lambda t: t.reshape(t.shape[0], NSA_KV_HEADS, HEAD_DIM, t.shape[-1]).transpose(0, 3, 1, 2)[None]
    outs_p = (y_p.reshape(B, T, D_MODEL), rs_p[None],
              kv5(ckT), kv5(cvT), kv5(skT), kv5(svT),
              kv5(wkT[:, :, T - win_p:]), kv5(wvT[:, :, T - win_p:]))

    cos_s, sin_s = _rope_tables(jnp.full((DB,), past_len, I32))
    xs = x_sample.reshape(DB, D_MODEL)
    (rqT, rkT, rvT, rgT, nq, ckT, cvT, skT, svT, sk, sv, wk, wv, gt) = _inproj_sample(
        xs, ln_pre_mix, w_main, w_gate, cos_s, sin_s)
    ret_yT, rs_s = _ret_sample(rqT, rkT, rvT, rgT, ret_gn_w, state_ret[0].transpose(1, 2, 3, 0))
    feat_major = lambda t: t[0].transpose(0, 2, 3, 1).reshape(t.shape[1], KV_WIDTH, t.shape[2])
    cwk = _compress_weights(cmp_pos_k[0], cmp_w1_k[0], cmp_w2_k[0], False)
    cwv = _compress_weights(cmp_pos_v[0], cmp_w1_v[0], cmp_w2_v[0], False)
    ocmp, idx = _cmp_sample(page_table, nq, feat_major(cache_cmp_k), feat_major(cache_cmp_v), cwk, cwv)
    idx_flat = idx[:, :, :N_SELECT].reshape(-1)
    y_row, nwk, nwv = _att_sample(idx_flat, page_table, nq, sk, sv, wk, wv, gt, ocmp,
                                  feat_major(cache_sel_k), feat_major(cache_sel_v),
                                  feat_major(cache_win_k), feat_major(cache_win_v))
    y_s = _finish(xs, ret_yT, y_row.reshape(DB, NSA_WIDTH), *tail, DB, ret_t=True)
    new5 = lambda t: t.reshape(NSA_KV_HEADS, HEAD_DIM, DB).transpose(2, 0, 1).reshape(1, DB, 1, NSA_KV_HEADS, HEAD_DIM)
    outs_s = (y_s.reshape(DB, 1, D_MODEL), rs_s.transpose(3, 0, 1, 2)[None],
              new5(ckT), new5(cvT), new5(skT), new5(svT), kv5(nwk), kv5(nwv))

    return (outs_p[0], outs_s[0]) + outs_p[1:] + outs_s[1:]
```

```python
import functools

import numpy as np
import jax
import jax.numpy as jnp
from jax import lax
from jax.experimental import pallas as pl
from jax.experimental.pallas import tpu as pltpu

F32 = jnp.float32
BF16 = jnp.bfloat16
I32 = jnp.int32

D_MODEL = 1024
HEAD_DIM = 64
RET_WIDTH = 512
RET_HEADS = 8
NSA_WIDTH = 512
NSA_HEADS = 8
NSA_KV_HEADS = 2
NSA_GROUP = 4
KV_WIDTH = 128
N_BRANCH = 3
D_FF = 4 * D_MODEL
CMP_BLOCK = 64
N_SELECT = 16
N_PICK = N_SELECT - 1
WINDOW = 512
PAGE_SIZE = 128
RET_CHUNK = 128
ROPE_BASE = 10000.0
EPS = 1e-6
NEG = -1e30
SCALE = HEAD_DIM ** -0.5

MAIN_WIDTH = 4 * RET_WIDTH + NSA_WIDTH + 6 * KV_WIDTH
GATE_PAD = 128
ROPE_TABLE_WIDTH = 128
ATT_Q = 128
ATT_K = 256
MAX_BLOCKS = 64
CMP_SAMPLES = 2
SUBLANES = 8
PAGE_TILES = KV_WIDTH // SUBLANES
VMEM_LIMIT = 56 * 1024 * 1024

_LOG_GAMMA = [float(np.log1p(-np.exp2(-5.0 - h))) for h in range(RET_HEADS)]

_KV_NAMES = ("ck", "cv", "sk", "sv", "wk", "wv")
_COLUMNS = {}
_off = 0
for _name, _width in ((("rq", RET_WIDTH), ("rk", RET_WIDTH), ("rv", RET_WIDTH), ("rg", RET_WIDTH),
                       ("nq", NSA_WIDTH)) + tuple((n, KV_WIDTH) for n in _KV_NAMES)):
    _COLUMNS[_name] = (_off, _off + _width)
    _off += _width
assert _off == MAIN_WIDTH


def _params(*sem):
    return pltpu.CompilerParams(dimension_semantics=sem, vmem_limit_bytes=VMEM_LIMIT)


def _dot(a, b):
    return jnp.dot(a, b, preferred_element_type=F32)


def _dot_nt(a, b):
    return lax.dot_general(a, b, (((1,), (1,)), ((), ())), preferred_element_type=F32)


def _dot_tn(a, b):
    return lax.dot_general(a, b, (((0,), (0,)), ((), ())), preferred_element_type=F32)


def _rms(x, w):
    return x * lax.rsqrt(jnp.mean(x * x, axis=-1, keepdims=True) + EPS) * w


def _projector(x_ref, lnw_ref, w_ref, cos_ref, sin_ref):
    h = _rms(x_ref[...], lnw_ref[...]).astype(BF16)
    rows = h.shape[0]

    acc = _dot(h, w_ref[...])

    def proj(name):
        a, b = _COLUMNS[name]
        return acc[:, a:b]

    cos = jnp.concatenate([cos_ref[...]] * (RET_WIDTH // ROPE_TABLE_WIDTH), axis=1)
    sin = jnp.concatenate([sin_ref[...]] * (RET_WIDTH // ROPE_TABLE_WIDTH), axis=1)
    lane = lax.broadcasted_iota(I32, (rows, RET_WIDTH), 1)
    first_half = (lane % HEAD_DIM) < (HEAD_DIM // 2)

    def rope(y):
        partner = jnp.where(first_half,
                            pltpu.roll(y, RET_WIDTH - HEAD_DIM // 2, 1),
                            pltpu.roll(y, HEAD_DIM // 2, 1))
        return y * cos + partner * sin

    return h, proj, rope


def _rope_tables(pos):
    half = HEAD_DIM // 2
    inv = ROPE_BASE ** (-jnp.arange(half, dtype=F32) / half)
    ang = pos.astype(F32)[:, None] * inv
    cos, sin = jnp.cos(ang), jnp.sin(ang)
    cos_h = jnp.concatenate([cos, cos], -1)
    sin_h = jnp.concatenate([-sin, sin], -1)
    reps = ROPE_TABLE_WIDTH // HEAD_DIM
    return jnp.tile(cos_h, (1, reps)), jnp.tile(sin_h, (1, reps))


def _inproj_prompt_kernel(x_ref, lnw_ref, w_ref, wg_ref, cos_ref, sin_ref,
                          rq_ref, rk_ref, rv_ref, rg_ref, nqT_ref, ck_ref, cv_ref, sk_ref, wk_ref,
                          ckT_ref, cvT_ref, skT_ref, svT_ref, wkT_ref, wvT_ref, svTb_ref, wvTb_ref, gtT_ref):
    h, proj, rope = _projector(x_ref, lnw_ref, w_ref, cos_ref, sin_ref)
    rq_ref[...] = rope(proj("rq")).astype(BF16)
    rk_ref[...] = (rope(proj("rk")) * SCALE).astype(BF16)
    rv_ref[...] = proj("rv").astype(BF16)
    rg_ref[...] = proj("rg")
    nq = proj("nq") * SCALE
    for n in range(NSA_WIDTH // KV_WIDTH):
        nqT_ref[0, n * KV_WIDTH:(n + 1) * KV_WIDTH, :] = nq[:, n * KV_WIDTH:(n + 1) * KV_WIDTH].T.astype(BF16)
    kv = [proj(name) for name in _KV_NAMES]
    ck_ref[...] = kv[0]
    cv_ref[...] = kv[1]
    sk_ref[...] = kv[2].astype(BF16)
    wk_ref[...] = kv[4].astype(BF16)
    for n, ref in enumerate((ckT_ref, cvT_ref, skT_ref, svT_ref, wkT_ref, wvT_ref)):
        ref[0] = kv[n].T
    svTb_ref[0] = kv[3].T.astype(BF16)
    wvTb_ref[0] = kv[5].T.astype(BF16)
    gtT_ref[0] = jax.nn.sigmoid(_dot(h, wg_ref[...])).T


def _inproj_prompt(x2d, lnw, w_main, w_gate, cos_t, sin_t, B, T, tile):
    rows = x2d.shape[0]
    nt = T // tile
    row = lambda i: (i, 0)
    const = lambda i: (0, 0)
    tab = lambda i: (i % nt, 0)
    colT = lambda i: (i // nt, 0, i % nt)
    rm = lambda w, dt: (pl.BlockSpec((tile, w), row), jax.ShapeDtypeStruct((rows, w), dt))
    tr = lambda w, dt: (pl.BlockSpec((1, w, tile), colT), jax.ShapeDtypeStruct((B, w, T), dt))
    outs = [rm(RET_WIDTH, BF16), rm(RET_WIDTH, BF16), rm(RET_WIDTH, BF16), rm(RET_WIDTH, F32),
            tr(NSA_WIDTH, BF16), rm(KV_WIDTH, F32), rm(KV_WIDTH, F32), rm(KV_WIDTH, BF16), rm(KV_WIDTH, BF16),
            *[tr(KV_WIDTH, F32)] * 6,
            tr(KV_WIDTH, BF16), tr(KV_WIDTH, BF16), tr(GATE_PAD, F32)]
    return pl.pallas_call(
        _inproj_prompt_kernel,
        grid=(rows // tile,),
        in_specs=[pl.BlockSpec((tile, D_MODEL), row),
                  pl.BlockSpec((1, D_MODEL), const),
                  pl.BlockSpec((D_MODEL, MAIN_WIDTH), const),
                  pl.BlockSpec((D_MODEL, GATE_PAD), const),
                  pl.BlockSpec((tile, ROPE_TABLE_WIDTH), tab),
                  pl.BlockSpec((tile, ROPE_TABLE_WIDTH), tab)],
        out_specs=[o[0] for o in outs],
        out_shape=[o[1] for o in outs],
        compiler_params=_params("parallel"),
        name="inproj_prompt",
    )(x2d, lnw, w_main, w_gate, cos_t, sin_t)


def _inproj_sample_kernel(x_ref, lnw_ref, w_ref, wg_ref, cos_ref, sin_ref,
                          rqT_ref, rkT_ref, rvT_ref, rgT_ref, nq_ref,
                          ckT_ref, cvT_ref, skT_ref, svT_ref, sk_ref, sv_ref, wk_ref, wv_ref, gt_ref):
    h, proj, rope = _projector(x_ref, lnw_ref, w_ref, cos_ref, sin_ref)
    rqT_ref[...] = rope(proj("rq")).T
    rkT_ref[...] = (rope(proj("rk")) * SCALE).T
    rvT_ref[...] = proj("rv").T
    rgT_ref[...] = proj("rg").T
    nq_ref[...] = proj("nq") * SCALE
    kv = [proj(name) for name in _KV_NAMES]
    for n, ref in enumerate((ckT_ref, cvT_ref, skT_ref, svT_ref)):
        ref[...] = kv[n].T
    for n, ref in enumerate((sk_ref, sv_ref, wk_ref, wv_ref)):
        ref[...] = kv[2 + n]
    gt_ref[...] = jax.nn.sigmoid(_dot(h, wg_ref[...]))


def _inproj_sample(x2d, lnw, w_main, w_gate, cos_t, sin_t):
    rows = x2d.shape[0]
    full = lambda a, b: pl.BlockSpec((a, b), lambda i: (0, 0))
    rm = lambda w: (full(rows, w), jax.ShapeDtypeStruct((rows, w), F32))
    tr = lambda w: (full(w, rows), jax.ShapeDtypeStruct((w, rows), F32))
    outs = [tr(RET_WIDTH)] * 4 + [rm(NSA_WIDTH)] + [tr(KV_WIDTH)] * 4 + [rm(KV_WIDTH)] * 4 + [rm(GATE_PAD)]
    return pl.pallas_call(
        _inproj_sample_kernel,
        grid=(1,),
        in_specs=[full(rows, D_MODEL), full(1, D_MODEL), full(D_MODEL, MAIN_WIDTH), full(D_MODEL, GATE_PAD),
                  full(rows, ROPE_TABLE_WIDTH), full(rows, ROPE_TABLE_WIDTH)],
        out_specs=[o[0] for o in outs],
        out_shape=[o[1] for o in outs],
        compiler_params=_params("arbitrary"),
        name="inproj_sample",
    )(x2d, lnw, w_main, w_gate, cos_t, sin_t)


def _group_norm_gate(outs, g, gnw):
    ys = []
    for o in outs:
        mu = jnp.mean(o, axis=-1, keepdims=True)
        d = o - mu
        var = jnp.mean(d * d, axis=-1, keepdims=True)
        ys.append(d * lax.rsqrt(var + EPS))
    y = jnp.concatenate(ys, axis=1) * gnw
    return g * jax.nn.sigmoid(g) * y


def _ret_prompt_kernel(q_ref, k_ref, v_ref, g_ref, gnw_ref, inner_ref, qdec_ref, kdec_ref,
                       y_ref, s_out_ref, s_scr):
    c = pl.program_id(1)

    @pl.when(c == 0)
    def _():
        s_scr[...] = jnp.zeros_like(s_scr)

    qdec = qdec_ref[...]
    sls = [slice(h * HEAD_DIM, (h + 1) * HEAD_DIM) for h in range(RET_HEADS)]
    q = q_ref[0]
    k = k_ref[0]
    v = v_ref[0]
    kd = (k.astype(F32) * kdec_ref[...]).astype(BF16)
    att = [(_dot_nt(q[:, sl], k[:, sl]) * inner_ref[h]).astype(BF16) for h, sl in enumerate(sls)]
    cross = [_dot(q[:, sl], s_scr[h].astype(BF16)) * qdec[:, sl] for h, sl in enumerate(sls)]
    grow = [_dot_tn(kd[:, sl], v[:, sl]) for sl in sls]
    outs = [_dot(att[h], v[:, sl]) + cross[h] for h, sl in enumerate(sls)]
    for h in range(RET_HEADS):
        s_scr[h] = s_scr[h] * float(np.exp(_LOG_GAMMA[h] * RET_CHUNK)) + grow[h]
    y_ref[0] = _group_norm_gate(outs, g_ref[0], gnw_ref[...]).astype(BF16)

    @pl.when(c == pl.num_programs(1) - 1)
    def _():
        s_out_ref[0] = s_scr[...]


def _ret_prompt(rq, rk, rv, rg, gnw):
    B, T, _ = rq.shape
    C = RET_CHUNK
    lg = jnp.asarray(_LOG_GAMMA, F32)
    i = jnp.arange(C, dtype=F32)
    diff = i[:, None] - i[None, :]
    inner = jnp.where(diff >= 0, jnp.exp(lg[:, None, None] * jnp.maximum(diff, 0.0)), 0.0)
    qdec = jnp.repeat(jnp.exp(lg[None, :] * (i[:, None] + 1.0)), HEAD_DIM, axis=1)
    kdec = jnp.repeat(jnp.exp(lg[None, :] * (C - 1.0 - i[:, None])), HEAD_DIM, axis=1)
    blk = pl.BlockSpec((1, C, RET_WIDTH), lambda b, c: (b, c, 0))
    c2 = lambda b, c: (0, 0)
    return pl.pallas_call(
        _ret_prompt_kernel,
        grid=(B, T // C),
        in_specs=[blk, blk, blk, blk,
                  pl.BlockSpec((1, RET_WIDTH), c2),
                  pl.BlockSpec((RET_HEADS, C, C), lambda b, c: (0, 0, 0)),
                  pl.BlockSpec((C, RET_WIDTH), c2),
                  pl.BlockSpec((C, RET_WIDTH), c2)],
        out_specs=[blk, pl.BlockSpec((1, RET_HEADS, HEAD_DIM, HEAD_DIM), lambda b, c: (b, 0, 0, 0))],
        out_shape=[jax.ShapeDtypeStruct((B, T, RET_WIDTH), BF16),
                   jax.ShapeDtypeStruct((B, RET_HEADS, HEAD_DIM, HEAD_DIM), F32)],
        scratch_shapes=[pltpu.VMEM((RET_HEADS, HEAD_DIM, HEAD_DIM), F32)],
        compiler_params=_params("parallel", "arbitrary"),
        name="ret_prompt",
    )(rq, rk, rv, rg, gnw, inner, qdec, kdec)


def _ret_sample_kernel(q_ref, k_ref, v_ref, g_ref, gnw_ref, gam_ref, s_ref, y_ref, s_out_ref):
    q = q_ref[...]
    k = k_ref[...]
    v = v_ref[...]
    g = g_ref[...]
    gamma = gam_ref[0]
    acc = jnp.zeros(v.shape, F32)
    for d in range(HEAD_DIM):
        s_d = s_ref[0, d]
        acc = acc + s_d * q[d:d + 1, :]
        s_out_ref[0, d] = s_d * gamma + k[d:d + 1, :] * v
    o = acc * gamma + jnp.sum(q * k, axis=0, keepdims=True) * v
    mu = jnp.mean(o, axis=0, keepdims=True)
    dlt = o - mu
    var = jnp.mean(dlt * dlt, axis=0, keepdims=True)
    y_ref[...] = g * jax.nn.sigmoid(g) * (dlt * lax.rsqrt(var + EPS) * gnw_ref[...])


def _ret_sample(rqT, rkT, rvT, rgT, gnw, state_t):
    DB = rqT.shape[1]
    blk = pl.BlockSpec((HEAD_DIM, DB), lambda h: (h, 0))
    st = pl.BlockSpec((1, HEAD_DIM, HEAD_DIM, DB), lambda h: (h, 0, 0, 0))
    gam = jnp.broadcast_to(jnp.exp(jnp.asarray(_LOG_GAMMA, F32))[:, None, None], (RET_HEADS, 1, DB))
    return pl.pallas_call(
        _ret_sample_kernel,
        grid=(RET_HEADS,),
        in_specs=[blk, blk, blk, blk,
                  pl.BlockSpec((HEAD_DIM, 1), lambda h: (h, 0)),
                  pl.BlockSpec((1, 1, DB), lambda h: (h, 0, 0)),
                  st],
        out_specs=[blk, st],
        out_shape=[jax.ShapeDtypeStruct((RET_WIDTH, DB), F32),
                   jax.ShapeDtypeStruct(state_t.shape, F32)],
        compiler_params=_params("parallel"),
        name="ret_sample",
    )(rqT, rkT, rvT, rgT, gnw.reshape(RET_WIDTH, 1), gam, state_t)


def _compress(load_rows, pos_ref, w1_ref, w2_ref):
    acc = None
    for j2 in range(CMP_BLOCK // 2):
        x = jnp.concatenate([load_rows(2 * j2) + pos_ref[2 * j2:2 * j2 + 1, :],
                             load_rows(2 * j2 + 1) + pos_ref[2 * j2 + 1:2 * j2 + 2, :]], axis=1).astype(BF16)
        part = _dot(x, w1_ref[j2])
        acc = part if acc is None else acc + part
    h = jax.nn.gelu(acc)
    return _dot(h.astype(BF16), w2_ref[...])


def _block_diag2(w):
    z = jnp.zeros_like(w)
    return jnp.concatenate([jnp.concatenate([w, z], -1), jnp.concatenate([z, w], -1)], -2)


def _compress_weights(pos, w1, w2, token_major):
    if not token_major:
        pos, w1 = pos.T, w1.transpose(1, 0, 2)
    pos2 = jnp.concatenate([pos, pos], -1)
    w1p = _block_diag2(w1).reshape(CMP_BLOCK // 2, 2 * KV_WIDTH, KV_WIDTH)
    return pos2, w1p.astype(BF16), _block_diag2(w2).astype(BF16)


def _rank_rows(imp, rank_scr, last_live):
    nb, tq = imp.shape
    sub = SUBLANES
    n_groups = nb // sub
    groups = [imp[sub * j:sub * (j + 1)] for j in range(n_groups)]
    row_in_group = lax.broadcasted_iota(I32, (sub, tq), 0)
    rank_scr[...] = jnp.zeros((nb, tq), I32)
    for jm in range(n_groups):
        @pl.when(jm * sub <= last_live)
        def _():
            ranks = [rank_scr[sub * j:sub * (j + 1), :] for j in range(n_groups)]
            for sm in range(sub):
                row = imp[jm * sub + sm:jm * sub + sm + 1, :]
                for j, grp in enumerate(groups):
                    if j < jm:
                        inc = jnp.where(row > grp, 1, 0)
                    elif j > jm:
                        inc = jnp.where(row >= grp, 1, 0)
                    else:
                        inc = jnp.where(row_in_group > sm, jnp.where(row >= grp, 1, 0),
                                        jnp.where(row > grp, 1, 0))
                    ranks[j] = ranks[j] + inc
            for j in range(n_groups):
                rank_scr[sub * j:sub * (j + 1), :] = ranks[j]
    return rank_scr[...]


def _kv_head_rows(x, g):
    z = jnp.zeros_like(x)
    return jnp.concatenate([x, z] if g == 0 else [z, x], axis=0)


def _cmp_prompt_kernel(ck_ref, cv_ref, posk_ref, w1k_ref, w2k_ref, posv_ref, w1v_ref, w2v_ref,
                       nqT_ref, gtT_ref, ocmpT_ref, biasT_ref, kc_scr, vc_scr, rank_scr):
    i = pl.program_id(1)
    nb = kc_scr.shape[0]
    tq = nqT_ref.shape[2]

    @pl.when(i == 0)
    def _():
        kc_scr[...] = _compress(lambda l: ck_ref[0, pl.ds(l, nb, stride=CMP_BLOCK), :],
                                posk_ref, w1k_ref, w2k_ref).astype(BF16)
        vc_scr[...] = _compress(lambda l: cv_ref[0, pl.ds(l, nb, stride=CMP_BLOCK), :],
                                posv_ref, w1v_ref, w2v_ref).astype(BF16)

    kc = kc_scr[...]
    vc = vc_scr[...]
    pos = i * tq + lax.broadcasted_iota(I32, (nb, tq), 1)
    n_idx = lax.broadcasted_iota(I32, (nb, tq), 0)
    valid = (n_idx + 1) * CMP_BLOCK - 1 <= pos
    cur = pos // CMP_BLOCK
    cand = n_idx < cur
    for g in range(NSA_KV_HEADS):
        qT = jnp.concatenate(
            [_kv_head_rows(nqT_ref[0, (g * NSA_GROUP + r) * HEAD_DIM:(g * NSA_GROUP + r + 1) * HEAD_DIM, :], g)
             for r in range(NSA_GROUP)], axis=1)
        valid4 = jnp.concatenate([valid] * NSA_GROUP, axis=1)
        s = jnp.where(valid4, _dot(kc, qT), NEG)
        e = jnp.exp(s - jnp.max(s, axis=0, keepdims=True))
        p = jnp.where(valid4, e / jnp.sum(e, axis=0, keepdims=True), 0.0)
        oT = _dot_tn(vc, p.astype(BF16))[g * HEAD_DIM:(g + 1) * HEAD_DIM]
        imp = jnp.zeros((nb, tq), F32)
        for r in range(NSA_GROUP):
            hh = g * NSA_GROUP + r
            imp = imp + p[:, r * tq:(r + 1) * tq]
            ocmpT_ref[0, hh * HEAD_DIM:(hh + 1) * HEAD_DIM, :] = (
                oT[:, r * tq:(r + 1) * tq] * gtT_ref[0, N_BRANCH * hh:N_BRANCH * hh + 1, :])
        imp = jnp.where(cand, imp, -1.0)
        rank = _rank_rows(imp, rank_scr, (i * tq + tq - 1) // CMP_BLOCK)
        sel = (cand & (rank < N_PICK)) | (n_idx == cur)
        biasT_ref[0, g * MAX_BLOCKS:g * MAX_BLOCKS + nb, :] = jnp.where(sel, 0.0, NEG).astype(BF16)
        if nb < MAX_BLOCKS:
            biasT_ref[0, g * MAX_BLOCKS + nb:(g + 1) * MAX_BLOCKS, :] = jnp.zeros((MAX_BLOCKS - nb, tq), BF16)


def _cmp_weight_specs(c2, c3):
    return [pl.BlockSpec((CMP_BLOCK, KV_WIDTH), c2),
            pl.BlockSpec((CMP_BLOCK // 2, 2 * KV_WIDTH, KV_WIDTH), c3),
            pl.BlockSpec((KV_WIDTH, KV_WIDTH), c2)]


def _cmp_prompt(ck, cv, cwk, cwv, nqT, gtT, tq):
    B, T, _ = ck.shape
    nb = T // CMP_BLOCK
    full = pl.BlockSpec((1, T, KV_WIDTH), lambda b, i: (b, 0, 0))
    wspecs = _cmp_weight_specs(lambda b, i: (0, 0), lambda b, i: (0, 0, 0))
    blk = lambda w: pl.BlockSpec((1, w, tq), lambda b, i: (b, 0, i))
    return pl.pallas_call(
        _cmp_prompt_kernel,
        grid=(B, T // tq),
        in_specs=[full, full] + wspecs + wspecs + [blk(NSA_WIDTH), blk(GATE_PAD)],
        out_specs=[blk(NSA_WIDTH), blk(NSA_KV_HEADS * MAX_BLOCKS)],
        out_shape=[jax.ShapeDtypeStruct((B, NSA_WIDTH, T), F32),
                   jax.ShapeDtypeStruct((B, NSA_KV_HEADS * MAX_BLOCKS, T), BF16)],
        scratch_shapes=[pltpu.VMEM((nb, KV_WIDTH), BF16), pltpu.VMEM((nb, KV_WIDTH), BF16),
                        pltpu.VMEM((nb, tq), I32)],
        compiler_params=_params("parallel", "arbitrary"),
        name="cmp_prompt",
    )(ck, cv, *cwk, *cwv, nqT, gtT)


def _att_prompt_kernel(nqT_ref, biasT_ref, sk_ref, svT_ref, wk_ref, wvT_ref, gtT_ref, ocmpT_ref,
                       y_ref, ka_scr, va_scr, wva_scr, s_scr, m_scr, acc_scr):
    i = pl.program_id(1)
    tq = nqT_ref.shape[2]
    T = sk_ref.shape[1]
    cols = NSA_GROUP * tq

    @pl.when(i == 0)
    def _():
        sk = sk_ref[0].astype(F32)
        lane = lax.broadcasted_iota(I32, (T, KV_WIDTH), 1)
        blk = lax.broadcasted_iota(I32, (T, KV_WIDTH), 0) // CMP_BLOCK
        ka_scr[0] = jnp.where(lane < HEAD_DIM, sk, jnp.where(lane - HEAD_DIM == blk, 1.0, 0.0)).astype(BF16)
        ka_scr[1] = jnp.where(lane >= HEAD_DIM, sk, jnp.where(lane == blk, 1.0, 0.0)).astype(BF16)
        svT = svT_ref[0].astype(F32)
        row = lax.broadcasted_iota(I32, (KV_WIDTH, T), 0)
        va_scr[0] = jnp.where(row < HEAD_DIM, svT, 1.0).astype(BF16)
        va_scr[1] = jnp.where(row >= HEAD_DIM, svT, 1.0).astype(BF16)
        wvT_all = wvT_ref[0].astype(F32)
        wva_scr[0] = jnp.where(row < HEAD_DIM, wvT_all, 1.0).astype(BF16)
        wva_scr[1] = jnp.where(row >= HEAD_DIM, wvT_all, 1.0).astype(BF16)

    q_pos = i * tq + lax.broadcasted_iota(I32, (ATT_K, cols), 1) % tq
    k_off = lax.broadcasted_iota(I32, (ATT_K, cols), 0)
    n_win = WINDOW // tq
    win_keys = (n_win + 1) * tq
    win_start = pl.multiple_of(jnp.maximum(i - n_win, 0) * tq, tq)
    rel = i * tq - win_start + lax.broadcasted_iota(I32, (win_keys, tq), 1)
    j_win = lax.broadcasted_iota(I32, (win_keys, tq), 0)
    win_bias = jnp.where((j_win <= rel) & (j_win > rel - WINDOW), 0.0, NEG)
    win_bias = jnp.concatenate([win_bias] * NSA_GROUP, axis=1)
    wk = wk_ref[0, pl.ds(win_start, win_keys), :]

    qaTs, qwTs = [], []
    for g in range(NSA_KV_HEADS):
        qTs = [nqT_ref[0, (g * NSA_GROUP + r) * HEAD_DIM:(g * NSA_GROUP + r + 1) * HEAD_DIM, :]
               for r in range(NSA_GROUP)]
        bT = biasT_ref[0, g * MAX_BLOCKS:(g + 1) * MAX_BLOCKS, :]
        qaTs.append(jnp.concatenate([jnp.concatenate([q, bT] if g == 0 else [bT, q], axis=0) for q in qTs],
                                    axis=1))
        qwTs.append(jnp.concatenate([_kv_head_rows(q, g) for q in qTs], axis=1))

    def scores_into(slot, c):
        start = pl.multiple_of(c * ATT_K, ATT_K)
        for g in range(NSA_KV_HEADS):
            s_scr[slot, g] = _dot(ka_scr[g, pl.ds(start, ATT_K), :], qaTs[g])

    def sel_step(c, slot, last):
        start = pl.multiple_of(c * ATT_K, ATT_K)
        if not last:
            scores_into(1 - slot, c + 1)
        for g in range(NSA_KV_HEADS):
            s = s_scr[slot, g]
            if last:
                s = jnp.where(start + k_off > q_pos, NEG, s)
            m = m_scr[g]
            m_new = jnp.maximum(m, jnp.max(s, axis=0, keepdims=True))
            m_scr[g] = m_new
            p = jnp.exp(s - m_new)
            pv = _dot(va_scr[g, :, pl.ds(start, ATT_K)], p.astype(BF16))
            acc_scr[g] = jnp.exp(m - m_new) * acc_scr[g] + pv

    m_scr[...] = jnp.full(m_scr.shape, NEG, F32)
    acc_scr[...] = jnp.zeros(acc_scr.shape, F32)
    scores_into(0, 0)
    n_before = (i * tq) // ATT_K

    def pair(pr, carry):
        sel_step(2 * pr, 0, False)
        sel_step(2 * pr + 1, 1, False)
        return carry

    lax.fori_loop(0, n_before // 2, pair, 0)

    @pl.when(n_before % 2 == 1)
    def _():
        sel_step(n_before - 1, 0, False)

    sel_step(n_before, n_before % 2, True)

    heads_out = []
    for g in range(NSA_KV_HEADS):
        gs = slice(g * HEAD_DIM, (g + 1) * HEAD_DIM)
        acc = acc_scr[g]
        ones_row = (1 - g) * HEAD_DIM
        o_sel = acc[gs] / acc[ones_row:ones_row + 1]

        s = _dot(wk, qwTs[g]) + win_bias
        p = jnp.exp(s - jnp.max(s, axis=0, keepdims=True))
        acc_w = _dot(wva_scr[g, :, pl.ds(win_start, win_keys)], p.astype(BF16))
        o_win = acc_w[gs] / acc_w[ones_row:ones_row + 1]

        for r in range(NSA_GROUP):
            hh = g * NSA_GROUP + r
            cs = slice(r * tq, (r + 1) * tq)
            heads_out.append(ocmpT_ref[0, hh * HEAD_DIM:(hh + 1) * HEAD_DIM, :]
                             + gtT_ref[0, N_BRANCH * hh + 1:N_BRANCH * hh + 2, :] * o_sel[:, cs]
                             + gtT_ref[0, N_BRANCH * hh + 2:N_BRANCH * hh + 3, :] * o_win[:, cs])
    y_ref[0] = jnp.concatenate(heads_out, axis=0).T.astype(BF16)


def _att_prompt(nqT, biasT, sk, svT, wk, wvT, gtT, ocmpT):
    B, _, T = nqT.shape
    tq = ATT_Q
    rows_full = pl.BlockSpec((1, T, KV_WIDTH), lambda b, i: (b, 0, 0))
    cols_full = pl.BlockSpec((1, KV_WIDTH, T), lambda b, i: (b, 0, 0))
    blk = lambda w: pl.BlockSpec((1, w, tq), lambda b, i: (b, 0, i))
    return pl.pallas_call(
        _att_prompt_kernel,
        grid=(B, T // tq),
        in_specs=[blk(NSA_WIDTH), blk(NSA_KV_HEADS * MAX_BLOCKS), rows_full, cols_full, rows_full, cols_full,
                  blk(GATE_PAD), blk(NSA_WIDTH)],
        out_specs=pl.BlockSpec((1, tq, NSA_WIDTH), lambda b, i: (b, i, 0)),
        out_shape=jax.ShapeDtypeStruct((B, T, NSA_WIDTH), BF16),
        scratch_shapes=[pltpu.VMEM((NSA_KV_HEADS, T, KV_WIDTH), BF16),
                        pltpu.VMEM((NSA_KV_HEADS, KV_WIDTH, T), BF16),
                        pltpu.VMEM((NSA_KV_HEADS, KV_WIDTH, T), BF16),
                        pltpu.VMEM((2, NSA_KV_HEADS, ATT_K, NSA_GROUP * tq), F32),
                        pltpu.VMEM((NSA_KV_HEADS, 1, NSA_GROUP * tq), F32),
                        pltpu.VMEM((NSA_KV_HEADS, KV_WIDTH, NSA_GROUP * tq), F32)],
        compiler_params=_params("parallel", "arbitrary"),
        name="att_prompt",
    )(nqT, biasT, sk, svT, wk, wvT, gtT, ocmpT)


def _finish_kernel(x_ref, ry_ref, ny_ref, wo_ref, l1_ref, l2_ref, l3_ref, wu_ref, wd_ref, o_ref, *, ret_t):
    if ret_t:
        mix = _dot_tn(ry_ref[...].astype(BF16), wo_ref[0:RET_WIDTH, :])
    else:
        mix = _dot(ry_ref[...], wo_ref[0:RET_WIDTH, :])
    mix = mix + _dot(ny_ref[...].astype(BF16), wo_ref[RET_WIDTH:, :])
    h = x_ref[...] + _rms(mix, l1_ref[...])
    u = _dot(_rms(h, l2_ref[...]).astype(BF16), wu_ref[...])
    a = jnp.square(jnp.maximum(u, 0.0)).astype(BF16)
    f = _dot(a, wd_ref[...])
    o_ref[...] = h + _rms(f, l3_ref[...])


def _finish(x2d, ret_y, nsa_y, w_o, l1, l2, l3, w_up, w_down, tile, ret_t=False):
    rows = x2d.shape[0]
    row = lambda w: pl.BlockSpec((tile, w), lambda i: (i, 0))
    const = lambda a, b: pl.BlockSpec((a, b), lambda i: (0, 0), pipeline_mode=pl.Buffered(1))
    ret_spec = pl.BlockSpec((RET_WIDTH, tile), lambda i: (0, i)) if ret_t else row(RET_WIDTH)
    return pl.pallas_call(
        functools.partial(_finish_kernel, ret_t=ret_t),
        grid=(rows // tile,),
        in_specs=[row(D_MODEL), ret_spec, row(NSA_WIDTH),
                  const(D_MODEL, D_MODEL), const(1, D_MODEL), const(1, D_MODEL), const(1, D_MODEL),
                  const(D_MODEL, D_FF), const(D_FF, D_MODEL)],
        out_specs=row(D_MODEL),
        out_shape=jax.ShapeDtypeStruct((rows, D_MODEL), F32),
        compiler_params=_params("parallel"),
        name="finish",
    )(x2d, ret_y, nsa_y, w_o, l1, l2, l3, w_up, w_down)


def _column(row):
    n = row.shape[1]
    eye = lax.broadcasted_iota(I32, (n, n), 0) == lax.broadcasted_iota(I32, (n, n), 1)
    return jnp.sum(jnp.where(eye, row, 0.0), axis=1, keepdims=True)


def _cmp_sample_kernel(pt_ref, nq_ref, poolk_ref, poolv_ref,
                       posk_ref, w1k_ref, w2k_ref, posv_ref, w1v_ref, w2v_ref,
                       ocmp_ref, idx_ref, kbuf, vbuf, ksem, vsem):
    step = pl.program_id(0)
    n_pages = pt_ref.shape[1]
    nbk = 2 * n_pages
    slot = step % 2

    def fetch(st, sl, start):
        for j in range(CMP_SAMPLES):
            for p in range(n_pages):
                tiles = pl.ds(((sl * CMP_SAMPLES + j) * n_pages + p) * PAGE_TILES, PAGE_TILES)
                page = pt_ref[st * CMP_SAMPLES + j, p]
                for pool, buf, sem in ((poolk_ref, kbuf, ksem), (poolv_ref, vbuf, vsem)):
                    cp = pltpu.make_async_copy(pool.at[page], buf.at[tiles], sem.at[sl])
                    cp.start() if start else cp.wait()

    @pl.when(step == 0)
    def _():
        fetch(0, 0, True)

    @pl.when(step + 1 < pl.num_programs(0))
    def _():
        fetch(step + 1, 1 - slot, True)

    fetch(step, slot, False)

    step_pages = CMP_SAMPLES * n_pages
    base = slot * step_pages * PAGE_TILES
    group_tiles = HEAD_DIM // SUBLANES

    def rows_of(buf):
        cache = {}

        def load(d):
            fg, s = divmod(d, SUBLANES)
            if fg not in cache:
                cache.clear()
                cache[fg] = [jnp.swapaxes(buf[pl.ds(base + g * group_tiles + fg, step_pages, stride=PAGE_TILES)],
                                          0, 1) for g in range(NSA_KV_HEADS)]
            return jnp.concatenate([t[s] for t in cache[fg]], axis=0)
        return load

    kc = _compress(rows_of(kbuf), posk_ref, w1k_ref, w2k_ref).astype(BF16)
    vc = _compress(rows_of(vbuf), posv_ref, w1v_ref, w2v_ref).astype(BF16)

    j_lane = lax.broadcasted_iota(I32, (nbk, nbk), 1)
    j_sub = lax.broadcasted_iota(I32, (nbk, nbk), 0)
    n_lane = 2 * (j_lane % n_pages) + j_lane // n_pages
    n_sub = 2 * (j_sub % n_pages) + j_sub // n_pages
    z = jnp.zeros((1, HEAD_DIM), F32)
    for j in range(CMP_SAMPLES):
        q = nq_ref[j]
        o_rows, idx_rows = [], []
        for g in range(NSA_KV_HEADS):
            blk_rows = slice((g * CMP_SAMPLES + j) * n_pages, (g * CMP_SAMPLES + j + 1) * n_pages)
            pieces = [q[:, (g * NSA_GROUP + r) * HEAD_DIM:(g * NSA_GROUP + r + 1) * HEAD_DIM]
                      for r in range(NSA_GROUP)]
            qh = jnp.concatenate([jnp.concatenate([pc, z], axis=1) for pc in pieces]
                                 + [jnp.concatenate([z, pc], axis=1) for pc in pieces], axis=0)
            s = _dot_nt(qh.astype(BF16), kc[blk_rows])
            m = jnp.max(s, axis=1, keepdims=True)
            m = jnp.maximum(m[:NSA_GROUP], m[NSA_GROUP:])
            e = jnp.exp(s - jnp.concatenate([m, m], axis=0))
            l = jnp.sum(e, axis=1, keepdims=True)
            l = l[:NSA_GROUP] + l[NSA_GROUP:]
            p = e / jnp.concatenate([l, l], axis=0)
            out = _dot(p.astype(BF16), vc[blk_rows])
            o_rows.append(out[:NSA_GROUP, :HEAD_DIM] + out[NSA_GROUP:, HEAD_DIM:])
            imp = jnp.concatenate([jnp.sum(p[:NSA_GROUP], axis=0, keepdims=True),
                                   jnp.sum(p[NSA_GROUP:], axis=0, keepdims=True)], axis=1)
            imp_col = _column(imp)
            ahead = (imp > imp_col) | ((imp == imp_col) & (n_lane < n_sub))
            rank_col = jnp.sum(jnp.where(ahead, 1, 0), axis=1, keepdims=True)
            idx_rows.append(jnp.sum(jnp.where(rank_col == j_lane, n_sub, 0), axis=0, keepdims=True))
        ocmp_ref[j] = jnp.concatenate([o[r:r + 1] for o in o_rows for r in range(NSA_GROUP)], axis=1)
        idx_ref[j] = jnp.concatenate(idx_rows, axis=0)


def _cmp_sample(page_table, nq, pool_k, pool_v, cwk, cwv):
    DB, n_pages = page_table.shape
    nbk = 2 * n_pages
    wspecs = _cmp_weight_specs(lambda s, pt: (0, 0), lambda s, pt: (0, 0, 0))
    any_spec = pl.BlockSpec(memory_space=pl.ANY)
    tiled = lambda t: t.reshape(t.shape[0], PAGE_TILES, SUBLANES, PAGE_SIZE)
    buf_shape = (2 * CMP_SAMPLES * n_pages * PAGE_TILES, SUBLANES, PAGE_SIZE)
    grid_spec = pltpu.PrefetchScalarGridSpec(
        num_scalar_prefetch=1,
        grid=(DB // CMP_SAMPLES,),
        in_specs=[pl.BlockSpec((CMP_SAMPLES, 1, NSA_WIDTH), lambda s, pt: (s, 0, 0)), any_spec, any_spec]
        + wspecs + wspecs,
        out_specs=[pl.BlockSpec((CMP_SAMPLES, 1, NSA_WIDTH), lambda s, pt: (s, 0, 0)),
                   pl.BlockSpec((CMP_SAMPLES, NSA_KV_HEADS, nbk), lambda s, pt: (s, 0, 0))],
        scratch_shapes=[pltpu.VMEM(buf_shape, F32),
                        pltpu.VMEM(buf_shape, F32),
                        pltpu.SemaphoreType.DMA((2,)),
                        pltpu.SemaphoreType.DMA((2,))],
    )
    pool_k, pool_v = tiled(pool_k), tiled(pool_v)
    return pl.pallas_call(
        _cmp_sample_kernel,
        grid_spec=grid_spec,
        out_shape=[jax.ShapeDtypeStruct((DB, 1, NSA_WIDTH), F32),
                   jax.ShapeDtypeStruct((DB, NSA_KV_HEADS, nbk), I32)],
        compiler_params=_params("arbitrary"),
        name="cmp_sample",
    )(page_table, nq.reshape(DB, 1, NSA_WIDTH), pool_k, pool_v, *cwk, *cwv)


def _att_sample_kernel(idx_ref, pt_ref, nq_ref, knew_ref, vnew_ref, wknew_ref, wvnew_ref,
                       gt_ref, ocmp_ref, poolk_ref, poolv_ref, wkT_ref, wvT_ref,
                       y_ref, wk_out_ref, wv_out_ref, kbuf, vbuf, ksem, vsem, *, past_len):
    b = pl.program_id(0)
    slot = b % 2

    def gather(bb, sl, start):
        for g in range(NSA_KV_HEADS):
            for s in range(N_PICK):
                blk = idx_ref[(bb * NSA_KV_HEADS + g) * N_SELECT + s]
                page = pt_ref[bb, blk // 2]
                for pool, buf, sem in ((poolk_ref, kbuf, ksem), (poolv_ref, vbuf, vsem)):
                    cp = pltpu.make_async_copy(pool.at[page, pl.ds(g * HEAD_DIM, HEAD_DIM), :],
                                               buf.at[sl, g, :, pl.ds(s * PAGE_SIZE, PAGE_SIZE)], sem.at[sl])
                    cp.start() if start else cp.wait()

    @pl.when(b == 0)
    def _():
        gather(0, 0, True)

    @pl.when(b + 1 < pl.num_programs(0))
    def _():
        gather(b + 1, 1 - slot, True)

    q = nq_ref[0]
    pieces = [q[:, hh * HEAD_DIM:(hh + 1) * HEAD_DIM] for hh in range(NSA_HEADS)]
    z = jnp.zeros((1, HEAD_DIM), F32)
    q_wide = jnp.concatenate([jnp.concatenate([pc, z] if hh < NSA_GROUP else [z, pc], axis=1)
                              for hh, pc in enumerate(pieces)], axis=0)

    def bf_round(x):
        return x.astype(BF16).astype(F32)

    def attend(qf, s_past, vT_past, k_new, v_new):
        s_self = jnp.sum(bf_round(qf) * bf_round(k_new), axis=1, keepdims=True)
        m = jnp.maximum(jnp.max(s_past, axis=-1, keepdims=True), s_self)
        p = jnp.exp(s_past - m)
        p_self = jnp.exp(s_self - m)
        l = jnp.sum(p, axis=-1, keepdims=True) + p_self
        o = _dot_nt(p.astype(BF16), vT_past) + bf_round(p_self) * bf_round(v_new)
        return o / l

    wkT = wkT_ref[0]
    wvT = wvT_ref[0]
    wb = wkT.shape[1]
    j = lax.broadcasted_iota(I32, (NSA_HEADS, wb), 1)
    pos_k = past_len - wb + j
    keep = (past_len - pos_k < WINDOW) & (pos_k >= 0)
    s_win = jnp.where(keep, _dot(q_wide.astype(BF16), wkT.astype(BF16)), NEG)
    o_win = attend(q_wide, s_win, wvT.astype(BF16), wknew_ref[0], wvnew_ref[0])

    lane_w = lax.broadcasted_iota(I32, (KV_WIDTH, wb), 1)
    wk_out_ref[0] = jnp.where(lane_w == wb - 1, _column(wknew_ref[0]), pltpu.roll(wkT, wb - 1, 1))
    wv_out_ref[0] = jnp.where(lane_w == wb - 1, _column(wvnew_ref[0]), pltpu.roll(wvT, wb - 1, 1))

    gather(b, slot, False)
    n_keys = N_PICK * PAGE_SIZE
    half_of_lane = (lax.broadcasted_iota(I32, (1, n_keys), 1) % PAGE_SIZE) // CMP_BLOCK
    zeros4 = jnp.zeros((NSA_GROUP, HEAD_DIM), F32)
    gt = gt_ref[0]
    ocmp = ocmp_ref[0]
    y_pieces = []
    for g in range(NSA_KV_HEADS):
        gs = slice(g * HEAD_DIM, (g + 1) * HEAD_DIM)
        want = jnp.concatenate(
            [jnp.full((1, PAGE_SIZE), idx_ref[(b * NSA_KV_HEADS + g) * N_SELECT + s] % 2, I32)
             for s in range(N_PICK)], axis=1)
        qg = jnp.concatenate(pieces[g * NSA_GROUP:(g + 1) * NSA_GROUP] + [zeros4], axis=0)
        s_sel = jnp.where(half_of_lane == want, _dot(qg.astype(BF16), kbuf[slot, g].astype(BF16)), NEG)
        o_sel = attend(qg, s_sel, vbuf[slot, g].astype(BF16), knew_ref[0][:, gs], vnew_ref[0][:, gs])
        for r in range(NSA_GROUP):
            hh = g * NSA_GROUP + r
            y_pieces.append(gt[:, N_BRANCH * hh:N_BRANCH * hh + 1] * ocmp[:, hh * HEAD_DIM:(hh + 1) * HEAD_DIM]
                            + gt[:, N_BRANCH * hh + 1:N_BRANCH * hh + 2] * o_sel[r:r + 1]
                            + gt[:, N_BRANCH * hh + 2:N_BRANCH * hh + 3] * o_win[hh:hh + 1, gs])
    y_ref[0] = jnp.concatenate(y_pieces, axis=1)


def _att_sample(idx, page_table, nq, sk, sv, wk_new, wv_new, gates, ocmp, pool_k, pool_v, buf_k, buf_v):
    DB, n_pages = page_table.shape
    past_len = n_pages * PAGE_SIZE
    wb = buf_k.shape[2]
    r3 = lambda t: t.reshape(DB, 1, t.shape[-1])
    row = lambda w: pl.BlockSpec((1, 1, w), lambda b, ix, pt: (b, 0, 0))
    any_spec = pl.BlockSpec(memory_space=pl.ANY)
    wspec = pl.BlockSpec((1, KV_WIDTH, wb), lambda b, ix, pt: (b, 0, 0))
    buf_shape = (2, NSA_KV_HEADS, HEAD_DIM, N_PICK * PAGE_SIZE)
    grid_spec = pltpu.PrefetchScalarGridSpec(
        num_scalar_prefetch=2,
        grid=(DB,),
        in_specs=[row(NSA_WIDTH), row(KV_WIDTH), row(KV_WIDTH), row(KV_WIDTH), row(KV_WIDTH),
                  row(GATE_PAD), row(NSA_WIDTH), any_spec, any_spec, wspec, wspec],
        out_specs=[row(NSA_WIDTH), wspec, wspec],
        scratch_shapes=[pltpu.VMEM(buf_shape, F32), pltpu.VMEM(buf_shape, F32),
                        pltpu.SemaphoreType.DMA((2,)), pltpu.SemaphoreType.DMA((2,))],
    )
    return pl.pallas_call(
        functools.partial(_att_sample_kernel, past_len=past_len),
        grid_spec=grid_spec,
        out_shape=[jax.ShapeDtypeStruct((DB, 1, NSA_WIDTH), F32),
                   jax.ShapeDtypeStruct(buf_k.shape, F32),
                   jax.ShapeDtypeStruct(buf_v.shape, F32)],
        compiler_params=_params("arbitrary"),
        name="att_sample",
    )(idx, page_table, r3(nq), r3(sk), r3(sv), r3(wk_new), r3(wv_new), r3(gates), ocmp,
      pool_k, pool_v, buf_k, buf_v)


def _row_tile(rows, want=256):
    return next(t for t in (want, 256, 128) if rows % t == 0)


def kernel(x_prompt, x_sample, state_ret, cache_cmp_k, cache_cmp_v, cache_sel_k, cache_sel_v,
           cache_win_k, cache_win_v, page_table, ln_pre_mix, w_in, ret_gn_w,
           cmp_pos_k, cmp_w1_k, cmp_w2_k, cmp_pos_v, cmp_w1_v, cmp_w2_v,
           w_o, ln_post_mix, ln_pre_ffn, w_up, w_down, ln_post_ffn):
    B, T, _ = x_prompt.shape
    DB, Tn, _ = x_sample.shape
    n_pages = page_table.shape[1]
    past_len = n_pages * PAGE_SIZE
    wb = cache_win_k.shape[2]
    assert w_in.shape[0] == 1 and Tn == 1
    assert T % ATT_K == 0 and T >= WINDOW + ATT_Q and T // CMP_BLOCK <= MAX_BLOCKS
    assert 2 * n_pages >= N_PICK and DB % CMP_SAMPLES == 0

    w_main = w_in[0, :, :MAIN_WIDTH].astype(BF16)
    w_gate = jnp.pad(w_in[0, :, MAIN_WIDTH:], ((0, 0), (0, GATE_PAD - NSA_HEADS * N_BRANCH))).astype(BF16)
    tail = (w_o[0].astype(BF16), ln_post_mix, ln_pre_ffn, ln_post_ffn, w_up[0].astype(BF16), w_down[0].astype(BF16))

    tile_p = _row_tile(T, 512)
    cos_p, sin_p = _rope_tables(jnp.arange(T, dtype=I32))
    xp = x_prompt.reshape(B * T, D_MODEL)
    (rq, rk, rv, rg, nqT, ck, cv, sk, wk, ckT, cvT, skT, svT, wkT, wvT, svTb, wvTb, gtT) = _inproj_prompt(
        xp, ln_pre_mix, w_main, w_gate, cos_p, sin_p, B, T, tile_p)
    b3 = lambda t: t.reshape(B, T, t.shape[-1])
    ret_y, rs_p = _ret_prompt(b3(rq), b3(rk), b3(rv), b3(rg), ret_gn_w)
    cwk = _compress_weights(cmp_pos_k[0], cmp_w1_k[0], cmp_w2_k[0], True)
    cwv = _compress_weights(cmp_pos_v[0], cmp_w1_v[0], cmp_w2_v[0], True)
    ocmpT, biasT = _cmp_prompt(b3(ck), b3(cv), cwk, cwv, nqT, gtT, _row_tile(T, 512))
    nsa_y = _att_prompt(nqT, biasT, b3(sk), svTb, b3(wk), wvTb, gtT, ocmpT)
    y_p = _finish(xp, ret_y.reshape(B * T, RET_WIDTH), nsa_y.reshape(B * T, NSA_WIDTH), *tail,
                  _row_tile(B * T, 512))
    win_p = min(WINDOW, T)
    kv5 = lambda t: t.reshape(t.shape[0], NSA_KV_HEADS, HEAD_DIM, t.shape[-1]).transpose(0, 3, 1, 2)[None]
    outs_p = (y_p.reshape(B, T, D_MODEL), rs_p[None],
              kv5(ckT), kv5(cvT), kv5(skT), kv5(svT),
              kv5(wkT[:, :, T - win_p:]), kv5(wvT[:, :, T - win_p:]))

    cos_s, sin_s = _rope_tables(jnp.full((DB,), past_len, I32))
    xs = x_sample.reshape(DB, D_MODEL)
    (rqT, rkT, rvT, rgT, nq, ckT, cvT, skT, svT, sk, sv, wk, wv, gt) = _inproj_sample(
        xs, ln_pre_mix, w_main, w_gate, cos_s, sin_s)
    ret_yT, rs_s = _ret_sample(rqT, rkT, rvT, rgT, ret_gn_w, state_ret[0].transpose(1, 2, 3, 0))
    feat_major = lambda t: t[0].transpose(0, 2, 3, 1).reshape(t.shape[1], KV_WIDTH, t.shape[2])
    cwk = _compress_weights(cmp_pos_k[0], cmp_w1_k[0], cmp_w2_k[0], False)
    cwv = _compress_weights(cmp_pos_v[0], cmp_w1_v[0], cmp_w2_v[0], False)
    ocmp, idx = _cmp_sample(page_table, nq, feat_major(cache_cmp_k), feat_major(cache_cmp_v), cwk, cwv)
    idx_flat = idx[:, :, :N_SELECT].reshape(-1)
    y_row, nwk, nwv = _att_sample(idx_flat, page_table, nq, sk, sv, wk, wv, gt, ocmp,
                                  feat_major(cache_sel_k), feat_major(cache_sel_v),
                                  feat_major(cache_win_k), feat_major(cache_win_v))
    y_s = _finish(xs, ret_yT, y_row.reshape(DB, NSA_WIDTH), *tail, DB, ret_t=True)
    new5 = lambda t: t.reshape(NSA_KV_HEADS, HEAD_DIM, DB).transpose(2, 0, 1).reshape(1, DB, 1, NSA_KV_HEADS, HEAD_DIM)
    outs_s = (y_s.reshape(DB, 1, D_MODEL), rs_s.transpose(3, 0, 1, 2)[None],
              new5(ckT), new5(cvT), new5(skT), new5(svT), kv5(nwk), kv5(nwv))

    return (outs_p[0], outs_s[0]) + outs_p[1:] + outs_s[1:]
```
